```python
import math
import jax, jax.numpy as jnp
from jax import lax
import numpy as np

D_MODEL = 4096
BATCH = 2
SEQ = 8192
DEPTH = 2

N_MIXERS = 2
EPS = 1e-6
SGU_WIDTH = D_MODEL
SGU_CHUNK = 128
SGU_GROUP_DIM = 128
SGU_GROUPS = SGU_WIDTH // SGU_GROUP_DIM
DIL_CONFIGS = ((128, 1), (512, 4), (2048, 16))
DIL_GROUPS = len(DIL_CONFIGS)
DIL_HEADS = 16
HEAD_DIM = 128
DIL_BLOCK = 128
DIL_INNER = DIL_HEADS * HEAD_DIM
NUM_BUCKETS = 32
MAX_DISTANCE = 2048
N_BIAS_HEADS = DIL_GROUPS * DIL_HEADS
N_EXPERT_GROUPS = 4
EXPERTS_PER_GROUP = 8
N_EXPERTS = N_EXPERT_GROUPS * EXPERTS_PER_GROUP
TOP_K = 2
D_EXPERT = 512
MOE_BLOCK = 128

N_SGU_LAYERS = (DEPTH + 1) // 2
N_DIL_LAYERS = DEPTH // 2

kernel_name = "hybrid_sgu_dilated_attn_hier_moe"


def rms_norm(x, g):
    xf = x.astype(jnp.float32)
    y = xf * lax.rsqrt(jnp.mean(xf * xf, axis=-1, keepdims=True) + EPS)
    return (y * g.astype(jnp.float32)).astype(x.dtype)


def layer_norm(x, g, b):
    xf = x.astype(jnp.float32)
    mu = jnp.mean(xf, axis=-1, keepdims=True)
    var = jnp.mean(jnp.square(xf - mu), axis=-1, keepdims=True)
    y = (xf - mu) * lax.rsqrt(var + EPS)
    return (y * g.astype(jnp.float32) + b.astype(jnp.float32)).astype(x.dtype)


def sgu_mixer(h, w_in, b_in, v_gain, v_bias, w_s, b_s, w_out):
    B_, S_, _ = h.shape
    z = jax.nn.gelu(h @ w_in + b_in)
    u, v = jnp.split(z, 2, axis=-1)
    v = layer_norm(v, v_gain, v_bias)
    v = v.reshape(B_, S_ // SGU_CHUNK, SGU_CHUNK, SGU_GROUPS, SGU_GROUP_DIM)
    causal = jnp.tril(jnp.ones((SGU_CHUNK, SGU_CHUNK), dtype=bool))
    w_causal = jnp.where(causal[None], w_s, jnp.zeros((), w_s.dtype)).astype(v.dtype)
    vm = jnp.einsum('gts,bcsgd->bctgd', w_causal, v) + b_s.T[None, None, :, :, None].astype(v.dtype)
    y = u * vm.reshape(B_, S_, SGU_WIDTH)
    return y @ w_out


def t5_causal_bucket(dist):
    max_exact = NUM_BUCKETS // 2
    d = jnp.maximum(dist, 0)
    df = jnp.maximum(d, 1).astype(jnp.float32)
    large = max_exact + (jnp.log(df / max_exact) / math.log(MAX_DISTANCE / max_exact)
                         * (NUM_BUCKETS - max_exact)).astype(jnp.int32)
    large = jnp.minimum(large, NUM_BUCKETS - 1)
    return jnp.where(d < max_exact, d, large)


def dilated_group_attention(q, k, v, bias_table, window, dilation):
    B_, S_, H, Dh = q.shape
    L = S_ // dilation
    nb = -(-L // DIL_BLOCK)
    Lp = nb * DIL_BLOCK
    span = window // dilation

    def to_blocks(t):
        t = t.reshape(B_, L, dilation, H, Dh).transpose(0, 2, 1, 3, 4)
        t = jnp.pad(t, ((0, 0), (0, 0), (0, Lp - L), (0, 0), (0, 0)))
        return t.reshape(B_, dilation, nb, DIL_BLOCK, H, Dh)

    def with_prev(t):
        prev = jnp.pad(t, ((0, 0), (0, 0), (1, 0), (0, 0), (0, 0), (0, 0)))[:, :, :-1]
        return jnp.concatenate([prev, t], axis=3)

    qb = to_blocks(q)
    kw = with_prev(to_blocks(k))
    vw = with_prev(to_blocks(v))

    qi = jnp.arange(DIL_BLOCK)[:, None]
    kj = jnp.arange(2 * DIL_BLOCK)[None, :]
    sub_dist = qi + DIL_BLOCK - kj
    band = (sub_dist >= 0) & (sub_dist <= span)
    bias = bias_table[t5_causal_bucket(sub_dist * dilation)]
    bias = bias.transpose(2, 0, 1).astype(jnp.float32)
    key_valid = (jnp.arange(nb)[:, None] * DIL_BLOCK + kj - DIL_BLOCK) >= 0
    mask = band[None] & key_valid[:, None, :]

    logits = jnp.einsum('brnqhe,brnkhe->brnhqk', qb, kw,
                        preferred_element_type=jnp.float32) * (Dh ** -0.5) + bias
    logits = jnp.where(mask[None, None, :, None], logits, -1e30)
    m = jnp.max(logits, axis=-1, keepdims=True)
    p = jnp.exp(logits - m)
    den = jnp.sum(p, axis=-1)
    o = jnp.einsum('brnhqk,brnkhe->brnqhe', p, vw.astype(jnp.float32))
    o = o / den.transpose(0, 1, 2, 4, 3)[..., None]
    lse = (m[..., 0] + jnp.log(den)).transpose(0, 1, 2, 4, 3)

    o = o.reshape(B_, dilation, Lp, H, Dh)[:, :, :L].transpose(0, 2, 1, 3, 4).reshape(B_, S_, H, Dh)
    lse = lse.reshape(B_, dilation, Lp, H)[:, :, :L].transpose(0, 2, 1, 3).reshape(B_, S_, H)
    return o, lse


def dilated_mixer(h, w_qkv, w_out, rel_bias):
    B_, S_, _ = h.shape
    qkv = (h @ w_qkv).reshape(B_, S_, 3, DIL_GROUPS, DIL_HEADS, HEAD_DIM)
    outs, lses = [], []
    for g, (window, dilation) in enumerate(DIL_CONFIGS):
        o, lse = dilated_group_attention(qkv[:, :, 0, g], qkv[:, :, 1, g], qkv[:, :, 2, g],
                                         rel_bias[:, g * DIL_HEADS:(g + 1) * DIL_HEADS],
                                         window, dilation)
        outs.append(o)
        lses.append(lse)
    wts = jax.nn.softmax(jnp.stack(lses, axis=-1), axis=-1)
    o = jnp.einsum('bshgd,bshg->bshd', jnp.stack(outs, axis=3), wts)
    return o.reshape(B_, S_, DIL_INNER).astype(h.dtype) @ w_out


def hierarchical_moe(h, w_rg, b_rg, w_re, b_re, w_gate, w_up, w_down):
    B_, S_, D = h.shape
    N = B_ * S_
    hf = h.reshape(N, D)
    g_logits = (hf @ w_rg).astype(jnp.float32) + b_rg.astype(jnp.float32)
    g_prob = jax.nn.softmax(g_logits, axis=-1)
    g_idx = jnp.argmax(g_logits, axis=-1).astype(jnp.int32)
    g_w = jnp.take_along_axis(g_prob, g_idx[:, None], axis=-1)
    e_logits = ((hf @ w_re).astype(jnp.float32) + b_re.astype(jnp.float32)).reshape(
        N, N_EXPERT_GROUPS, EXPERTS_PER_GROUP)
    e_logits = jnp.take_along_axis(e_logits, g_idx[:, None, None], axis=1)[:, 0]
    top_vals, top_idx = lax.top_k(e_logits, TOP_K)
    gates = g_w * jax.nn.softmax(top_vals, axis=-1)
    expert_id = g_idx[:, None] * EXPERTS_PER_GROUP + top_idx.astype(jnp.int32)

    A = N * TOP_K
    flat_e = expert_id.reshape(A)
    flat_tok = jnp.repeat(jnp.arange(N, dtype=jnp.int32), TOP_K)
    flat_w = gates.reshape(A)
    order = jnp.argsort(flat_e)
    se, st, sw = flat_e[order], flat_tok[order], flat_w[order]
    counts = jnp.bincount(flat_e, length=N_EXPERTS)
    starts = jnp.cumsum(counts) - counts
    padded = ((counts + MOE_BLOCK - 1) // MOE_BLOCK) * MOE_BLOCK
    pends = jnp.cumsum(padded)
    pstarts = pends - padded
    dest = pstarts[se] + (jnp.arange(A) - starts[se])
    P = A + N_EXPERTS * MOE_BLOCK
    n_blk = P // MOE_BLOCK
    buf_tok = jnp.zeros((P,), jnp.int32).at[dest].set(st)
    buf_w = jnp.zeros((P,), jnp.float32).at[dest].set(sw)
    blk_e = jnp.clip(jnp.searchsorted(pends, jnp.arange(n_blk) * MOE_BLOCK, side='right'),
                     0, N_EXPERTS - 1).astype(jnp.int32)
    xb = hf[buf_tok].reshape(n_blk, MOE_BLOCK, D)

    def expert_block(args):
        xblk, e = args
        return (jax.nn.silu(xblk @ w_gate[e]) * (xblk @ w_up[e])) @ w_down[e]

    yb = lax.map(expert_block, (xb, blk_e)).reshape(P, D)
    y = jax.ops.segment_sum(yb * buf_w[:, None].astype(yb.dtype), buf_tok, num_segments=N)
    return y.reshape(B_, S_, D).astype(h.dtype)


def setup_inputs(seed: int = 0) -> dict:
    key = jax.random.key(seed)
    ks = jax.random.split(key, 24)
    f32 = jnp.float32

    def nrm(k, shape, fan_in):
        return jax.random.normal(k, shape, f32) * (fan_in ** -0.5)

    def gain(k, shape):
        return 1.0 + 0.02 * jax.random.normal(k, shape, f32)

    def small(k, shape, s=0.02):
        return s * jax.random.normal(k, shape, f32)

    return {
        "x": jax.random.normal(ks[0], (BATCH, SEQ, D_MODEL), f32),
        "mix_norm": gain(ks[1], (DEPTH, D_MODEL)),
        "ffn_norm": gain(ks[2], (DEPTH, D_MODEL)),
        "sgu_w_in": nrm(ks[3], (N_SGU_LAYERS, D_MODEL, 2 * SGU_WIDTH), D_MODEL),
        "sgu_b_in": small(ks[4], (N_SGU_LAYERS, 2 * SGU_WIDTH)),
        "sgu_v_gain": gain(ks[5], (N_SGU_LAYERS, SGU_WIDTH)),
        "sgu_v_bias": small(ks[6], (N_SGU_LAYERS, SGU_WIDTH)),
        "sgu_w_spatial": nrm(ks[7], (N_SGU_LAYERS, SGU_GROUPS, SGU_CHUNK, SGU_CHUNK), SGU_CHUNK),
        "sgu_b_spatial": gain(ks[8], (N_SGU_LAYERS, SGU_GROUPS, SGU_CHUNK)),
        "sgu_w_out": nrm(ks[9], (N_SGU_LAYERS, SGU_WIDTH, D_MODEL), SGU_WIDTH),
        "dil_w_qkv": nrm(ks[10], (N_DIL_LAYERS, D_MODEL, 3 * DIL_GROUPS * DIL_INNER), D_MODEL),
        "dil_w_out": nrm(ks[11], (N_DIL_LAYERS, DIL_INNER, D_MODEL), DIL_INNER),
        "rel_bias": small(ks[12], (NUM_BUCKETS, N_BIAS_HEADS), 0.1),
        "router_w_group": nrm(ks[13], (DEPTH, D_MODEL, N_EXPERT_GROUPS), D_MODEL),
        "router_b_group": small(ks[14], (DEPTH, N_EXPERT_GROUPS), 0.01),
        "router_w_expert": nrm(ks[15], (DEPTH, D_MODEL, N_EXPERTS), D_MODEL),
        "router_b_expert": small(ks[16], (DEPTH, N_EXPERTS), 0.01),
        "moe_w_gate": nrm(ks[17], (DEPTH, N_EXPERTS, D_MODEL, D_EXPERT), D_MODEL),
        "moe_w_up": nrm(ks[18], (DEPTH, N_EXPERTS, D_MODEL, D_EXPERT), D_MODEL),
        "moe_w_down": nrm(ks[19], (DEPTH, N_EXPERTS, D_EXPERT, D_MODEL), D_EXPERT),
        "final_norm": gain(ks[20], (D_MODEL,)),
    }


def reference(x, mix_norm, ffn_norm, sgu_w_in, sgu_b_in, sgu_v_gain, sgu_v_bias,
              sgu_w_spatial, sgu_b_spatial, sgu_w_out, dil_w_qkv, dil_w_out, rel_bias,
              router_w_group, router_b_group, router_w_expert, router_b_expert,
              moe_w_gate, moe_w_up, moe_w_down, final_norm):
    for i in range(DEPTH):
        h = rms_norm(x, mix_norm[i])
        j = i // N_MIXERS
        if i % N_MIXERS == 0:
            x = x + sgu_mixer(h, sgu_w_in[j], sgu_b_in[j], sgu_v_gain[j], sgu_v_bias[j],
                              sgu_w_spatial[j], sgu_b_spatial[j], sgu_w_out[j])
        else:
            x = x + dilated_mixer(h, dil_w_qkv[j], dil_w_out[j], rel_bias)
        h = rms_norm(x, ffn_norm[i])
        x = x + hierarchical_moe(h, router_w_group[i], router_b_group[i], router_w_expert[i],
                                 router_b_expert[i], moe_w_gate[i], moe_w_up[i], moe_w_down[i])
    return rms_norm(x, final_norm)
```

```python
import functools
import math

import jax
import jax.numpy as jnp
from jax import lax
from jax.experimental import pallas as pl
from jax.experimental.pallas import tpu as pltpu

EPS = 1e-6
SGU_CHUNK = 128
SGU_GROUP_DIM = 128
DIL_CONFIGS = ((128, 1), (512, 4), (2048, 16))
DIL_HEADS = 16
HEAD_DIM = 128
DIL_BLOCK = 128
NUM_BUCKETS = 32
MAX_DISTANCE = 2048
N_EXPERT_GROUPS = 4
EXPERTS_PER_GROUP = 8
N_EXPERTS = N_EXPERT_GROUPS * EXPERTS_PER_GROUP
TOP_K = 2
MASK_VALUE = -1e30

LANES = 128
MOE_ROWS = 256
VMEM_LIMIT = 52 * 1024 * 1024


def _params(sem, vmem=VMEM_LIMIT):
    return pltpu.CompilerParams(dimension_semantics=sem, vmem_limit_bytes=vmem)


def _rmsnorm_body(x_ref, g_ref, o_ref):
    x = x_ref[...]
    ms = jnp.mean(x * x, axis=-1, keepdims=True)
    o_ref[...] = (x * lax.rsqrt(ms + EPS) * g_ref[...]).astype(o_ref.dtype)


def _rmsnorm(x, g, out_dtype, rows=512):
    n, d = x.shape
    return pl.pallas_call(
        _rmsnorm_body,
        grid=(n // rows,),
        in_specs=[pl.BlockSpec((rows, d), lambda i: (i, 0)),
                  pl.BlockSpec((1, d), lambda i: (0, 0))],
        out_specs=pl.BlockSpec((rows, d), lambda i: (i, 0)),
        out_shape=jax.ShapeDtypeStruct((n, d), out_dtype),
        compiler_params=_params(("parallel",)),
        name="rmsnorm",
    )(x, g.reshape(1, d))


def _matmul_body(*refs, has_bias, use_gelu, has_residual):
    a_ref, w_ref = refs[0], refs[1]
    o_ref = refs[-1]
    acc = jnp.dot(a_ref[...], w_ref[...], preferred_element_type=jnp.float32)
    pos = 2
    if has_bias:
        acc = acc + refs[pos][...]
        pos += 1
    if use_gelu:
        acc = jax.nn.gelu(acc)
    if has_residual:
        acc = refs[pos][...] + acc
    o_ref[...] = acc.astype(o_ref.dtype)


def _matmul(a, w, *, bias=None, use_gelu=False, residual=None, out_dtype, tm=1024, tn=512):
    m, k = a.shape
    _, n = w.shape
    tm, tn = min(tm, m), min(tn, n)
    in_specs = [pl.BlockSpec((tm, k), lambda i, j: (i, 0)),
                pl.BlockSpec((k, tn), lambda i, j: (0, j))]
    args = [a, w]
    if bias is not None:
        in_specs.append(pl.BlockSpec((1, tn), lambda i, j: (0, j)))
        args.append(bias.reshape(1, n))
    if residual is not None:
        in_specs.append(pl.BlockSpec((tm, tn), lambda i, j: (i, j)))
        args.append(residual)
    body = functools.partial(_matmul_body, has_bias=bias is not None, use_gelu=use_gelu,
                             has_residual=residual is not None)
    return pl.pallas_call(
        body,
        grid=(m // tm, n // tn),
        in_specs=in_specs,
        out_specs=pl.BlockSpec((tm, tn), lambda i, j: (i, j)),
        out_shape=jax.ShapeDtypeStruct((m, n), out_dtype),
        compiler_params=_params(("parallel", "arbitrary")),
        name="matmul",
    )(*args)


def _sgu_gate_body(u_ref, v_ref, gain_ref, vbias_ref, ws_ref, bs_ref, o_ref, *, chunks):
    v = v_ref[...].astype(jnp.float32)
    mu = jnp.mean(v, axis=-1, keepdims=True)
    vc = v - mu
    var = jnp.mean(vc * vc, axis=-1, keepdims=True)
    vn = (vc * lax.rsqrt(var + EPS) * gain_ref[...] + vbias_ref[...]).astype(jnp.bfloat16)
    n_groups = ws_ref.shape[0]
    t_idx = lax.broadcasted_iota(jnp.int32, (SGU_CHUNK, SGU_CHUNK), 0)
    s_idx = lax.broadcasted_iota(jnp.int32, (SGU_CHUNK, SGU_CHUNK), 1)
    causal = s_idx <= t_idx
    for g in range(n_groups):
        cols = slice(g * SGU_GROUP_DIM, (g + 1) * SGU_GROUP_DIM)
        w = jnp.where(causal, ws_ref[g], jnp.zeros((), ws_ref.dtype))
        for c in range(chunks):
            rows = slice(c * SGU_CHUNK, (c + 1) * SGU_CHUNK)
            vm = jnp.dot(w, vn[rows, cols], preferred_element_type=jnp.float32) + bs_ref[:, cols]
            o_ref[rows, cols] = (u_ref[rows, cols].astype(jnp.float32) * vm).astype(o_ref.dtype)


def _sgu_gate(z, v_gain, v_bias, w_s, b_s, chunks=2):
    n, two_w = z.shape
    width = two_w // 2
    rows = chunks * SGU_CHUNK
    n_groups = w_s.shape[0]
    b_full = jnp.repeat(b_s.T, SGU_GROUP_DIM, axis=1)
    body = functools.partial(_sgu_gate_body, chunks=chunks)
    return pl.pallas_call(
        body,
        grid=(n // rows,),
        in_specs=[pl.BlockSpec((rows, width), lambda i: (i, 0)),
                  pl.BlockSpec((rows, width), lambda i: (i, 1)),
                  pl.BlockSpec((1, width), lambda i: (0, 0)),
                  pl.BlockSpec((1, width), lambda i: (0, 0)),
                  pl.BlockSpec((n_groups, SGU_CHUNK, SGU_CHUNK), lambda i: (0, 0, 0)),
                  pl.BlockSpec((SGU_CHUNK, width), lambda i: (0, 0))],
        out_specs=pl.BlockSpec((rows, width), lambda i: (i, 0)),
        out_shape=jax.ShapeDtypeStruct((n, width), jnp.bfloat16),
        compiler_params=_params(("parallel",)),
        name="sgu_gate",
    )(z, z, v_gain.reshape(1, width), v_bias.reshape(1, width), w_s.astype(jnp.bfloat16), b_full)


def _t5_causal_bucket(dist):
    max_exact = NUM_BUCKETS // 2
    d = jnp.maximum(dist, 0)
    df = jnp.maximum(d, 1).astype(jnp.float32)
    large = max_exact + (jnp.log(df / max_exact) / math.log(MAX_DISTANCE / max_exact)
                         * (NUM_BUCKETS - max_exact)).astype(jnp.int32)
    large = jnp.minimum(large, NUM_BUCKETS - 1)
    return jnp.where(d < max_exact, d, large)


def _band_bias(bias_table, window, dilation):
    span = window // dilation
    qi = jnp.arange(DIL_BLOCK)[:, None]
    kj = jnp.arange(2 * DIL_BLOCK)[None, :]
    sub_dist = qi + DIL_BLOCK - kj
    band = (sub_dist >= 0) & (sub_dist <= span)
    bias = bias_table[_t5_causal_bucket(sub_dist * dilation)]
    bias = bias.transpose(2, 0, 1).astype(jnp.float32)
    return jnp.where(band[None], bias, MASK_VALUE)


def _attn_body(q_ref, kp_ref, kc_ref, vp_ref, vc_ref, bias_ref, o_ref, lse_ref):
    first = pl.program_id(2) == 0
    key_idx = lax.broadcasted_iota(jnp.int32, (DIL_BLOCK, 2 * DIL_BLOCK), 1)
    prev_invalid = jnp.logical_and(first, key_idx < DIL_BLOCK)
    lane = lax.broadcasted_iota(jnp.int32, (DIL_BLOCK, LANES), 1)
    lse_all = jnp.zeros((DIL_BLOCK, LANES), jnp.float32)
    scale = HEAD_DIM ** -0.5
    for h in range(DIL_HEADS):
        cols = slice(h * HEAD_DIM, (h + 1) * HEAD_DIM)
        q = q_ref[0, :, cols]
        k = jnp.concatenate([kp_ref[0, :, cols], kc_ref[0, :, cols]], axis=0)
        v = jnp.concatenate([vp_ref[0, :, cols], vc_ref[0, :, cols]], axis=0)
        s = lax.dot_general(q, k, (((1,), (1,)), ((), ())), preferred_element_type=jnp.float32)
        s = s * scale + bias_ref[h]
        s = jnp.where(prev_invalid, MASK_VALUE, s)
        m = jnp.max(s, axis=-1, keepdims=True)
        p = jnp.exp(s - m)
        den = jnp.sum(p, axis=-1, keepdims=True)
        o = jnp.dot(p.astype(jnp.bfloat16), v, preferred_element_type=jnp.float32) / den
        o_ref[0, :, cols] = o.astype(o_ref.dtype)
        lse_all = jnp.where(lane == h, m + jnp.log(den), lse_all)
    lse_ref[0] = lse_all


def _dilated_group(qkv, band_bias, g, dilation, batch, seq):
    inner = DIL_HEADS * HEAD_DIM
    n_groups = len(DIL_CONFIGS)
    col_blocks = qkv.shape[1] // inner
    sub_len = seq // dilation
    nb = sub_len // DIL_BLOCK
    view = qkv.reshape(batch, sub_len, dilation * qkv.shape[1])

    def spec(part, prev):
        def index_map(b, r, n):
            blk = jnp.maximum(n - 1, 0) if prev else n
            return (b, blk, r * col_blocks + part * n_groups + g)
        return pl.BlockSpec((1, DIL_BLOCK, inner), index_map)

    o, lse = pl.pallas_call(
        _attn_body,
        grid=(batch, dilation, nb),
        in_specs=[spec(0, False), spec(1, True), spec(1, False), spec(2, True), spec(2, False),
                  pl.BlockSpec((DIL_HEADS, DIL_BLOCK, 2 * DIL_BLOCK), lambda b, r, n: (0, 0, 0))],
        out_specs=[pl.BlockSpec((1, DIL_BLOCK, inner), lambda b, r, n: (b, n, r)),
                   pl.BlockSpec((1, DIL_BLOCK, LANES), lambda b, r, n: (b, n, r))],
        out_shape=[jax.ShapeDtypeStruct((batch, sub_len, dilation * inner), jnp.bfloat16),
                   jax.ShapeDtypeStruct((batch, sub_len, dilation * LANES), jnp.float32)],
        compiler_params=_params(("parallel", "parallel", "arbitrary")),
        name="dilated_attn",
    )(view, view, view, view, view, band_bias)
    return o.reshape(batch * seq, inner), lse.reshape(batch * seq, LANES)


def _merge_body(o0_ref, o1_ref, o2_ref, l0_ref, l1_ref, l2_ref, out_ref):
    l0, l1, l2 = l0_ref[...], l1_ref[...], l2_ref[...]
    m = jnp.maximum(jnp.maximum(l0, l1), l2)
    e0, e1, e2 = jnp.exp(l0 - m), jnp.exp(l1 - m), jnp.exp(l2 - m)
    tot = e0 + e1 + e2
    w0, w1, w2 = e0 / tot, e1 / tot, e2 / tot
    for h in range(DIL_HEADS):
        cols = slice(h * HEAD_DIM, (h + 1) * HEAD_DIM)
        acc = (o0_ref[:, cols].astype(jnp.float32) * w0[:, h:h + 1]
               + o1_ref[:, cols].astype(jnp.float32) * w1[:, h:h + 1]
               + o2_ref[:, cols].astype(jnp.float32) * w2[:, h:h + 1])
        out_ref[:, cols] = acc.astype(out_ref.dtype)


def _merge_groups(outs, lses, rows=512):
    n, inner = outs[0].shape
    o_spec = pl.BlockSpec((rows, inner), lambda i: (i, 0))
    l_spec = pl.BlockSpec((rows, LANES), lambda i: (i, 0))
    return pl.pallas_call(
        _merge_body,
        grid=(n // rows,),
        in_specs=[o_spec] * 3 + [l_spec] * 3,
        out_specs=o_spec,
        out_shape=jax.ShapeDtypeStruct((n, inner), jnp.bfloat16),
        compiler_params=_params(("parallel",)),
        name="merge_groups",
    )(*outs, *lses)


def _router_body(x_ref, g_ref, w_ref, b_ref, o_ref):
    x = x_ref[...]
    ms = jnp.mean(x * x, axis=-1, keepdims=True)
    h = (x * lax.rsqrt(ms + EPS) * g_ref[...]).astype(jnp.bfloat16)
    logits = jnp.dot(h, w_ref[...], preferred_element_type=jnp.float32) + b_ref[...]
    lane = lax.broadcasted_iota(jnp.int32, logits.shape, 1)
    neg = -jnp.inf
    gl = jnp.where(lane < N_EXPERT_GROUPS, logits, neg)
    gmax = jnp.max(gl, axis=-1, keepdims=True)
    g_idx = jnp.min(jnp.where(gl == gmax, lane, LANES), axis=-1, keepdims=True)
    g_w = 1.0 / jnp.sum(jnp.exp(gl - gmax), axis=-1, keepdims=True)
    lo = N_EXPERT_GROUPS + EXPERTS_PER_GROUP * g_idx
    el = jnp.where(jnp.logical_and(lane >= lo, lane < lo + EXPERTS_PER_GROUP), logits, neg)
    v1 = jnp.max(el, axis=-1, keepdims=True)
    i1 = jnp.min(jnp.where(el == v1, lane, LANES), axis=-1, keepdims=True)
    el2 = jnp.where(lane == i1, neg, el)
    v2 = jnp.max(el2, axis=-1, keepdims=True)
    i2 = jnp.min(jnp.where(el2 == v2, lane, LANES), axis=-1, keepdims=True)
    e2 = jnp.exp(v2 - v1)
    p1 = 1.0 / (1.0 + e2)
    p2 = e2 / (1.0 + e2)
    out = jnp.where(lane == 0, (i1 - N_EXPERT_GROUPS).astype(jnp.float32),
          jnp.where(lane == 1, (i2 - N_EXPERT_GROUPS).astype(jnp.float32),
          jnp.where(lane == 2, g_w * p1,
          jnp.where(lane == 3, g_w * p2, 0.0))))
    o_ref[...] = out


def _router(x, g, w_rg, b_rg, w_re, b_re, rows=512):
    n, d = x.shape
    pad = LANES - N_EXPERT_GROUPS - N_EXPERTS
    w = jnp.concatenate([w_rg, w_re, jnp.zeros((d, pad), w_rg.dtype)], axis=1).astype(jnp.bfloat16)
    b = jnp.concatenate([b_rg, b_re, jnp.zeros((pad,), b_rg.dtype)]).reshape(1, LANES)
    return pl.pallas_call(
        _router_body,
        grid=(n // rows,),
        in_specs=[pl.BlockSpec((rows, d), lambda i: (i, 0)),
                  pl.BlockSpec((1, d), lambda i: (0, 0)),
                  pl.BlockSpec((d, LANES), lambda i: (0, 0)),
                  pl.BlockSpec((1, LANES), lambda i: (0, 0))],
        out_specs=pl.BlockSpec((rows, LANES), lambda i: (i, 0)),
        out_shape=jax.ShapeDtypeStruct((n, LANES), jnp.float32),
        compiler_params=_params(("parallel",)),
        name="router",
    )(x, g.reshape(1, d), w, b)


def _expert_body(blk_e_ref, nvalid_ref,
                 tok_ref, tok_next_ref, dst_ref,
                 x_hbm, g_ref, gate_ref, wg_ref, wu_ref, wd_ref,
                 z_hbm,
                 xbuf, obuf, gsem, ssem):
    i = pl.program_id(0)
    n_blk = pl.num_programs(0)
    slot = i % 2
    rows = xbuf.shape[1]
    active = nvalid_ref[i] > 0
    next_active = jnp.logical_and(i + 1 < n_blk, nvalid_ref[jnp.minimum(i + 1, n_blk - 1)] > 0)

    def gather(table_ref, to_slot):
        def issue(r, carry):
            pltpu.make_async_copy(x_hbm.at[pl.ds(table_ref[0, 0, r], 1)],
                                  xbuf.at[to_slot, pl.ds(r, 1)], gsem.at[to_slot]).start()
            return carry
        lax.fori_loop(0, rows, issue, 0)

    def wait_rows(buf, sem, of_slot):
        pltpu.make_async_copy(buf.at[of_slot], buf.at[of_slot], sem.at[of_slot]).wait()

    def wait_scatter(of_block, of_slot):
        nv = nvalid_ref[of_block]
        nv8 = pl.multiple_of(lax.shift_right_logical(nv, 3) * 8, 8)

        @pl.when(nv8 > 0)
        def _():
            pltpu.make_async_copy(obuf.at[of_slot, pl.ds(0, nv8)], z_hbm.at[pl.ds(0, nv8)],
                                  ssem.at[of_slot]).wait()

        def one_row(r, carry):
            pltpu.make_async_copy(obuf.at[of_slot, pl.ds(0, 1)], z_hbm.at[pl.ds(0, 1)],
                                  ssem.at[of_slot]).wait()
            return carry
        lax.fori_loop(0, nv - nv8, one_row, 0)

    @pl.when(jnp.logical_and(i == 0, active))
    def _():
        gather(tok_ref, 0)

    @pl.when(next_active)
    def _():
        gather(tok_next_ref, 1 - slot)

    @pl.when(active)
    def _():
        wait_rows(xbuf, gsem, slot)
        x = xbuf[slot]
        ms = jnp.mean(x * x, axis=-1, keepdims=True)
        h = (x * lax.rsqrt(ms + EPS) * g_ref[...]).astype(jnp.bfloat16)
        a = jax.nn.silu(jnp.dot(h, wg_ref[0], preferred_element_type=jnp.float32))
        a = a * jnp.dot(h, wu_ref[0], preferred_element_type=jnp.float32)
        y = jnp.dot(a.astype(jnp.bfloat16), wd_ref[0], preferred_element_type=jnp.float32)
        y = y * gate_ref[...]

        @pl.when(i >= 2)
        def _():
            wait_scatter(i - 2, slot)

        obuf[slot] = y

        def emit(r, carry):
            pltpu.make_async_copy(obuf.at[slot, pl.ds(r, 1)], z_hbm.at[pl.ds(dst_ref[0, 0, r], 1)],
                                  ssem.at[slot]).start()
            return carry
        lax.fori_loop(0, nvalid_ref[i], emit, 0)

        @pl.when(jnp.logical_not(next_active))
        def _():
            @pl.when(i >= 1)
            def _():
                wait_scatter(i - 1, 1 - slot)
            wait_scatter(i, slot)


def _moe_experts(x, g, tok, dst, gate, blk_e, nvalid, w_gate, w_up, w_down, z_rows):
    n, d = x.shape
    n_blk = tok.shape[0]
    d_e = w_gate.shape[-1]
    last = n_blk - 1
    grid_spec = pltpu.PrefetchScalarGridSpec(
        num_scalar_prefetch=2,
        grid=(n_blk,),
        in_specs=[
            pl.BlockSpec((1, 1, MOE_ROWS), lambda i, e, nv: (i, 0, 0), memory_space=pltpu.SMEM),
            pl.BlockSpec((1, 1, MOE_ROWS), lambda i, e, nv: (jnp.minimum(i + 1, last), 0, 0),
                         memory_space=pltpu.SMEM),
            pl.BlockSpec((1, 1, MOE_ROWS), lambda i, e, nv: (i, 0, 0), memory_space=pltpu.SMEM),
            pl.BlockSpec(memory_space=pl.ANY),
            pl.BlockSpec((1, d), lambda i, e, nv: (0, 0)),
            pl.BlockSpec((MOE_ROWS, 1), lambda i, e, nv: (i, 0)),
            pl.BlockSpec((1, d, d_e), lambda i, e, nv: (e[i], 0, 0)),
            pl.BlockSpec((1, d, d_e), lambda i, e, nv: (e[i], 0, 0)),
            pl.BlockSpec((1, d_e, d), lambda i, e, nv: (e[i], 0, 0)),
        ],
        out_specs=pl.BlockSpec(memory_space=pl.ANY),
        scratch_shapes=[pltpu.VMEM((2, MOE_ROWS, d), jnp.float32),
                        pltpu.VMEM((2, MOE_ROWS, d), jnp.float32),
                        pltpu.SemaphoreType.DMA((2,)),
                        pltpu.SemaphoreType.DMA((2,))],
    )
    return pl.pallas_call(
        _expert_body,
        grid_spec=grid_spec,
        out_shape=jax.ShapeDtypeStruct((z_rows, d), jnp.float32),
        compiler_params=_params(("arbitrary",)),
        name="moe_experts",
    )(blk_e, nvalid, tok, tok, dst, x, g.reshape(1, d), gate, w_gate, w_up, w_down)


def _combine_body(x_ref, z0_ref, z1_ref, g_ref, *out_refs, emit_sum):
    s = x_ref[...] + z0_ref[0] + z1_ref[0]
    if emit_sum:
        out_refs[0][...] = s
    ms = jnp.mean(s * s, axis=-1, keepdims=True)
    out_refs[-1][...] = (s * lax.rsqrt(ms + EPS) * g_ref[...]).astype(out_refs[-1].dtype)


def _combine_norm(x, z, g, norm_dtype, emit_sum, rows=256):
    n, d = x.shape
    row_spec = pl.BlockSpec((rows, d), lambda i: (i, 0))
    out_specs = [row_spec]
    out_shape = [jax.ShapeDtypeStruct((n, d), norm_dtype)]
    if emit_sum:
        out_specs = [row_spec] + out_specs
        out_shape = [jax.ShapeDtypeStruct((n, d), x.dtype)] + out_shape
    return pl.pallas_call(
        functools.partial(_combine_body, emit_sum=emit_sum),
        grid=(n // rows,),
        in_specs=[row_spec,
                  pl.BlockSpec((1, rows, d), lambda i: (0, i, 0)),
                  pl.BlockSpec((1, rows, d), lambda i: (1, i, 0)),
                  pl.BlockSpec((1, d), lambda i: (0, 0))],
        out_specs=out_specs,
        out_shape=out_shape,
        compiler_params=_params(("parallel",)),
        name="combine_norm",
    )(x, z, z, g.reshape(1, d))


def _moe_layer(x, ffn_g, w_rg, b_rg, w_re, b_re, w_gate, w_up, w_down):
    n, d = x.shape
    routed = _router(x, ffn_g, w_rg, b_rg, w_re, b_re)
    expert_id = routed[:, :TOP_K].astype(jnp.int32)
    gates = routed[:, TOP_K:2 * TOP_K]

    a_total = n * TOP_K
    n_blk = a_total // MOE_ROWS + N_EXPERTS
    p_total = n_blk * MOE_ROWS
    flat_e = expert_id.reshape(a_total)
    onehot = (flat_e[:, None] == jnp.arange(N_EXPERTS, dtype=jnp.int32)[None, :]).astype(jnp.int32)
    csum = jnp.cumsum(onehot, axis=0)
    rank = jnp.sum(onehot * csum, axis=1) - 1
    counts = csum[-1]
    padded = ((counts + MOE_ROWS - 1) // MOE_ROWS) * MOE_ROWS
    pends = jnp.cumsum(padded)
    pstarts = pends - padded
    slot_of = pstarts[flat_e] + rank
    a_idx = jnp.arange(a_total, dtype=jnp.int32)
    tok_of = a_idx // TOP_K
    dst_of = (a_idx % TOP_K) * n + tok_of
    buf_tok = jnp.zeros((p_total,), jnp.int32).at[slot_of].set(tok_of)
    buf_dst = jnp.zeros((p_total,), jnp.int32).at[slot_of].set(dst_of)
    buf_gate = jnp.zeros((p_total,), jnp.float32).at[slot_of].set(gates.reshape(a_total))
    blk_start = jnp.arange(n_blk, dtype=jnp.int32) * MOE_ROWS
    blk_e = jnp.clip(jnp.searchsorted(pends, blk_start, side='right'), 0, N_EXPERTS - 1).astype(jnp.int32)
    nvalid = jnp.clip(pstarts[blk_e] + counts[blk_e] - blk_start, 0, MOE_ROWS).astype(jnp.int32)
    nvalid = jnp.where(blk_start < pends[-1], nvalid, 0)

    z = _moe_experts(x, ffn_g,
                     buf_tok.reshape(n_blk, 1, MOE_ROWS), buf_dst.reshape(n_blk, 1, MOE_ROWS),
                     buf_gate.reshape(p_total, 1), blk_e, nvalid,
                     w_gate.astype(jnp.bfloat16), w_up.astype(jnp.bfloat16), w_down.astype(jnp.bfloat16),
                     TOP_K * n)
    return z.reshape(TOP_K, n, d)


def kernel(x, mix_norm, ffn_norm, sgu_w_in, sgu_b_in, sgu_v_gain, sgu_v_bias, sgu_w_spatial, sgu_b_spatial,
           sgu_w_out, dil_w_qkv, dil_w_out, rel_bias, router_w_group, router_b_group, router_w_expert,
           router_b_expert, moe_w_gate, moe_w_up, moe_w_down, final_norm):
    batch, seq, d = x.shape
    n = batch * seq
    bf16 = jnp.bfloat16
    xf = x.reshape(n, d)
    depth = mix_norm.shape[0]
    h = _rmsnorm(xf, mix_norm[0], bf16)
    out = None
    for i in range(depth):
        j = i // 2
        if i % 2 == 0:
            z = _matmul(h, sgu_w_in[j].astype(bf16), bias=sgu_b_in[j], use_gelu=True, out_dtype=bf16)
            y = _sgu_gate(z, sgu_v_gain[j], sgu_v_bias[j], sgu_w_spatial[j], sgu_b_spatial[j])
            xf = _matmul(y, sgu_w_out[j].astype(bf16), residual=xf, out_dtype=jnp.float32)
        else:
            qkv = _matmul(h, dil_w_qkv[j].astype(bf16), out_dtype=bf16)
            outs, lses = [], []
            for g, (window, dilation) in enumerate(DIL_CONFIGS):
                bias = _band_bias(rel_bias[:, g * DIL_HEADS:(g + 1) * DIL_HEADS], window, dilation)
                o, lse = _dilated_group(qkv, bias, g, dilation, batch, seq)
                outs.append(o)
                lses.append(lse)
            merged = _merge_groups(outs, lses)
            xf = _matmul(merged, dil_w_out[j].astype(bf16), residual=xf, out_dtype=jnp.float32)
        z = _moe_layer(xf, ffn_norm[i], router_w_group[i], router_b_group[i], router_w_expert[i],
                       router_b_expert[i], moe_w_gate[i], moe_w_up[i], moe_w_down[i])
        if i + 1 < depth:
            xf, h = _combine_norm(xf, z, mix_norm[i + 1], bf16, emit_sum=True)
        else:
            (out,) = _combine_norm(xf, z, final_norm, x.dtype, emit_sum=False)
    return out.reshape(batch, seq, d)
```

```python
import functools
import math

import jax
import jax.numpy as jnp
from jax import lax
from jax.experimental import pallas as pl
from jax.experimental.pallas import tpu as pltpu

EPS = 1e-6
SGU_CHUNK = 128
SGU_GROUP_DIM = 128
DIL_CONFIGS = ((128, 1), (512, 4), (2048, 16))
DIL_HEADS = 16
HEAD_DIM = 128
DIL_BLOCK = 128
DIL_INNER = DIL_HEADS * HEAD_DIM
NUM_BUCKETS = 32
MAX_DISTANCE = 2048
N_EXPERT_GROUPS = 4
EXPERTS_PER_GROUP = 8
N_EXPERTS = N_EXPERT_GROUPS * EXPERTS_PER_GROUP
TOP_K = 2
MASK_VALUE = -1e30

LANES = 128
MOE_ROWS = 256
VMEM_LIMIT = 52 * 1024 * 1024


def _params(sem, vmem=VMEM_LIMIT):
    return pltpu.CompilerParams(dimension_semantics=sem, vmem_limit_bytes=vmem)


def _rmsnorm_body(x_ref, g_ref, o_ref):
    x = x_ref[...]
    ms = jnp.mean(x * x, axis=-1, keepdims=True)
    o_ref[...] = (x * lax.rsqrt(ms + EPS) * g_ref[...]).astype(o_ref.dtype)


def _rmsnorm(x, g, out_dtype, rows=512):
    n, d = x.shape
    return pl.pallas_call(
        _rmsnorm_body,
        grid=(n // rows,),
        in_specs=[pl.BlockSpec((rows, d), lambda i: (i, 0)),
                  pl.BlockSpec((1, d), lambda i: (0, 0))],
        out_specs=pl.BlockSpec((rows, d), lambda i: (i, 0)),
        out_shape=jax.ShapeDtypeStruct((n, d), out_dtype),
        compiler_params=_params(("parallel",)),
        name="rmsnorm",
    )(x, g.reshape(1, d))


def _matmul_body(*refs, has_bias, use_gelu, has_residual, dilation):
    a_ref, w_ref = refs[0], refs[1]
    acc = jnp.dot(a_ref[...], w_ref[...].astype(a_ref.dtype), preferred_element_type=jnp.float32)
    pos = 2
    if has_bias:
        acc = acc + refs[pos][...]
        pos += 1
    if use_gelu:
        acc = jax.nn.gelu(acc)
    if has_residual:
        acc = refs[pos][...] + acc
        pos += 1
    o_ref = refs[pos]
    if dilation == 1:
        o_ref[...] = acc.astype(o_ref.dtype).reshape(o_ref.shape)
    else:
        acc_ref = refs[pos + 1]
        sub = acc.shape[0] // dilation
        for c in range(acc_ref.shape[0]):
            cols = slice(c * LANES, (c + 1) * LANES)
            acc_ref[c] = acc[:, cols]
            for r in range(dilation):
                o_ref[0, r, :, cols] = acc_ref[c, pl.ds(r, sub, stride=dilation), :].astype(o_ref.dtype)


def _matmul(a, w, *, bias=None, use_gelu=False, residual=None, out_dtype, col_block_map=None, n_out=None,
            regroup=None, tm=1024, tn=512):
    m, k = a.shape
    n = w.shape[1] if n_out is None else n_out
    tm, tn = min(tm, m), min(tn, n)
    w_map = (lambda i, j: (0, j)) if col_block_map is None else (lambda i, j: (0, col_block_map(j)))
    in_specs = [pl.BlockSpec((tm, k), lambda i, j: (i, 0)),
                pl.BlockSpec((k, tn), w_map)]
    args = [a, w]
    if bias is not None:
        in_specs.append(pl.BlockSpec((1, tn), lambda i, j: (0, j)))
        args.append(bias.reshape(1, n))
    if residual is not None:
        in_specs.append(pl.BlockSpec((tm, tn), lambda i, j: (i, j)))
        args.append(residual)
    scratch = []
    dilation = 1
    if regroup is None:
        out_spec = pl.BlockSpec((tm, tn), lambda i, j: (i, j))
        out_shape = jax.ShapeDtypeStruct((m, n), out_dtype)
    else:
        batch, seq, dilation = regroup
        tiles = seq // tm
        out_spec = pl.BlockSpec((1, dilation, tm // dilation, tn), lambda i, j: (i // tiles, 0, i % tiles, j))
        out_shape = jax.ShapeDtypeStruct((batch, dilation, seq // dilation, n), out_dtype)
        if dilation > 1:
            scratch = [pltpu.VMEM((tn // LANES, tm, LANES), jnp.float32)]
    body = functools.partial(_matmul_body, has_bias=bias is not None, use_gelu=use_gelu,
                             has_residual=residual is not None, dilation=dilation)
    return pl.pallas_call(
        body,
        grid=(m // tm, n // tn),
        in_specs=in_specs,
        out_specs=out_spec,
        out_shape=out_shape,
        scratch_shapes=scratch,
        compiler_params=_params(("parallel", "arbitrary")),
        name="matmul",
    )(*args)


def _sgu_gate_body(u_ref, v_ref, gain_ref, vbias_ref, ws_ref, bs_ref, o_ref, *, chunks):
    v = v_ref[...].astype(jnp.float32)
    mu = jnp.mean(v, axis=-1, keepdims=True)
    vc = v - mu
    var = jnp.mean(vc * vc, axis=-1, keepdims=True)
    vn = (vc * lax.rsqrt(var + EPS) * gain_ref[...] + vbias_ref[...]).astype(jnp.bfloat16)
    n_groups = ws_ref.shape[0]
    t_idx = lax.broadcasted_iota(jnp.int32, (SGU_CHUNK, SGU_CHUNK), 0)
    s_idx = lax.broadcasted_iota(jnp.int32, (SGU_CHUNK, SGU_CHUNK), 1)
    causal = s_idx <= t_idx
    for g in range(n_groups):
        cols = slice(g * SGU_GROUP_DIM, (g + 1) * SGU_GROUP_DIM)
        w = jnp.where(causal, ws_ref[g], jnp.zeros((), ws_ref.dtype))
        for c in range(chunks):
            rows = slice(c * SGU_CHUNK, (c + 1) * SGU_CHUNK)
            vm = jnp.dot(w, vn[rows, cols], preferred_element_type=jnp.float32) + bs_ref[:, cols]
            o_ref[rows, cols] = (u_ref[rows, cols].astype(jnp.float32) * vm).astype(o_ref.dtype)


def _sgu_gate(z, v_gain, v_bias, w_s, b_s, chunks=2):
    n, two_w = z.shape
    width = two_w // 2
    rows = chunks * SGU_CHUNK
    n_groups = w_s.shape[0]
    b_full = jnp.repeat(b_s.T, SGU_GROUP_DIM, axis=1)
    body = functools.partial(_sgu_gate_body, chunks=chunks)
    return pl.pallas_call(
        body,
        grid=(n // rows,),
        in_specs=[pl.BlockSpec((rows, width), lambda i: (i, 0)),
                  pl.BlockSpec((rows, width), lambda i: (i, 1)),
                  pl.BlockSpec((1, width), lambda i: (0, 0)),
                  pl.BlockSpec((1, width), lambda i: (0, 0)),
                  pl.BlockSpec((n_groups, SGU_CHUNK, SGU_CHUNK), lambda i: (0, 0, 0)),
                  pl.BlockSpec((SGU_CHUNK, width), lambda i: (0, 0))],
        out_specs=pl.BlockSpec((rows, width), lambda i: (i, 0)),
        out_shape=jax.ShapeDtypeStruct((n, width), jnp.bfloat16),
        compiler_params=_params(("parallel",)),
        name="sgu_gate",
    )(z, z, v_gain.reshape(1, width), v_bias.reshape(1, width), w_s.astype(jnp.bfloat16), b_full)


def _t5_causal_bucket(dist):
    max_exact = NUM_BUCKETS // 2
    d = jnp.maximum(dist, 0)
    df = jnp.maximum(d, 1).astype(jnp.float32)
    large = max_exact + (jnp.log(df / max_exact) / math.log(MAX_DISTANCE / max_exact)
                         * (NUM_BUCKETS - max_exact)).astype(jnp.int32)
    large = jnp.minimum(large, NUM_BUCKETS - 1)
    return jnp.where(d < max_exact, d, large)


def _band_bias(bias_table, window, dilation):
    span = window // dilation
    qi = jnp.arange(DIL_BLOCK)[:, None]
    kj = jnp.arange(2 * DIL_BLOCK)[None, :]
    sub_dist = qi + DIL_BLOCK - kj
    band = (sub_dist >= 0) & (sub_dist <= span)
    bias = bias_table[_t5_causal_bucket(sub_dist * dilation)]
    bias = bias.transpose(2, 0, 1).astype(jnp.float32)
    return jnp.where(band[None], bias, MASK_VALUE)


def _attn_body(q_ref, kp_ref, kc_ref, vp_ref, vc_ref, bias_ref, o_ref, lse_ref):
    first = pl.program_id(2) == 0
    key_idx = lax.broadcasted_iota(jnp.int32, (DIL_BLOCK, 2 * DIL_BLOCK), 1)
    prev_invalid = jnp.logical_and(first, key_idx < DIL_BLOCK)
    lane = lax.broadcasted_iota(jnp.int32, (DIL_BLOCK, LANES), 1)
    lse_all = jnp.zeros((DIL_BLOCK, LANES), jnp.float32)
    scale = HEAD_DIM ** -0.5
    for h in range(DIL_HEADS):
        cols = slice(h * HEAD_DIM, (h + 1) * HEAD_DIM)
        q = q_ref[0, 0, :, cols]
        k = jnp.concatenate([kp_ref[0, 0, :, cols], kc_ref[0, 0, :, cols]], axis=0)
        v = jnp.concatenate([vp_ref[0, 0, :, cols], vc_ref[0, 0, :, cols]], axis=0)
        s = lax.dot_general(q, k, (((1,), (1,)), ((), ())), preferred_element_type=jnp.float32)
        s = s * scale + bias_ref[h]
        s = jnp.where(prev_invalid, MASK_VALUE, s)
        m = jnp.max(s, axis=-1, keepdims=True)
        p = jnp.exp(s - m)
        den = jnp.sum(p, axis=-1, keepdims=True)
        o = jnp.dot(p.astype(jnp.bfloat16), v, preferred_element_type=jnp.float32) / den
        o_ref[0, 0, :, cols] = o.astype(o_ref.dtype)
        lse_all = jnp.where(lane == h, m + jnp.log(den), lse_all)
    lse_ref[0, 0] = lse_all


def _dilated_group(qkv, band_bias):
    batch, dilation, sub_len, _ = qkv.shape
    nb = sub_len // DIL_BLOCK

    def spec(part, prev):
        def index_map(b, r, n):
            return (b, r, jnp.maximum(n - 1, 0) if prev else n, part)
        return pl.BlockSpec((1, 1, DIL_BLOCK, DIL_INNER), index_map)

    return pl.pallas_call(
        _attn_body,
        grid=(batch, dilation, nb),
        in_specs=[spec(0, False), spec(1, True), spec(1, False), spec(2, True), spec(2, False),
                  pl.BlockSpec((DIL_HEADS, DIL_BLOCK, 2 * DIL_BLOCK), lambda b, r, n: (0, 0, 0))],
        out_specs=[pl.BlockSpec((1, 1, DIL_BLOCK, DIL_INNER), lambda b, r, n: (b, r, n, 0)),
                   pl.BlockSpec((1, 1, DIL_BLOCK, LANES), lambda b, r, n: (b, r, n, 0))],
        out_shape=[jax.ShapeDtypeStruct((batch, dilation, sub_len, DIL_INNER), jnp.bfloat16),
                   jax.ShapeDtypeStruct((batch, dilation, sub_len, LANES), jnp.float32)],
        compiler_params=_params(("parallel", "parallel", "arbitrary")),
        name="dilated_attn",
    )(qkv, qkv, qkv, qkv, qkv, band_bias)


def _merge_body(*refs, dilations):
    n_g = len(dilations)
    o_refs, l_refs = refs[:n_g], refs[n_g:2 * n_g]
    out_ref, ltok, wtok, acc = refs[2 * n_g:]
    rows = out_ref.shape[0]

    def residue_rows(r, d):
        return pl.ds(r, rows // d, stride=d) if d > 1 else slice(None)

    for g, d in enumerate(dilations):
        for r in range(d):
            ltok[g, residue_rows(r, d), :] = l_refs[g][0, r]
    lse = [ltok[g] for g in range(n_g)]
    m = functools.reduce(jnp.maximum, lse)
    e = [jnp.exp(l - m) for l in lse]
    tot = functools.reduce(jnp.add, e)
    for g in range(n_g):
        wtok[g] = e[g] / tot
    for g, d in enumerate(dilations):
        for r in range(d):
            rr = residue_rows(r, d)
            w = wtok[g, rr, :]
            for h in range(DIL_HEADS):
                cols = slice(h * HEAD_DIM, (h + 1) * HEAD_DIM)
                term = o_refs[g][0, r, :, cols].astype(jnp.float32) * w[:, h:h + 1]
                if g == 0:
                    acc[h, rr, :] = term
                else:
                    acc[h, rr, :] = acc[h, rr, :] + term
    for h in range(DIL_HEADS):
        out_ref[:, h * HEAD_DIM:(h + 1) * HEAD_DIM] = acc[h].astype(out_ref.dtype)


def _merge_groups(outs, lses, rows=512):
    batch = outs[0].shape[0]
    dilations = tuple(o.shape[1] for o in outs)
    seq = outs[0].shape[1] * outs[0].shape[2]
    tiles = seq // rows

    def spec(d, width):
        return pl.BlockSpec((1, d, rows // d, width), lambda i: (i // tiles, 0, i % tiles, 0))

    n_g = len(outs)
    return pl.pallas_call(
        functools.partial(_merge_body, dilations=dilations),
        grid=(batch * tiles,),
        in_specs=[spec(d, DIL_INNER) for d in dilations] + [spec(d, LANES) for d in dilations],
        out_specs=pl.BlockSpec((rows, DIL_INNER), lambda i: (i, 0)),
        out_shape=jax.ShapeDtypeStruct((batch * seq, DIL_INNER), jnp.bfloat16),
        scratch_shapes=[pltpu.VMEM((n_g, rows, LANES), jnp.float32),
                        pltpu.VMEM((n_g, rows, LANES), jnp.float32),
                        pltpu.VMEM((DIL_HEADS, rows, HEAD_DIM), jnp.float32)],
        compiler_params=_params(("parallel",)),
        name="merge_groups",
    )(*outs, *lses)


def _pack_halves(v):
    c = v.shape[1] // 2
    r = v.astype(jnp.bfloat16).astype(jnp.float32)
    lo = lax.bitcast_convert_type(r[:, :c], jnp.uint32)
    hi = lax.bitcast_convert_type(r[:, c:], jnp.uint32)
    return (lo >> 16) | (hi & jnp.uint32(0xFFFF0000))


def _unpack_halves(w):
    lo = lax.bitcast_convert_type(w << 16, jnp.float32)
    hi = lax.bitcast_convert_type(w & jnp.uint32(0xFFFF0000), jnp.float32)
    return jnp.concatenate([lo, hi], axis=1)


def _router_body(x_ref, g_ref, w_ref, b_ref, o_ref, hp_ref, cnt_ref, carry):
    i = pl.program_id(0)

    @pl.when(i == 0)
    def _():
        carry[...] = jnp.zeros_like(carry)

    x = x_ref[...]
    ms = jnp.mean(x * x, axis=-1, keepdims=True)
    hf = x * lax.rsqrt(ms + EPS) * g_ref[...]
    hp_ref[...] = _pack_halves(hf)
    logits = jnp.dot(hf.astype(jnp.bfloat16), w_ref[...], preferred_element_type=jnp.float32) + b_ref[...]
    rows = logits.shape[0]
    lane = lax.broadcasted_iota(jnp.int32, logits.shape, 1)
    neg = -jnp.inf
    gl = jnp.where(lane < N_EXPERT_GROUPS, logits, neg)
    gmax = jnp.max(gl, axis=-1, keepdims=True)
    g_idx = jnp.min(jnp.where(gl == gmax, lane, LANES), axis=-1, keepdims=True)
    g_w = 1.0 / jnp.sum(jnp.exp(gl - gmax), axis=-1, keepdims=True)
    lo = N_EXPERT_GROUPS + EXPERTS_PER_GROUP * g_idx
    el = jnp.where(jnp.logical_and(lane >= lo, lane < lo + EXPERTS_PER_GROUP), logits, neg)
    v1 = jnp.max(el, axis=-1, keepdims=True)
    i1 = jnp.min(jnp.where(el == v1, lane, LANES), axis=-1, keepdims=True)
    el2 = jnp.where(lane == i1, neg, el)
    v2 = jnp.max(el2, axis=-1, keepdims=True)
    i2 = jnp.min(jnp.where(el2 == v2, lane, LANES), axis=-1, keepdims=True)
    e2 = jnp.exp(v2 - v1)
    p1 = 1.0 / (1.0 + e2)
    p2 = e2 / (1.0 + e2)
    oh1 = lane == i1 - N_EXPERT_GROUPS
    oh2 = lane == i2 - N_EXPERT_GROUPS
    oh1f, oh2f = oh1.astype(jnp.float32), oh2.astype(jnp.float32)
    before = (lax.broadcasted_iota(jnp.int32, (rows, rows), 1)
              < lax.broadcasted_iota(jnp.int32, (rows, rows), 0)).astype(jnp.bfloat16)
    pre1 = jnp.dot(before, oh1f.astype(jnp.bfloat16), preferred_element_type=jnp.float32)
    pre2 = jnp.dot(before, oh2f.astype(jnp.bfloat16), preferred_element_type=jnp.float32)
    cnt1 = jnp.sum(oh1f, axis=0, keepdims=True)
    cnt2 = jnp.sum(oh2f, axis=0, keepdims=True)
    base = carry[...]
    rank1 = jnp.sum(jnp.where(oh1, pre1 + base, 0.0), axis=-1, keepdims=True)
    rank2 = jnp.sum(jnp.where(oh2, pre2 + (base + cnt1), 0.0), axis=-1, keepdims=True)
    total = base + cnt1 + cnt2
    carry[...] = total
    cnt_ref[...] = total
    out = jnp.where(lane == 0, (i1 - N_EXPERT_GROUPS).astype(jnp.float32),
          jnp.where(lane == 1, (i2 - N_EXPERT_GROUPS).astype(jnp.float32),
          jnp.where(lane == 2, g_w * p1,
          jnp.where(lane == 3, g_w * p2,
          jnp.where(lane == 4, rank1,
          jnp.where(lane == 5, rank2, 0.0))))))
    o_ref[...] = out


def _router(x, g, w_rg, b_rg, w_re, b_re, rows=512):
    n, d = x.shape
    pad = LANES - N_EXPERT_GROUPS - N_EXPERTS
    w = jnp.concatenate([w_rg, w_re, jnp.zeros((d, pad), w_rg.dtype)], axis=1).astype(jnp.bfloat16)
    b = jnp.concatenate([b_rg, b_re, jnp.zeros((pad,), b_rg.dtype)]).reshape(1, LANES)
    return pl.pallas_call(
        _router_body,
        grid=(n // rows,),
        in_specs=[pl.BlockSpec((rows, d), lambda i: (i, 0)),
                  pl.BlockSpec((1, d), lambda i: (0, 0)),
                  pl.BlockSpec((d, LANES), lambda i: (0, 0)),
                  pl.BlockSpec((1, LANES), lambda i: (0, 0))],
        out_specs=[pl.BlockSpec((rows, LANES), lambda i: (i, 0)),
                   pl.BlockSpec((rows, d // 2), lambda i: (i, 0)),
                   pl.BlockSpec((1, LANES), lambda i: (0, 0))],
        out_shape=[jax.ShapeDtypeStruct((n, LANES), jnp.float32),
                   jax.ShapeDtypeStruct((n, d // 2), jnp.uint32),
                   jax.ShapeDtypeStruct((1, LANES), jnp.float32)],
        scratch_shapes=[pltpu.VMEM((1, LANES), jnp.float32)],
        compiler_params=_params(("arbitrary",)),
        name="router",
    )(x, g.reshape(1, d), w, b)


def _dispatch_body(s1_ref, s2_ref, hp_hbm, xb_in_hbm, xb_hbm, sem):
    del xb_in_hbm
    i = pl.program_id(0)
    n_steps = pl.num_programs(0)
    rows = s1_ref.shape[2]
    base = i * rows
    par = i % 2

    def issue(r, carry):
        src = hp_hbm.at[pl.ds(base + r, 1)]
        pltpu.make_async_copy(src, xb_hbm.at[pl.ds(s1_ref[0, 0, r], 1)], sem.at[par]).start()
        pltpu.make_async_copy(src, xb_hbm.at[pl.ds(s2_ref[0, 0, r], 1)], sem.at[par]).start()
        return carry
    lax.fori_loop(0, rows, issue, 0, unroll=8)

    def wait_step(of_par):
        for _ in range(TOP_K):
            pltpu.make_async_copy(hp_hbm.at[pl.ds(0, rows)], xb_hbm.at[pl.ds(0, rows)], sem.at[of_par]).wait()

    @pl.when(i >= 1)
    def _():
        wait_step(1 - par)

    @pl.when(i == n_steps - 1)
    def _():
        wait_step(par)


def _dispatch(hp, slot1, slot2, p_total, rows=512):
    n, c = hp.shape
    steps = n // rows
    tab = pl.BlockSpec((1, 1, rows), lambda i: (i, 0, 0), memory_space=pltpu.SMEM)
    any_spec = pl.BlockSpec(memory_space=pl.ANY)
    return pl.pallas_call(
        _dispatch_body,
        grid=(steps,),
        in_specs=[tab, tab, any_spec, any_spec],
        out_specs=any_spec,
        out_shape=jax.ShapeDtypeStruct((p_total, c), hp.dtype),
        scratch_shapes=[pltpu.SemaphoreType.DMA((2,))],
        input_output_aliases={3: 0},
        compiler_params=_params(("arbitrary",)),
        name="moe_dispatch",
    )(slot1.reshape(steps, 1, rows), slot2.reshape(steps, 1, rows), hp, jnp.zeros((p_total, c), hp.dtype))


def _expert_body(blk_e_ref, nvalid_ref, xb_ref, wg_ref, wu_ref, wd_ref, yb_ref):
    del blk_e_ref
    i = pl.program_id(0)

    @pl.when(nvalid_ref[i] > 0)
    def _():
        h = _unpack_halves(xb_ref[...]).astype(jnp.bfloat16)
        a = jax.nn.silu(jnp.dot(h, wg_ref[0], preferred_element_type=jnp.float32))
        a = a * jnp.dot(h, wu_ref[0], preferred_element_type=jnp.float32)
        y = jnp.dot(a.astype(jnp.bfloat16), wd_ref[0], preferred_element_type=jnp.float32)
        yb_ref[...] = _pack_halves(y)

    @pl.when(nvalid_ref[i] == 0)
    def _():
        yb_ref[...] = jnp.zeros_like(yb_ref)


def _moe_experts(xb, blk_e, nvalid, w_gate, w_up, w_down):
    p_total, c = xb.shape
    d = 2 * c
    d_e = w_gate.shape[-1]
    grid_spec = pltpu.PrefetchScalarGridSpec(
        num_scalar_prefetch=2,
        grid=(p_total // MOE_ROWS,),
        in_specs=[pl.BlockSpec((MOE_ROWS, c), lambda i, e, nv: (i, 0)),
                  pl.BlockSpec((1, d, d_e), lambda i, e, nv: (e[i], 0, 0)),
                  pl.BlockSpec((1, d, d_e), lambda i, e, nv: (e[i], 0, 0)),
                  pl.BlockSpec((1, d_e, d), lambda i, e, nv: (e[i], 0, 0))],
        out_specs=pl.BlockSpec((MOE_ROWS, c), lambda i, e, nv: (i, 0)),
    )
    return pl.pallas_call(
        _expert_body,
        grid_spec=grid_spec,
        out_shape=jax.ShapeDtypeStruct((p_total, c), xb.dtype),
        compiler_params=_params(("arbitrary",)),
        name="moe_experts",
    )(blk_e, nvalid, xb, w_gate, w_up, w_down)


def _combine_body(s1_ref, s2_ref, s1n_ref, s2n_ref, x_ref, routed_ref, g_ref, yb_hbm, *refs, emit_sum):
    out_refs, ybuf, sem = refs[:-2], refs[-2], refs[-1]
    i = pl.program_id(0)
    n_steps = pl.num_programs(0)
    slot = i % 2
    rows = x_ref.shape[0]

    def gather(t1_ref, t2_ref, to_slot):
        def issue(r, carry):
            pltpu.make_async_copy(yb_hbm.at[pl.ds(t1_ref[0, 0, r], 1)], ybuf.at[to_slot, pl.ds(r, 1)],
                                  sem.at[to_slot]).start()
            pltpu.make_async_copy(yb_hbm.at[pl.ds(t2_ref[0, 0, r], 1)], ybuf.at[to_slot, pl.ds(rows + r, 1)],
                                  sem.at[to_slot]).start()
            return carry
        lax.fori_loop(0, rows, issue, 0, unroll=8)

    @pl.when(i == 0)
    def _():
        gather(s1_ref, s2_ref, 0)

    @pl.when(i + 1 < n_steps)
    def _():
        gather(s1n_ref, s2n_ref, 1 - slot)

    pltpu.make_async_copy(ybuf.at[slot], ybuf.at[slot], sem.at[slot]).wait()
    y1 = _unpack_halves(ybuf[slot, :rows])
    y2 = _unpack_halves(ybuf[slot, rows:])
    s = x_ref[...] + (y1 * routed_ref[:, 2:3] + y2 * routed_ref[:, 3:4])
    if emit_sum:
        out_refs[0][...] = s
    ms = jnp.mean(s * s, axis=-1, keepdims=True)
    out_refs[-1][...] = (s * lax.rsqrt(ms + EPS) * g_ref[...]).astype(out_refs[-1].dtype)


def _combine_norm(x, yb, routed, slot1, slot2, g, norm_dtype, emit_sum, rows=256):
    n, d = x.shape
    steps = n // rows
    row_spec = pl.BlockSpec((rows, d), lambda i: (i, 0))
    tab = pl.BlockSpec((1, 1, rows), lambda i: (i, 0, 0), memory_space=pltpu.SMEM)
    tab_next = pl.BlockSpec((1, 1, rows), lambda i: (jnp.minimum(i + 1, steps - 1), 0, 0),
                            memory_space=pltpu.SMEM)
    out_specs = [row_spec]
    out_shape = [jax.ShapeDtypeStruct((n, d), norm_dtype)]
    if emit_sum:
        out_specs = [row_spec] + out_specs
        out_shape = [jax.ShapeDtypeStruct((n, d), x.dtype)] + out_shape
    s1 = slot1.reshape(steps, 1, rows)
    s2 = slot2.reshape(steps, 1, rows)
    return pl.pallas_call(
        functools.partial(_combine_body, emit_sum=emit_sum),
        grid=(steps,),
        in_specs=[tab, tab, tab_next, tab_next, row_spec,
                  pl.BlockSpec((rows, LANES), lambda i: (i, 0)),
                  pl.BlockSpec((1, d), lambda i: (0, 0)),
                  pl.BlockSpec(memory_space=pl.ANY)],
        out_specs=out_specs,
        out_shape=out_shape,
        scratch_shapes=[pltpu.VMEM((2, TOP_K * rows, d // 2), yb.dtype),
                        pltpu.SemaphoreType.DMA((2,))],
        compiler_params=_params(("arbitrary",)),
        name="combine_norm",
    )(s1, s2, s1, s2, x, routed, g.reshape(1, d), yb)


def _moe_layer(x, ffn_g, next_g, norm_dtype, emit_sum, w_rg, b_rg, w_re, b_re, w_gate, w_up, w_down):
    n, d = x.shape
    routed, hp, counts = _router(x, ffn_g, w_rg, b_rg, w_re, b_re)

    n_blk = n * TOP_K // MOE_ROWS + N_EXPERTS
    experts = jnp.arange(N_EXPERTS, dtype=jnp.int32)
    counts = counts[0, :N_EXPERTS].astype(jnp.int32)
    padded = ((counts + MOE_ROWS - 1) // MOE_ROWS) * MOE_ROWS
    pends = jnp.cumsum(padded)
    pstarts = pends - padded
    expert_id = routed[:, 0:TOP_K].astype(jnp.int32)
    rank = routed[:, 2 * TOP_K:3 * TOP_K].astype(jnp.int32)
    slot = jnp.sum(jnp.where(expert_id[:, :, None] == experts, pstarts, 0), axis=-1) + rank
    blk_start = jnp.arange(n_blk, dtype=jnp.int32) * MOE_ROWS
    blk_e = jnp.minimum(jnp.sum(blk_start[:, None] >= pends[None, :], axis=1), N_EXPERTS - 1).astype(jnp.int32)
    seg_end = jnp.sum(jnp.where(blk_e[:, None] == experts, pstarts + counts, 0), axis=1)
    nvalid = jnp.clip(seg_end - blk_start, 0, MOE_ROWS).astype(jnp.int32)

    xb = _dispatch(hp, slot[:, 0], slot[:, 1], n_blk * MOE_ROWS)
    yb = _moe_experts(xb, blk_e, nvalid, w_gate.astype(jnp.bfloat16), w_up.astype(jnp.bfloat16),
                      w_down.astype(jnp.bfloat16))
    return _combine_norm(x, yb, routed, slot[:, 0], slot[:, 1], next_g, norm_dtype, emit_sum)


def kernel(x, mix_norm, ffn_norm, sgu_w_in, sgu_b_in, sgu_v_gain, sgu_v_bias, sgu_w_spatial, sgu_b_spatial,
           sgu_w_out, dil_w_qkv, dil_w_out, rel_bias, router_w_group, router_b_group, router_w_expert,
           router_b_expert, moe_w_gate, moe_w_up, moe_w_down, final_norm):
    batch, seq, d = x.shape
    n = batch * seq
    bf16 = jnp.bfloat16
    xf = x.reshape(n, d)
    depth = mix_norm.shape[0]
    n_dil = len(DIL_CONFIGS)
    h = _rmsnorm(xf, mix_norm[0], bf16)
    out = None
    for i in range(depth):
        j = i // 2
        if i % 2 == 0:
            z = _matmul(h, sgu_w_in[j], bias=sgu_b_in[j], use_gelu=True, out_dtype=bf16)
            y = _sgu_gate(z, sgu_v_gain[j], sgu_v_bias[j], sgu_w_spatial[j], sgu_b_spatial[j])
            xf = _matmul(y, sgu_w_out[j], residual=xf, out_dtype=jnp.float32)
        else:
            outs, lses = [], []
            tn = 512
            per_part = DIL_INNER // tn
            for g, (window, dilation) in enumerate(DIL_CONFIGS):
                qkv = _matmul(h, dil_w_qkv[j], out_dtype=bf16, n_out=3 * DIL_INNER, tn=tn,
                              col_block_map=lambda c, g=g: ((c // per_part) * n_dil + g) * per_part + c % per_part,
                              regroup=(batch, seq, dilation))
                bias = _band_bias(rel_bias[:, g * DIL_HEADS:(g + 1) * DIL_HEADS], window, dilation)
                o, lse = _dilated_group(qkv, bias)
                outs.append(o)
                lses.append(lse)
            merged = _merge_groups(outs, lses)
            xf = _matmul(merged, dil_w_out[j], residual=xf, out_dtype=jnp.float32)
        last = i + 1 == depth
        res = _moe_layer(xf, ffn_norm[i], final_norm if last else mix_norm[i + 1],
                         x.dtype if last else bf16, not last,
                         router_w_group[i], router_b_group[i], router_w_expert[i], router_b_expert[i],
                         moe_w_gate[i], moe_w_up[i], moe_w_down[i])
        if last:
            (out,) = res
        else:
            xf, h = res
    return out.reshape(batch, seq, d)
```

```python
import functools
import math

import jax
import jax.numpy as jnp
from jax import lax
from jax.experimental import pallas as pl
from jax.experimental.pallas import tpu as pltpu

EPS = 1e-6
SGU_CHUNK = 128
SGU_GROUP_DIM = 128
DIL_CONFIGS = ((128, 1), (512, 4), (2048, 16))
DIL_HEADS = 16
HEAD_DIM = 128
DIL_BLOCK = 128
DIL_INNER = DIL_HEADS * HEAD_DIM
NUM_BUCKETS = 32
MAX_DISTANCE = 2048
N_EXPERT_GROUPS = 4
EXPERTS_PER_GROUP = 8
N_EXPERTS = N_EXPERT_GROUPS * EXPERTS_PER_GROUP
TOP_K = 2
MASK_VALUE = -1e30

LANES = 128
MOE_ROWS = 256
VMEM_LIMIT = 52 * 1024 * 1024


def _params(sem, vmem=VMEM_LIMIT):
    return pltpu.CompilerParams(dimension_semantics=sem, vmem_limit_bytes=vmem)


def _rmsnorm_body(x_ref, g_ref, o_ref):
    x = x_ref[...]
    ms = jnp.mean(x * x, axis=-1, keepdims=True)
    o_ref[...] = (x * lax.rsqrt(ms + EPS) * g_ref[...]).astype(o_ref.dtype)


def _rmsnorm(x, g, out_dtype, rows=512):
    n, d = x.shape
    return pl.pallas_call(
        _rmsnorm_body,
        grid=(n // rows,),
        in_specs=[pl.BlockSpec((rows, d), lambda i: (i, 0)),
                  pl.BlockSpec((1, d), lambda i: (0, 0))],
        out_specs=pl.BlockSpec((rows, d), lambda i: (i, 0)),
        out_shape=jax.ShapeDtypeStruct((n, d), out_dtype),
        compiler_params=_params(("parallel",)),
        name="rmsnorm",
    )(x, g.reshape(1, d))


def _matmul_body(*refs, has_bias, use_gelu, has_residual, dilation):
    a_ref, w_ref = refs[0], refs[1]
    acc = jnp.dot(a_ref[...], w_ref[...].astype(a_ref.dtype), preferred_element_type=jnp.float32)
    pos = 2
    if has_bias:
        acc = acc + refs[pos][...]
        pos += 1
    if use_gelu:
        acc = jax.nn.gelu(acc)
    if has_residual:
        acc = refs[pos][...] + acc
        pos += 1
    o_ref = refs[pos]
    if dilation == 1:
        o_ref[...] = acc.astype(o_ref.dtype).reshape(o_ref.shape)
    else:
        acc_ref = refs[pos + 1]
        sub = acc.shape[0] // dilation
        for c in range(acc_ref.shape[0]):
            cols = slice(c * LANES, (c + 1) * LANES)
            acc_ref[c] = acc[:, cols]
            for r in range(dilation):
                o_ref[0, r, :, cols] = acc_ref[c, pl.ds(r, sub, stride=dilation), :].astype(o_ref.dtype)


def _matmul(a, w, *, bias=None, use_gelu=False, residual=None, out_dtype, col_block_map=None, n_out=None,
            regroup=None, tm=1024, tn=512):
    m, k = a.shape
    n = w.shape[1] if n_out is None else n_out
    tm, tn = min(tm, m), min(tn, n)
    w_map = (lambda i, j: (0, j)) if col_block_map is None else (lambda i, j: (0, col_block_map(j)))
    in_specs = [pl.BlockSpec((tm, k), lambda i, j: (i, 0)),
                pl.BlockSpec((k, tn), w_map)]
    args = [a, w]
    if bias is not None:
        in_specs.append(pl.BlockSpec((1, tn), lambda i, j: (0, j)))
        args.append(bias.reshape(1, n))
    if residual is not None:
        in_specs.append(pl.BlockSpec((tm, tn), lambda i, j: (i, j)))
        args.append(residual)
    scratch = []
    dilation = 1
    if regroup is None:
        out_spec = pl.BlockSpec((tm, tn), lambda i, j: (i, j))
        out_shape = jax.ShapeDtypeStruct((m, n), out_dtype)
    else:
        batch, seq, dilation = regroup
        tiles = seq // tm
        out_spec = pl.BlockSpec((1, dilation, tm // dilation, tn), lambda i, j: (i // tiles, 0, i % tiles, j))
        out_shape = jax.ShapeDtypeStruct((batch, dilation, seq // dilation, n), out_dtype)
        if dilation > 1:
            scratch = [pltpu.VMEM((tn // LANES, tm, LANES), jnp.float32)]
    body = functools.partial(_matmul_body, has_bias=bias is not None, use_gelu=use_gelu,
                             has_residual=residual is not None, dilation=dilation)
    return pl.pallas_call(
        body,
        grid=(m // tm, n // tn),
        in_specs=in_specs,
        out_specs=out_spec,
        out_shape=out_shape,
        scratch_shapes=scratch,
        compiler_params=_params(("parallel", "arbitrary")),
        name="matmul",
    )(*args)


def _sgu_gate_body(u_ref, v_ref, gain_ref, vbias_ref, ws_ref, bs_ref, o_ref, *, chunks):
    v = v_ref[...].astype(jnp.float32)
    mu = jnp.mean(v, axis=-1, keepdims=True)
    vc = v - mu
    var = jnp.mean(vc * vc, axis=-1, keepdims=True)
    vn = (vc * lax.rsqrt(var + EPS) * gain_ref[...] + vbias_ref[...]).astype(jnp.bfloat16)
    n_groups = ws_ref.shape[0]
    t_idx = lax.broadcasted_iota(jnp.int32, (SGU_CHUNK, SGU_CHUNK), 0)
    s_idx = lax.broadcasted_iota(jnp.int32, (SGU_CHUNK, SGU_CHUNK), 1)
    causal = s_idx <= t_idx
    for g in range(n_groups):
        cols = slice(g * SGU_GROUP_DIM, (g + 1) * SGU_GROUP_DIM)
        w = jnp.where(causal, ws_ref[g], jnp.zeros((), ws_ref.dtype))
        for c in range(chunks):
            rows = slice(c * SGU_CHUNK, (c + 1) * SGU_CHUNK)
            vm = jnp.dot(w, vn[rows, cols], preferred_element_type=jnp.float32) + bs_ref[:, cols]
            o_ref[rows, cols] = (u_ref[rows, cols].astype(jnp.float32) * vm).astype(o_ref.dtype)


def _sgu_gate(z, v_gain, v_bias, w_s, b_s, chunks=2):
    n, two_w = z.shape
    width = two_w // 2
    rows = chunks * SGU_CHUNK
    n_groups = w_s.shape[0]
    b_full = jnp.repeat(b_s.T, SGU_GROUP_DIM, axis=1)
    body = functools.partial(_sgu_gate_body, chunks=chunks)
    return pl.pallas_call(
        body,
        grid=(n // rows,),
        in_specs=[pl.BlockSpec((rows, width), lambda i: (i, 0)),
                  pl.BlockSpec((rows, width), lambda i: (i, 1)),
                  pl.BlockSpec((1, width), lambda i: (0, 0)),
                  pl.BlockSpec((1, width), lambda i: (0, 0)),
                  pl.BlockSpec((n_groups, SGU_CHUNK, SGU_CHUNK), lambda i: (0, 0, 0)),
                  pl.BlockSpec((SGU_CHUNK, width), lambda i: (0, 0))],
        out_specs=pl.BlockSpec((rows, width), lambda i: (i, 0)),
        out_shape=jax.ShapeDtypeStruct((n, width), jnp.bfloat16),
        compiler_params=_params(("parallel",)),
        name="sgu_gate",
    )(z, z, v_gain.reshape(1, width), v_bias.reshape(1, width), w_s.astype(jnp.bfloat16), b_full)


def _t5_causal_bucket(dist):
    max_exact = NUM_BUCKETS // 2
    d = jnp.maximum(dist, 0)
    df = jnp.maximum(d, 1).astype(jnp.float32)
    large = max_exact + (jnp.log(df / max_exact) / math.log(MAX_DISTANCE / max_exact)
                         * (NUM_BUCKETS - max_exact)).astype(jnp.int32)
    large = jnp.minimum(large, NUM_BUCKETS - 1)
    return jnp.where(d < max_exact, d, large)


def _band_bias(bias_table, window, dilation):
    span = window // dilation
    qi = jnp.arange(DIL_BLOCK)[:, None]
    kj = jnp.arange(2 * DIL_BLOCK)[None, :]
    sub_dist = qi + DIL_BLOCK - kj
    band = (sub_dist >= 0) & (sub_dist <= span)
    bucket = _t5_causal_bucket(sub_dist * dilation)
    onehot = (bucket[:, :, None] == jnp.arange(NUM_BUCKETS)).astype(jnp.float32)
    bias = jnp.einsum('qkb,bh->hqk', onehot, bias_table.astype(jnp.float32), precision=lax.Precision.HIGHEST)
    return jnp.where(band[None], bias, MASK_VALUE)


def _attn_body(q_ref, kp_ref, kc_ref, vp_ref, vc_ref, bias_ref, o_ref, lse_ref):
    first = pl.program_id(2) == 0
    key_idx = lax.broadcasted_iota(jnp.int32, (DIL_BLOCK, 2 * DIL_BLOCK), 1)
    prev_invalid = jnp.logical_and(first, key_idx < DIL_BLOCK)
    lane = lax.broadcasted_iota(jnp.int32, (DIL_BLOCK, LANES), 1)
    lse_all = jnp.zeros((DIL_BLOCK, LANES), jnp.float32)
    scale = HEAD_DIM ** -0.5
    for h in range(DIL_HEADS):
        cols = slice(h * HEAD_DIM, (h + 1) * HEAD_DIM)
        q = q_ref[0, 0, :, cols]
        k = jnp.concatenate([kp_ref[0, 0, :, cols], kc_ref[0, 0, :, cols]], axis=0)
        v = jnp.concatenate([vp_ref[0, 0, :, cols], vc_ref[0, 0, :, cols]], axis=0)
        s = lax.dot_general(q, k, (((1,), (1,)), ((), ())), preferred_element_type=jnp.float32)
        s = s * scale + bias_ref[h]
        s = jnp.where(prev_invalid, MASK_VALUE, s)
        m = jnp.max(s, axis=-1, keepdims=True)
        p = jnp.exp(s - m)
        den = jnp.sum(p, axis=-1, keepdims=True)
        o = jnp.dot(p.astype(jnp.bfloat16), v, preferred_element_type=jnp.float32) / den
        o_ref[0, 0, :, cols] = o.astype(o_ref.dtype)
        lse_all = jnp.where(lane == h, m + jnp.log(den), lse_all)
    lse_ref[0, 0] = lse_all


def _dilated_group(qkv, band_bias):
    batch, dilation, sub_len, _ = qkv.shape
    nb = sub_len // DIL_BLOCK

    def spec(part, prev):
        def index_map(b, r, n):
            return (b, r, jnp.maximum(n - 1, 0) if prev else n, part)
        return pl.BlockSpec((1, 1, DIL_BLOCK, DIL_INNER), index_map)

    return pl.pallas_call(
        _attn_body,
        grid=(batch, dilation, nb),
        in_specs=[spec(0, False), spec(1, True), spec(1, False), spec(2, True), spec(2, False),
                  pl.BlockSpec((DIL_HEADS, DIL_BLOCK, 2 * DIL_BLOCK), lambda b, r, n: (0, 0, 0))],
        out_specs=[pl.BlockSpec((1, 1, DIL_BLOCK, DIL_INNER), lambda b, r, n: (b, r, n, 0)),
                   pl.BlockSpec((1, 1, DIL_BLOCK, LANES), lambda b, r, n: (b, r, n, 0))],
        out_shape=[jax.ShapeDtypeStruct((batch, dilation, sub_len, DIL_INNER), jnp.bfloat16),
                   jax.ShapeDtypeStruct((batch, dilation, sub_len, LANES), jnp.float32)],
        compiler_params=_params(("parallel", "parallel", "arbitrary")),
        name="dilated_attn",
    )(qkv, qkv, qkv, qkv, qkv, band_bias)


def _merge_body(*refs, dilations):
    n_g = len(dilations)
    o_refs, l_refs = refs[:n_g], refs[n_g:2 * n_g]
    out_ref, ltok, wtok, acc = refs[2 * n_g:]
    rows = out_ref.shape[0]

    def residue_rows(r, d):
        return pl.ds(r, rows // d, stride=d) if d > 1 else slice(None)

    for g, d in enumerate(dilations):
        for r in range(d):
            ltok[g, residue_rows(r, d), :] = l_refs[g][0, r]
    lse = [ltok[g] for g in range(n_g)]
    m = functools.reduce(jnp.maximum, lse)
    e = [jnp.exp(l - m) for l in lse]
    tot = functools.reduce(jnp.add, e)
    for g in range(n_g):
        wtok[g] = e[g] / tot
    for g, d in enumerate(dilations):
        for r in range(d):
            rr = residue_rows(r, d)
            w = wtok[g, rr, :]
            for h in range(DIL_HEADS):
                cols = slice(h * HEAD_DIM, (h + 1) * HEAD_DIM)
                term = o_refs[g][0, r, :, cols].astype(jnp.float32) * w[:, h:h + 1]
                if g == 0:
                    acc[h, rr, :] = term
                else:
                    acc[h, rr, :] = acc[h, rr, :] + term
    for h in range(DIL_HEADS):
        out_ref[:, h * HEAD_DIM:(h + 1) * HEAD_DIM] = acc[h].astype(out_ref.dtype)


def _merge_groups(outs, lses, rows=512):
    batch = outs[0].shape[0]
    dilations = tuple(o.shape[1] for o in outs)
    seq = outs[0].shape[1] * outs[0].shape[2]
    tiles = seq // rows

    def spec(d, width):
        return pl.BlockSpec((1, d, rows // d, width), lambda i: (i // tiles, 0, i % tiles, 0))

    n_g = len(outs)
    return pl.pallas_call(
        functools.partial(_merge_body, dilations=dilations),
        grid=(batch * tiles,),
        in_specs=[spec(d, DIL_INNER) for d in dilations] + [spec(d, LANES) for d in dilations],
        out_specs=pl.BlockSpec((rows, DIL_INNER), lambda i: (i, 0)),
        out_shape=jax.ShapeDtypeStruct((batch * seq, DIL_INNER), jnp.bfloat16),
        scratch_shapes=[pltpu.VMEM((n_g, rows, LANES), jnp.float32),
                        pltpu.VMEM((n_g, rows, LANES), jnp.float32),
                        pltpu.VMEM((DIL_HEADS, rows, HEAD_DIM), jnp.float32)],
        compiler_params=_params(("parallel",)),
        name="merge_groups",
    )(*outs, *lses)


def _pack_halves(v):
    c = v.shape[1] // 2
    r = v.astype(jnp.bfloat16).astype(jnp.float32)
    lo = lax.bitcast_convert_type(r[:, :c], jnp.uint32)
    hi = lax.bitcast_convert_type(r[:, c:], jnp.uint32)
    return (lo >> 16) | (hi & jnp.uint32(0xFFFF0000))


def _unpack_halves(w):
    lo = lax.bitcast_convert_type(w << 16, jnp.float32)
    hi = lax.bitcast_convert_type(w & jnp.uint32(0xFFFF0000), jnp.float32)
    return jnp.concatenate([lo, hi], axis=1)


def _router_body(x_ref, g_ref, w_ref, b_ref, o_ref, hp_ref, cnt_ref, carry):
    i = pl.program_id(0)

    @pl.when(i == 0)
    def _():
        carry[...] = jnp.zeros_like(carry)

    x = x_ref[...]
    ms = jnp.mean(x * x, axis=-1, keepdims=True)
    hf = x * lax.rsqrt(ms + EPS) * g_ref[...]
    hp_ref[...] = _pack_halves(hf)
    logits = jnp.dot(hf.astype(jnp.bfloat16), w_ref[...], preferred_element_type=jnp.float32) + b_ref[...]
    rows = logits.shape[0]
    lane = lax.broadcasted_iota(jnp.int32, logits.shape, 1)
    neg = -jnp.inf
    gl = jnp.where(lane < N_EXPERT_GROUPS, logits, neg)
    gmax = jnp.max(gl, axis=-1, keepdims=True)
    g_idx = jnp.min(jnp.where(gl == gmax, lane, LANES), axis=-1, keepdims=True)
    g_w = 1.0 / jnp.sum(jnp.exp(gl - gmax), axis=-1, keepdims=True)
    lo = N_EXPERT_GROUPS + EXPERTS_PER_GROUP * g_idx
    el = jnp.where(jnp.logical_and(lane >= lo, lane < lo + EXPERTS_PER_GROUP), logits, neg)
    v1 = jnp.max(el, axis=-1, keepdims=True)
    i1 = jnp.min(jnp.where(el == v1, lane, LANES), axis=-1, keepdims=True)
    el2 = jnp.where(lane == i1, neg, el)
    v2 = jnp.max(el2, axis=-1, keepdims=True)
    i2 = jnp.min(jnp.where(el2 == v2, lane, LANES), axis=-1, keepdims=True)
    e2 = jnp.exp(v2 - v1)
    p1 = 1.0 / (1.0 + e2)
    p2 = e2 / (1.0 + e2)
    oh1 = lane == i1 - N_EXPERT_GROUPS
    oh2 = lane == i2 - N_EXPERT_GROUPS
    oh1f, oh2f = oh1.astype(jnp.float32), oh2.astype(jnp.float32)
    before = (lax.broadcasted_iota(jnp.int32, (rows, rows), 1)
              < lax.broadcasted_iota(jnp.int32, (rows, rows), 0)).astype(jnp.bfloat16)
    pre1 = jnp.dot(before, oh1f.astype(jnp.bfloat16), preferred_element_type=jnp.float32)
    pre2 = jnp.dot(before, oh2f.astype(jnp.bfloat16), preferred_element_type=jnp.float32)
    cnt1 = jnp.sum(oh1f, axis=0, keepdims=True)
    cnt2 = jnp.sum(oh2f, axis=0, keepdims=True)
    base = carry[...]
    rank1 = jnp.sum(jnp.where(oh1, pre1 + base, 0.0), axis=-1, keepdims=True)
    rank2 = jnp.sum(jnp.where(oh2, pre2 + (base + cnt1), 0.0), axis=-1, keepdims=True)
    total = base + cnt1 + cnt2
    carry[...] = total
    cnt_ref[...] = total
    out = jnp.where(lane == 0, (i1 - N_EXPERT_GROUPS).astype(jnp.float32),
          jnp.where(lane == 1, (i2 - N_EXPERT_GROUPS).astype(jnp.float32),
          jnp.where(lane == 2, g_w * p1,
          jnp.where(lane == 3, g_w * p2,
          jnp.where(lane == 4, rank1,
          jnp.where(lane == 5, rank2, 0.0))))))
    o_ref[...] = out


def _router(x, g, w_rg, b_rg, w_re, b_re, rows=512):
    n, d = x.shape
    pad = LANES - N_EXPERT_GROUPS - N_EXPERTS
    w = jnp.concatenate([w_rg, w_re, jnp.zeros((d, pad), w_rg.dtype)], axis=1).astype(jnp.bfloat16)
    b = jnp.concatenate([b_rg, b_re, jnp.zeros((pad,), b_rg.dtype)]).reshape(1, LANES)
    return pl.pallas_call(
        _router_body,
        grid=(n // rows,),
        in_specs=[pl.BlockSpec((rows, d), lambda i: (i, 0)),
                  pl.BlockSpec((1, d), lambda i: (0, 0)),
                  pl.BlockSpec((d, LANES), lambda i: (0, 0)),
                  pl.BlockSpec((1, LANES), lambda i: (0, 0))],
        out_specs=[pl.BlockSpec((rows, LANES), lambda i: (i, 0)),
                   pl.BlockSpec((rows, d // 2), lambda i: (i, 0)),
                   pl.BlockSpec((1, LANES), lambda i: (0, 0))],
        out_shape=[jax.ShapeDtypeStruct((n, LANES), jnp.float32),
                   jax.ShapeDtypeStruct((n, d // 2), jnp.uint32),
                   jax.ShapeDtypeStruct((1, LANES), jnp.float32)],
        scratch_shapes=[pltpu.VMEM((1, LANES), jnp.float32)],
        compiler_params=_params(("arbitrary",)),
        name="router",
    )(x, g.reshape(1, d), w, b)


def _dispatch_body(s1_ref, s2_ref, hp_ref, xb_in_hbm, xb_hbm, stage, sem):
    del xb_in_hbm
    i = pl.program_id(0)
    n_steps = pl.num_programs(0)
    rows = s1_ref.shape[2]
    slot = i % 2

    def wait_slot(of_slot):
        for _ in range(TOP_K):
            pltpu.make_async_copy(stage.at[of_slot], stage.at[of_slot], sem.at[of_slot]).wait()

    @pl.when(i >= 2)
    def _():
        wait_slot(slot)

    stage[slot] = hp_ref[...]

    def issue(r, carry):
        src = stage.at[slot, pl.ds(r, 1)]
        pltpu.make_async_copy(src, xb_hbm.at[pl.ds(s1_ref[0, 0, r], 1)], sem.at[slot]).start()
        pltpu.make_async_copy(src, xb_hbm.at[pl.ds(s2_ref[0, 0, r], 1)], sem.at[slot]).start()
        return carry
    lax.fori_loop(0, rows, issue, 0, unroll=8)

    @pl.when(i == n_steps - 1)
    def _():
        @pl.when(i >= 1)
        def _():
            wait_slot(1 - slot)
        wait_slot(slot)


def _dispatch(hp, slot1, slot2, p_total, rows=512):
    n, c = hp.shape
    steps = n // rows
    tab = pl.BlockSpec((1, 1, rows), lambda i: (i, 0, 0), memory_space=pltpu.SMEM)
    any_spec = pl.BlockSpec(memory_space=pl.ANY)
    return pl.pallas_call(
        _dispatch_body,
        grid=(steps,),
        in_specs=[tab, tab, pl.BlockSpec((rows, c), lambda i: (i, 0)), any_spec],
        out_specs=any_spec,
        out_shape=jax.ShapeDtypeStruct((p_total, c), hp.dtype),
        scratch_shapes=[pltpu.VMEM((2, rows, c), hp.dtype),
                        pltpu.SemaphoreType.DMA((2,))],
        input_output_aliases={3: 0},
        compiler_params=_params(("arbitrary",)),
        name="moe_dispatch",
    )(slot1.reshape(steps, 1, rows), slot2.reshape(steps, 1, rows), hp, jnp.zeros((p_total, c), hp.dtype))


def _expert_body(blk_e_ref, nvalid_ref, xb_ref, wg_ref, wu_ref, wd_ref, yb_ref):
    del blk_e_ref
    i = pl.program_id(0)

    @pl.when(nvalid_ref[i] > 0)
    def _():
        h = _unpack_halves(xb_ref[...]).astype(jnp.bfloat16)
        a = jax.nn.silu(jnp.dot(h, wg_ref[0], preferred_element_type=jnp.float32))
        a = a * jnp.dot(h, wu_ref[0], preferred_element_type=jnp.float32)
        y = jnp.dot(a.astype(jnp.bfloat16), wd_ref[0], preferred_element_type=jnp.float32)
        yb_ref[...] = _pack_halves(y)

    @pl.when(nvalid_ref[i] == 0)
    def _():
        yb_ref[...] = jnp.zeros_like(yb_ref)


def _moe_experts(xb, blk_e, nvalid, w_gate, w_up, w_down):
    p_total, c = xb.shape
    d = 2 * c
    d_e = w_gate.shape[-1]
    grid_spec = pltpu.PrefetchScalarGridSpec(
        num_scalar_prefetch=2,
        grid=(p_total // MOE_ROWS,),
        in_specs=[pl.BlockSpec((MOE_ROWS, c), lambda i, e, nv: (i, 0)),
                  pl.BlockSpec((1, d, d_e), lambda i, e, nv: (e[i], 0, 0)),
                  pl.BlockSpec((1, d, d_e), lambda i, e, nv: (e[i], 0, 0)),
                  pl.BlockSpec((1, d_e, d), lambda i, e, nv: (e[i], 0, 0))],
        out_specs=pl.BlockSpec((MOE_ROWS, c), lambda i, e, nv: (i, 0)),
    )
    return pl.pallas_call(
        _expert_body,
        grid_spec=grid_spec,
        out_shape=jax.ShapeDtypeStruct((p_total, c), xb.dtype),
        compiler_params=_params(("arbitrary",)),
        name="moe_experts",
    )(blk_e, nvalid, xb, w_gate, w_up, w_down)


def _combine_body(s1_ref, s2_ref, s1n_ref, s2n_ref, x_ref, routed_ref, g_ref, yb_hbm, *refs, emit_sum):
    out_refs, ybuf, sem = refs[:-2], refs[-2], refs[-1]
    i = pl.program_id(0)
    n_steps = pl.num_programs(0)
    slot = i % 2
    rows = x_ref.shape[0]

    def gather(t1_ref, t2_ref, to_slot):
        def issue(r, carry):
            pltpu.make_async_copy(yb_hbm.at[pl.ds(t1_ref[0, 0, r], 1)], ybuf.at[to_slot, pl.ds(r, 1)],
                                  sem.at[to_slot]).start()
            pltpu.make_async_copy(yb_hbm.at[pl.ds(t2_ref[0, 0, r], 1)], ybuf.at[to_slot, pl.ds(rows + r, 1)],
                                  sem.at[to_slot]).start()
            return carry
        lax.fori_loop(0, rows, issue, 0, unroll=8)

    @pl.when(i == 0)
    def _():
        gather(s1_ref, s2_ref, 0)

    @pl.when(i + 1 < n_steps)
    def _():
        gather(s1n_ref, s2n_ref, 1 - slot)

    pltpu.make_async_copy(ybuf.at[slot], ybuf.at[slot], sem.at[slot]).wait()
    y1 = _unpack_halves(ybuf[slot, :rows])
    y2 = _unpack_halves(ybuf[slot, rows:])
    s = x_ref[...] + (y1 * routed_ref[:, 2:3] + y2 * routed_ref[:, 3:4])
    if emit_sum:
        out_refs[0][...] = s
    ms = jnp.mean(s * s, axis=-1, keepdims=True)
    out_refs[-1][...] = (s * lax.rsqrt(ms + EPS) * g_ref[...]).astype(out_refs[-1].dtype)


def _combine_norm(x, yb, routed, slot1, slot2, g, norm_dtype, emit_sum, rows=256):
    n, d = x.shape
    steps = n // rows
    row_spec = pl.BlockSpec((rows, d), lambda i: (i, 0))
    tab = pl.BlockSpec((1, 1, rows), lambda i: (i, 0, 0), memory_space=pltpu.SMEM)
    tab_next = pl.BlockSpec((1, 1, rows), lambda i: (jnp.minimum(i + 1, steps - 1), 0, 0),
                            memory_space=pltpu.SMEM)
    out_specs = [row_spec]
    out_shape = [jax.ShapeDtypeStruct((n, d), norm_dtype)]
    if emit_sum:
        out_specs = [row_spec] + out_specs
        out_shape = [jax.ShapeDtypeStruct((n, d), x.dtype)] + out_shape
    s1 = slot1.reshape(steps, 1, rows)
    s2 = slot2.reshape(steps, 1, rows)
    return pl.pallas_call(
        functools.partial(_combine_body, emit_sum=emit_sum),
        grid=(steps,),
        in_specs=[tab, tab, tab_next, tab_next, row_spec,
                  pl.BlockSpec((rows, LANES), lambda i: (i, 0)),
                  pl.BlockSpec((1, d), lambda i: (0, 0)),
                  pl.BlockSpec(memory_space=pl.ANY)],
        out_specs=out_specs,
        out_shape=out_shape,
        scratch_shapes=[pltpu.VMEM((2, TOP_K * rows, d // 2), yb.dtype),
                        pltpu.SemaphoreType.DMA((2,))],
        compiler_params=_params(("arbitrary",)),
        name="combine_norm",
    )(s1, s2, s1, s2, x, routed, g.reshape(1, d), yb)


def _moe_layer(x, ffn_g, next_g, norm_dtype, emit_sum, w_rg, b_rg, w_re, b_re, w_gate, w_up, w_down):
    n, d = x.shape
    routed, hp, counts = _router(x, ffn_g, w_rg, b_rg, w_re, b_re)

    n_blk = n * TOP_K // MOE_ROWS + N_EXPERTS
    experts = jnp.arange(N_EXPERTS, dtype=jnp.int32)
    counts = counts[0, :N_EXPERTS].astype(jnp.int32)
    padded = ((counts + MOE_ROWS - 1) // MOE_ROWS) * MOE_ROWS
    pends = jnp.cumsum(padded)
    pstarts = pends - padded
    expert_id = routed[:, 0:TOP_K].astype(jnp.int32)
    rank = routed[:, 2 * TOP_K:3 * TOP_K].astype(jnp.int32)
    slot = jnp.sum(jnp.where(expert_id[:, :, None] == experts, pstarts, 0), axis=-1) + rank
    blk_start = jnp.arange(n_blk, dtype=jnp.int32) * MOE_ROWS
    blk_e = jnp.minimum(jnp.sum(blk_start[:, None] >= pends[None, :], axis=1), N_EXPERTS - 1).astype(jnp.int32)
    seg_end = jnp.sum(jnp.where(blk_e[:, None] == experts, pstarts + counts, 0), axis=1)
    nvalid = jnp.clip(seg_end - blk_start, 0, MOE_ROWS).astype(jnp.int32)

    xb = _dispatch(hp, slot[:, 0], slot[:, 1], n_blk * MOE_ROWS)
    yb = _moe_experts(xb, blk_e, nvalid, w_gate.astype(jnp.bfloat16), w_up.astype(jnp.bfloat16),
                      w_down.astype(jnp.bfloat16))
    return _combine_norm(x, yb, routed, slot[:, 0], slot[:, 1], next_g, norm_dtype, emit_sum)


def kernel(x, mix_norm, ffn_norm, sgu_w_in, sgu_b_in, sgu_v_gain, sgu_v_bias, sgu_w_spatial, sgu_b_spatial,
           sgu_w_out, dil_w_qkv, dil_w_out, rel_bias, router_w_group, router_b_group, router_w_expert,
           router_b_expert, moe_w_gate, moe_w_up, moe_w_down, final_norm):
    batch, seq, d = x.shape
    n = batch * seq
    bf16 = jnp.bfloat16
    xf = x.reshape(n, d)
    depth = mix_norm.shape[0]
    n_dil = len(DIL_CONFIGS)
    h = _rmsnorm(xf, mix_norm[0], bf16)
    out = None
    for i in range(depth):
        j = i // 2
        if i % 2 == 0:
            z = _matmul(h, sgu_w_in[j], bias=sgu_b_in[j], use_gelu=True, out_dtype=bf16)
            y = _sgu_gate(z, sgu_v_gain[j], sgu_v_bias[j], sgu_w_spatial[j], sgu_b_spatial[j])
            xf = _matmul(y, sgu_w_out[j], residual=xf, out_dtype=jnp.float32)
        else:
            outs, lses = [], []
            tn = 512
            per_part = DIL_INNER // tn
            for g, (window, dilation) in enumerate(DIL_CONFIGS):
                qkv = _matmul(h, dil_w_qkv[j], out_dtype=bf16, n_out=3 * DIL_INNER, tn=tn,
                              col_block_map=lambda c, g=g: ((c // per_part) * n_dil + g) * per_part + c % per_part,
                              regroup=(batch, seq, dilation))
                bias = _band_bias(rel_bias[:, g * DIL_HEADS:(g + 1) * DIL_HEADS], window, dilation)
                o, lse = _dilated_group(qkv, bias)
                outs.append(o)
                lses.append(lse)
            merged = _merge_groups(outs, lses)
            xf = _matmul(merged, dil_w_out[j], residual=xf, out_dtype=jnp.float32)
        last = i + 1 == depth
        res = _moe_layer(xf, ffn_norm[i], final_norm if last else mix_norm[i + 1],
                         x.dtype if last else bf16, not last,
                         router_w_group[i], router_b_group[i], router_w_expert[i], router_b_expert[i],
                         moe_w_gate[i], moe_w_up[i], moe_w_down[i])
        if last:
            (out,) = res
        else:
            xf, h = res
    return out.reshape(batch, seq, d)
```

```python
import functools
import math

import jax
import jax.numpy as jnp
from jax import lax
from jax.experimental import pallas as pl
from jax.experimental.pallas import tpu as pltpu

EPS = 1e-6
SGU_CHUNK = 128
SGU_GROUP_DIM = 128
DIL_CONFIGS = ((128, 1), (512, 4), (2048, 16))
DIL_HEADS = 16
HEAD_DIM = 128
DIL_BLOCK = 128
DIL_INNER = DIL_HEADS * HEAD_DIM
NUM_BUCKETS = 32
MAX_DISTANCE = 2048
N_EXPERT_GROUPS = 4
EXPERTS_PER_GROUP = 8
N_EXPERTS = N_EXPERT_GROUPS * EXPERTS_PER_GROUP
TOP_K = 2
MASK_VALUE = -1e30

LANES = 128
MXU_WIDTH = 256
MOE_ROWS = 256
VMEM_LIMIT = 52 * 1024 * 1024


def _params(sem, vmem=VMEM_LIMIT):
    return pltpu.CompilerParams(dimension_semantics=sem, vmem_limit_bytes=vmem)


def _rmsnorm_body(x_ref, g_ref, o_ref):
    x = x_ref[...]
    ms = jnp.mean(x * x, axis=-1, keepdims=True)
    o_ref[...] = (x * lax.rsqrt(ms + EPS) * g_ref[...]).astype(o_ref.dtype)


def _rmsnorm(x, g, out_dtype, rows=512):
    n, d = x.shape
    return pl.pallas_call(
        _rmsnorm_body,
        grid=(n // rows,),
        in_specs=[pl.BlockSpec((rows, d), lambda i: (i, 0)),
                  pl.BlockSpec((1, d), lambda i: (0, 0))],
        out_specs=pl.BlockSpec((rows, d), lambda i: (i, 0)),
        out_shape=jax.ShapeDtypeStruct((n, d), out_dtype),
        compiler_params=_params(("parallel",)),
        name="rmsnorm",
    )(x, g.reshape(1, d))


def _cast_body(x_ref, o_ref):
    o_ref[...] = x_ref[...].astype(o_ref.dtype)


def _cast_layer(w, layer, dtype, block_bytes=4 * 1024 * 1024):
    shape = w.shape[1:]
    cols = shape[-1]
    flat = w.reshape(-1, cols)
    total = flat.shape[0] // w.shape[0]
    rows = min(total, max(8, block_bytes // (cols * w.dtype.itemsize)))
    steps = total // rows
    out = pl.pallas_call(
        _cast_body,
        grid=(steps,),
        in_specs=[pl.BlockSpec((rows, cols), lambda i: (layer * steps + i, 0))],
        out_specs=pl.BlockSpec((rows, cols), lambda i: (i, 0)),
        out_shape=jax.ShapeDtypeStruct((total, cols), dtype),
        compiler_params=_params(("parallel",)),
        name="cast",
    )(flat)
    return out.reshape(shape)


def _matmul_body(*refs, has_bias, use_gelu, has_residual, dilation):
    a_ref, w_ref = refs[0], refs[1]
    pos = 2
    bias_ref = res_ref = None
    if has_bias:
        bias_ref = refs[pos]
        pos += 1
    if has_residual:
        res_ref = refs[pos]
        pos += 1
    o_ref = refs[pos]
    a = a_ref[...]
    tn = w_ref.shape[1]
    for c0 in range(0, tn, MXU_WIDTH):
        cols = slice(c0, min(c0 + MXU_WIDTH, tn))
        acc = jnp.dot(a, w_ref[:, cols].astype(a.dtype), preferred_element_type=jnp.float32)
        if has_bias:
            acc = acc + bias_ref[:, cols]
        if use_gelu:
            acc = jax.nn.gelu(acc)
        if has_residual:
            acc = res_ref[:, cols] + acc
        if len(o_ref.shape) == 2:
            o_ref[:, cols] = acc.astype(o_ref.dtype)
        elif dilation == 1:
            o_ref[0, 0, :, cols] = acc.astype(o_ref.dtype)
        else:
            acc_ref = refs[pos + 1]
            sub = acc.shape[0] // dilation
            for p0 in range(cols.start, cols.stop, LANES):
                panel = p0 // LANES
                acc_ref[panel] = acc[:, p0 - c0:p0 - c0 + LANES]
                for r in range(dilation):
                    o_ref[0, r, :, p0:p0 + LANES] = (
                        acc_ref[panel, pl.ds(r, sub, stride=dilation), :].astype(o_ref.dtype))


def _matmul(a, w, *, bias=None, use_gelu=False, residual=None, out_dtype, col_block_map=None, n_out=None,
            regroup=None, tm=1024, tn=512):
    m, k = a.shape
    n = w.shape[1] if n_out is None else n_out
    tm, tn = min(tm, m), min(tn, n)
    w_map = (lambda i, j: (0, j)) if col_block_map is None else (lambda i, j: (0, col_block_map(j)))
    in_specs = [pl.BlockSpec((tm, k), lambda i, j: (i, 0)),
                pl.BlockSpec((k, tn), w_map)]
    args = [a, w]
    if bias is not None:
        in_specs.append(pl.BlockSpec((1, tn), lambda i, j: (0, j)))
        args.append(bias.reshape(1, n))
    if residual is not None:
        in_specs.append(pl.BlockSpec((tm, tn), lambda i, j: (i, j)))
        args.append(residual)
    scratch = []
    dilation = 1
    if regroup is None:
        out_spec = pl.BlockSpec((tm, tn), lambda i, j: (i, j))
        out_shape = jax.ShapeDtypeStruct((m, n), out_dtype)
    else:
        batch, seq, dilation = regroup
        tiles = seq // tm
        out_spec = pl.BlockSpec((1, dilation, tm // dilation, tn), lambda i, j: (i // tiles, 0, i % tiles, j))
        out_shape = jax.ShapeDtypeStruct((batch, dilation, seq // dilation, n), out_dtype)
        if dilation > 1:
            scratch = [pltpu.VMEM((tn // LANES, tm, LANES), jnp.float32)]
    body = functools.partial(_matmul_body, has_bias=bias is not None, use_gelu=use_gelu,
                             has_residual=residual is not None, dilation=dilation)
    return pl.pallas_call(
        body,
        grid=(m // tm, n // tn),
        in_specs=in_specs,
        out_specs=out_spec,
        out_shape=out_shape,
        scratch_shapes=scratch,
        compiler_params=_params(("parallel", "arbitrary")),
        name="matmul",
    )(*args)


def _sgu_gate_body(u_ref, v_ref, gain_ref, vbias_ref, ws_ref, bs_ref, o_ref, *, chunks):
    v = v_ref[...].astype(jnp.float32)
    mu = jnp.mean(v, axis=-1, keepdims=True)
    vc = v - mu
    var = jnp.mean(vc * vc, axis=-1, keepdims=True)
    vn = (vc * lax.rsqrt(var + EPS) * gain_ref[...] + vbias_ref[...]).astype(jnp.bfloat16)
    n_groups = ws_ref.shape[0]
    t_idx = lax.broadcasted_iota(jnp.int32, (SGU_CHUNK, SGU_CHUNK), 0)
    s_idx = lax.broadcasted_iota(jnp.int32, (SGU_CHUNK, SGU_CHUNK), 1)
    causal = s_idx <= t_idx
    for g in range(n_groups):
        cols = slice(g * SGU_GROUP_DIM, (g + 1) * SGU_GROUP_DIM)
        w = jnp.where(causal, ws_ref[g], jnp.zeros((), ws_ref.dtype))
        for c in range(chunks):
            rows = slice(c * SGU_CHUNK, (c + 1) * SGU_CHUNK)
            vm = jnp.dot(w, vn[rows, cols], preferred_element_type=jnp.float32) + bs_ref[:, cols]
            o_ref[rows, cols] = (u_ref[rows, cols].astype(jnp.float32) * vm).astype(o_ref.dtype)


def _sgu_gate(z, v_gain, v_bias, w_s, b_s, chunks=2):
    n, two_w = z.shape
    width = two_w // 2
    rows = chunks * SGU_CHUNK
    n_groups = w_s.shape[0]
    b_full = jnp.repeat(b_s.T, SGU_GROUP_DIM, axis=1)
    body = functools.partial(_sgu_gate_body, chunks=chunks)
    return pl.pallas_call(
        body,
        grid=(n // rows,),
        in_specs=[pl.BlockSpec((rows, width), lambda i: (i, 0)),
                  pl.BlockSpec((rows, width), lambda i: (i, 1)),
                  pl.BlockSpec((1, width), lambda i: (0, 0)),
                  pl.BlockSpec((1, width), lambda i: (0, 0)),
                  pl.BlockSpec((n_groups, SGU_CHUNK, SGU_CHUNK), lambda i: (0, 0, 0)),
                  pl.BlockSpec((SGU_CHUNK, width), lambda i: (0, 0))],
        out_specs=pl.BlockSpec((rows, width), lambda i: (i, 0)),
        out_shape=jax.ShapeDtypeStruct((n, width), jnp.bfloat16),
        compiler_params=_params(("parallel",)),
        name="sgu_gate",
    )(z, z, v_gain.reshape(1, width), v_bias.reshape(1, width), w_s.astype(jnp.bfloat16), b_full)


def _t5_causal_bucket(dist):
    max_exact = NUM_BUCKETS // 2
    d = jnp.maximum(dist, 0)
    df = jnp.maximum(d, 1).astype(jnp.float32)
    large = max_exact + (jnp.log(df / max_exact) / math.log(MAX_DISTANCE / max_exact)
                         * (NUM_BUCKETS - max_exact)).astype(jnp.int32)
    large = jnp.minimum(large, NUM_BUCKETS - 1)
    return jnp.where(d < max_exact, d, large)


def _band_bias(bias_table, window, dilation):
    span = window // dilation
    qi = jnp.arange(DIL_BLOCK)[:, None]
    kj = jnp.arange(2 * DIL_BLOCK)[None, :]
    sub_dist = qi + DIL_BLOCK - kj
    band = (sub_dist >= 0) & (sub_dist <= span)
    bucket = _t5_causal_bucket(sub_dist * dilation)
    onehot = (bucket[:, :, None] == jnp.arange(NUM_BUCKETS)).astype(jnp.float32)
    bias = jnp.einsum('qkb,bh->hqk', onehot, bias_table.astype(jnp.float32), precision=lax.Precision.HIGHEST)
    return jnp.where(band[None], bias, MASK_VALUE)


def _attn_body(q_ref, kp_ref, kc_ref, vp_ref, vc_ref, bias_ref, o_ref, lse_ref):
    first = pl.program_id(2) == 0
    key_idx = lax.broadcasted_iota(jnp.int32, (DIL_BLOCK, 2 * DIL_BLOCK), 1)
    prev_invalid = jnp.logical_and(first, key_idx < DIL_BLOCK)
    lane = lax.broadcasted_iota(jnp.int32, (DIL_BLOCK, LANES), 1)
    lse_all = jnp.zeros((DIL_BLOCK, LANES), jnp.float32)
    scale = HEAD_DIM ** -0.5
    for h in range(DIL_HEADS):
        cols = slice(h * HEAD_DIM, (h + 1) * HEAD_DIM)
        q = q_ref[0, 0, :, cols]
        k = jnp.concatenate([kp_ref[0, 0, :, cols], kc_ref[0, 0, :, cols]], axis=0)
        v = jnp.concatenate([vp_ref[0, 0, :, cols], vc_ref[0, 0, :, cols]], axis=0)
        s = lax.dot_general(q, k, (((1,), (1,)), ((), ())), preferred_element_type=jnp.float32)
        s = s * scale + bias_ref[h]
        s = jnp.where(prev_invalid, MASK_VALUE, s)
        m = jnp.max(s, axis=-1, keepdims=True)
        p = jnp.exp(s - m)
        den = jnp.sum(p, axis=-1, keepdims=True)
        o = jnp.dot(p.astype(jnp.bfloat16), v, preferred_element_type=jnp.float32) / den
        o_ref[0, 0, :, cols] = o.astype(o_ref.dtype)
        lse_all = jnp.where(lane == h, m + jnp.log(den), lse_all)
    lse_ref[0, 0] = lse_all


def _dilated_group(qkv, band_bias):
    batch, dilation, sub_len, _ = qkv.shape
    nb = sub_len // DIL_BLOCK

    def spec(part, prev):
        def index_map(b, r, n):
            return (b, r, jnp.maximum(n - 1, 0) if prev else n, part)
        return pl.BlockSpec((1, 1, DIL_BLOCK, DIL_INNER), index_map)

    return pl.pallas_call(
        _attn_body,
        grid=(batch, dilation, nb),
        in_specs=[spec(0, False), spec(1, True), spec(1, False), spec(2, True), spec(2, False),
                  pl.BlockSpec((DIL_HEADS, DIL_BLOCK, 2 * DIL_BLOCK), lambda b, r, n: (0, 0, 0))],
        out_specs=[pl.BlockSpec((1, 1, DIL_BLOCK, DIL_INNER), lambda b, r, n: (b, r, n, 0)),
                   pl.BlockSpec((1, 1, DIL_BLOCK, LANES), lambda b, r, n: (b, r, n, 0))],
        out_shape=[jax.ShapeDtypeStruct((batch, dilation, sub_len, DIL_INNER), jnp.bfloat16),
                   jax.ShapeDtypeStruct((batch, dilation, sub_len, LANES), jnp.float32)],
        compiler_params=_params(("parallel", "parallel", "arbitrary")),
        name="dilated_attn",
    )(qkv, qkv, qkv, qkv, qkv, band_bias)


def _merge_body(*refs, dilations):
    n_g = len(dilations)
    o_refs, l_refs = refs[:n_g], refs[n_g:2 * n_g]
    out_ref, ltok, wtok, acc = refs[2 * n_g:]
    rows = out_ref.shape[0]

    def residue_rows(r, d):
        return pl.ds(r, rows // d, stride=d) if d > 1 else slice(None)

    for g, d in enumerate(dilations):
        for r in range(d):
            ltok[g, residue_rows(r, d), :] = l_refs[g][0, r]
    lse = [ltok[g] for g in range(n_g)]
    m = functools.reduce(jnp.maximum, lse)
    e = [jnp.exp(l - m) for l in lse]
    tot = functools.reduce(jnp.add, e)
    for g in range(n_g):
        wtok[g] = e[g] / tot
    for g, d in enumerate(dilations):
        for r in range(d):
            rr = residue_rows(r, d)
            w = wtok[g, rr, :]
            for h in range(DIL_HEADS):
                cols = slice(h * HEAD_DIM, (h + 1) * HEAD_DIM)
                term = o_refs[g][0, r, :, cols].astype(jnp.float32) * w[:, h:h + 1]
                if g == 0:
                    acc[h, rr, :] = term
                else:
                    acc[h, rr, :] = acc[h, rr, :] + term
    for h in range(DIL_HEADS):
        out_ref[:, h * HEAD_DIM:(h + 1) * HEAD_DIM] = acc[h].astype(out_ref.dtype)


def _merge_groups(outs, lses, rows=512):
    batch = outs[0].shape[0]
    dilations = tuple(o.shape[1] for o in outs)
    seq = outs[0].shape[1] * outs[0].shape[2]
    tiles = seq // rows

    def spec(d, width):
        return pl.BlockSpec((1, d, rows // d, width), lambda i: (i // tiles, 0, i % tiles, 0))

    n_g = len(outs)
    return pl.pallas_call(
        functools.partial(_merge_body, dilations=dilations),
        grid=(batch * tiles,),
        in_specs=[spec(d, DIL_INNER) for d in dilations] + [spec(d, LANES) for d in dilations],
        out_specs=pl.BlockSpec((rows, DIL_INNER), lambda i: (i, 0)),
        out_shape=jax.ShapeDtypeStruct((batch * seq, DIL_INNER), jnp.bfloat16),
        scratch_shapes=[pltpu.VMEM((n_g, rows, LANES), jnp.float32),
                        pltpu.VMEM((n_g, rows, LANES), jnp.float32),
                        pltpu.VMEM((DIL_HEADS, rows, HEAD_DIM), jnp.float32)],
        compiler_params=_params(("parallel",)),
        name="merge_groups",
    )(*outs, *lses)


def _pack_halves(v):
    c = v.shape[1] // 2
    r = v.astype(jnp.bfloat16).astype(jnp.float32)
    lo = lax.bitcast_convert_type(r[:, :c], jnp.uint32)
    hi = lax.bitcast_convert_type(r[:, c:], jnp.uint32)
    return (lo >> 16) | (hi & jnp.uint32(0xFFFF0000))


def _unpack_halves(w):
    lo = lax.bitcast_convert_type(w << 16, jnp.float32)
    hi = lax.bitcast_convert_type(w & jnp.uint32(0xFFFF0000), jnp.float32)
    return jnp.concatenate([lo, hi], axis=1)


def _router_body(x_ref, g_ref, w_ref, b_ref, o_ref, hp_ref, cnt_ref, carry):
    i = pl.program_id(0)

    @pl.when(i == 0)
    def _():
        carry[...] = jnp.zeros_like(carry)

    x = x_ref[...]
    ms = jnp.mean(x * x, axis=-1, keepdims=True)
    hf = x * lax.rsqrt(ms + EPS) * g_ref[...]
    hp_ref[...] = _pack_halves(hf)
    logits = jnp.dot(hf.astype(jnp.bfloat16), w_ref[...], preferred_element_type=jnp.float32) + b_ref[...]
    rows = logits.shape[0]
    lane = lax.broadcasted_iota(jnp.int32, logits.shape, 1)
    neg = -jnp.inf
    gl = jnp.where(lane < N_EXPERT_GROUPS, logits, neg)
    gmax = jnp.max(gl, axis=-1, keepdims=True)
    g_idx = jnp.min(jnp.where(gl == gmax, lane, LANES), axis=-1, keepdims=True)
    g_w = 1.0 / jnp.sum(jnp.exp(gl - gmax), axis=-1, keepdims=True)
    lo = N_EXPERT_GROUPS + EXPERTS_PER_GROUP * g_idx
    el = jnp.where(jnp.logical_and(lane >= lo, lane < lo + EXPERTS_PER_GROUP), logits, neg)
    v1 = jnp.max(el, axis=-1, keepdims=True)
    i1 = jnp.min(jnp.where(el == v1, lane, LANES), axis=-1, keepdims=True)
    el2 = jnp.where(lane == i1, neg, el)
    v2 = jnp.max(el2, axis=-1, keepdims=True)
    i2 = jnp.min(jnp.where(el2 == v2, lane, LANES), axis=-1, keepdims=True)
    e2 = jnp.exp(v2 - v1)
    p1 = 1.0 / (1.0 + e2)
    p2 = e2 / (1.0 + e2)
    oh1 = lane == i1 - N_EXPERT_GROUPS
    oh2 = lane == i2 - N_EXPERT_GROUPS
    oh1f, oh2f = oh1.astype(jnp.float32), oh2.astype(jnp.float32)
    before = (lax.broadcasted_iota(jnp.int32, (rows, rows), 1)
              < lax.broadcasted_iota(jnp.int32, (rows, rows), 0)).astype(jnp.bfloat16)
    pre1 = jnp.dot(before, oh1f.astype(jnp.bfloat16), preferred_element_type=jnp.float32)
    pre2 = jnp.dot(before, oh2f.astype(jnp.bfloat16), preferred_element_type=jnp.float32)
    cnt1 = jnp.sum(oh1f, axis=0, keepdims=True)
    cnt2 = jnp.sum(oh2f, axis=0, keepdims=True)
    base = carry[...]
    rank1 = jnp.sum(jnp.where(oh1, pre1 + base, 0.0), axis=-1, keepdims=True)
    rank2 = jnp.sum(jnp.where(oh2, pre2 + (base + cnt1), 0.0), axis=-1, keepdims=True)
    total = base + cnt1 + cnt2
    carry[...] = total
    cnt_ref[...] = total
    out = jnp.where(lane == 0, (i1 - N_EXPERT_GROUPS).astype(jnp.float32),
          jnp.where(lane == 1, (i2 - N_EXPERT_GROUPS).astype(jnp.float32),
          jnp.where(lane == 2, g_w * p1,
          jnp.where(lane == 3, g_w * p2,
          jnp.where(lane == 4, rank1,
          jnp.where(lane == 5, rank2, 0.0))))))
    o_ref[...] = out


def _router(x, g, w_rg, b_rg, w_re, b_re, rows=512):
    n, d = x.shape
    pad = LANES - N_EXPERT_GROUPS - N_EXPERTS
    w = jnp.concatenate([w_rg, w_re, jnp.zeros((d, pad), w_rg.dtype)], axis=1).astype(jnp.bfloat16)
    b = jnp.concatenate([b_rg, b_re, jnp.zeros((pad,), b_rg.dtype)]).reshape(1, LANES)
    return pl.pallas_call(
        _router_body,
        grid=(n // rows,),
        in_specs=[pl.BlockSpec((rows, d), lambda i: (i, 0)),
                  pl.BlockSpec((1, d), lambda i: (0, 0)),
                  pl.BlockSpec((d, LANES), lambda i: (0, 0)),
                  pl.BlockSpec((1, LANES), lambda i: (0, 0))],
        out_specs=[pl.BlockSpec((rows, LANES), lambda i: (i, 0)),
                   pl.BlockSpec((rows, d // 2), lambda i: (i, 0)),
                   pl.BlockSpec((1, LANES), lambda i: (0, 0))],
        out_shape=[jax.ShapeDtypeStruct((n, LANES), jnp.float32),
                   jax.ShapeDtypeStruct((n, d // 2), jnp.uint32),
                   jax.ShapeDtypeStruct((1, LANES), jnp.float32)],
        scratch_shapes=[pltpu.VMEM((1, LANES), jnp.float32)],
        compiler_params=_params(("arbitrary",)),
        name="router",
    )(x, g.reshape(1, d), w, b)


def _dispatch_body(s1_ref, s2_ref, hp_ref, xb_in_hbm, xb_hbm, stage, sem):
    del xb_in_hbm
    i = pl.program_id(0)
    n_steps = pl.num_programs(0)
    rows = s1_ref.shape[2]
    slot = i % 2

    def wait_slot(of_slot):
        for _ in range(TOP_K):
            pltpu.make_async_copy(stage.at[of_slot], stage.at[of_slot], sem.at[of_slot]).wait()

    @pl.when(i >= 2)
    def _():
        wait_slot(slot)

    stage[slot] = hp_ref[...]

    def issue(r, carry):
        src = stage.at[slot, pl.ds(r, 1)]
        pltpu.make_async_copy(src, xb_hbm.at[pl.ds(s1_ref[0, 0, r], 1)], sem.at[slot]).start()
        pltpu.make_async_copy(src, xb_hbm.at[pl.ds(s2_ref[0, 0, r], 1)], sem.at[slot]).start()
        return carry
    lax.fori_loop(0, rows, issue, 0, unroll=8)

    @pl.when(i == n_steps - 1)
    def _():
        @pl.when(i >= 1)
        def _():
            wait_slot(1 - slot)
        wait_slot(slot)


def _dispatch(hp, slot1, slot2, p_total, rows=512):
    n, c = hp.shape
    steps = n // rows
    tab = pl.BlockSpec((1, 1, rows), lambda i: (i, 0, 0), memory_space=pltpu.SMEM)
    any_spec = pl.BlockSpec(memory_space=pl.ANY)
    return pl.pallas_call(
        _dispatch_body,
        grid=(steps,),
        in_specs=[tab, tab, pl.BlockSpec((rows, c), lambda i: (i, 0)), any_spec],
        out_specs=any_spec,
        out_shape=jax.ShapeDtypeStruct((p_total, c), hp.dtype),
        scratch_shapes=[pltpu.VMEM((2, rows, c), hp.dtype),
                        pltpu.SemaphoreType.DMA((2,))],
        input_output_aliases={3: 0},
        compiler_params=_params(("arbitrary",)),
        name="moe_dispatch",
    )(slot1.reshape(steps, 1, rows), slot2.reshape(steps, 1, rows), hp, jnp.zeros((p_total, c), hp.dtype))


def _expert_body(blk_e_ref, nvalid_ref, xb_ref, wg_ref, wu_ref, wd_ref, yb_ref):
    del blk_e_ref
    i = pl.program_id(0)

    @pl.when(nvalid_ref[i] > 0)
    def _():
        h = _unpack_halves(xb_ref[...]).astype(jnp.bfloat16)
        a = jax.nn.silu(jnp.dot(h, wg_ref[0], preferred_element_type=jnp.float32))
        a = a * jnp.dot(h, wu_ref[0], preferred_element_type=jnp.float32)
        y = jnp.dot(a.astype(jnp.bfloat16), wd_ref[0], preferred_element_type=jnp.float32)
        yb_ref[...] = _pack_halves(y)

    @pl.when(nvalid_ref[i] == 0)
    def _():
        yb_ref[...] = jnp.zeros_like(yb_ref)


def _moe_experts(xb, blk_e, nvalid, w_gate, w_up, w_down):
    p_total, c = xb.shape
    d = 2 * c
    d_e = w_gate.shape[-1]
    grid_spec = pltpu.PrefetchScalarGridSpec(
        num_scalar_prefetch=2,
        grid=(p_total // MOE_ROWS,),
        in_specs=[pl.BlockSpec((MOE_ROWS, c), lambda i, e, nv: (i, 0)),
                  pl.BlockSpec((1, d, d_e), lambda i, e, nv: (e[i], 0, 0)),
                  pl.BlockSpec((1, d, d_e), lambda i, e, nv: (e[i], 0, 0)),
                  pl.BlockSpec((1, d_e, d), lambda i, e, nv: (e[i], 0, 0))],
        out_specs=pl.BlockSpec((MOE_ROWS, c), lambda i, e, nv: (i, 0)),
    )
    return pl.pallas_call(
        _expert_body,
        grid_spec=grid_spec,
        out_shape=jax.ShapeDtypeStruct((p_total, c), xb.dtype),
        compiler_params=_params(("arbitrary",)),
        name="moe_experts",
    )(blk_e, nvalid, xb, w_gate, w_up, w_down)


def _combine_body(s1_ref, s2_ref, s1n_ref, s2n_ref, x_ref, routed_ref, g_ref, yb_hbm, *refs, emit_sum):
    out_refs, ybuf, sem = refs[:-2], refs[-2], refs[-1]
    i = pl.program_id(0)
    n_steps = pl.num_programs(0)
    slot = i % 2
    rows = x_ref.shape[0]

    def gather(t1_ref, t2_ref, to_slot):
        def issue(r, carry):
            pltpu.make_async_copy(yb_hbm.at[pl.ds(t1_ref[0, 0, r], 1)], ybuf.at[to_slot, pl.ds(r, 1)],
                                  sem.at[to_slot]).start()
            pltpu.make_async_copy(yb_hbm.at[pl.ds(t2_ref[0, 0, r], 1)], ybuf.at[to_slot, pl.ds(rows + r, 1)],
                                  sem.at[to_slot]).start()
            return carry
        lax.fori_loop(0, rows, issue, 0, unroll=8)

    @pl.when(i == 0)
    def _():
        gather(s1_ref, s2_ref, 0)

    @pl.when(i + 1 < n_steps)
    def _():
        gather(s1n_ref, s2n_ref, 1 - slot)

    pltpu.make_async_copy(ybuf.at[slot], ybuf.at[slot], sem.at[slot]).wait()
    y1 = _unpack_halves(ybuf[slot, :rows])
    y2 = _unpack_halves(ybuf[slot, rows:])
    s = x_ref[...] + (y1 * routed_ref[:, 2:3] + y2 * routed_ref[:, 3:4])
    if emit_sum:
        out_refs[0][...] = s
    ms = jnp.mean(s * s, axis=-1, keepdims=True)
    out_refs[-1][...] = (s * lax.rsqrt(ms + EPS) * g_ref[...]).astype(out_refs[-1].dtype)


def _combine_norm(x, yb, routed, slot1, slot2, g, norm_dtype, emit_sum, rows=256):
    n, d = x.shape
    steps = n // rows
    row_spec = pl.BlockSpec((rows, d), lambda i: (i, 0))
    tab = pl.BlockSpec((1, 1, rows), lambda i: (i, 0, 0), memory_space=pltpu.SMEM)
    tab_next = pl.BlockSpec((1, 1, rows), lambda i: (jnp.minimum(i + 1, steps - 1), 0, 0),
                            memory_space=pltpu.SMEM)
    out_specs = [row_spec]
    out_shape = [jax.ShapeDtypeStruct((n, d), norm_dtype)]
    if emit_sum:
        out_specs = [row_spec] + out_specs
        out_shape = [jax.ShapeDtypeStruct((n, d), x.dtype)] + out_shape
    s1 = slot1.reshape(steps, 1, rows)
    s2 = slot2.reshape(steps, 1, rows)
    return pl.pallas_call(
        functools.partial(_combine_body, emit_sum=emit_sum),
        grid=(steps,),
        in_specs=[tab, tab, tab_next, tab_next, row_spec,
                  pl.BlockSpec((rows, LANES), lambda i: (i, 0)),
                  pl.BlockSpec((1, d), lambda i: (0, 0)),
                  pl.BlockSpec(memory_space=pl.ANY)],
        out_specs=out_specs,
        out_shape=out_shape,
        scratch_shapes=[pltpu.VMEM((2, TOP_K * rows, d // 2), yb.dtype),
                        pltpu.SemaphoreType.DMA((2,))],
        compiler_params=_params(("arbitrary",)),
        name="combine_norm",
    )(s1, s2, s1, s2, x, routed, g.reshape(1, d), yb)


def _moe_layer(x, ffn_g, next_g, norm_dtype, emit_sum, w_rg, b_rg, w_re, b_re, w_gate, w_up, w_down, layer):
    n, d = x.shape
    routed, hp, counts = _router(x, ffn_g, w_rg, b_rg, w_re, b_re)

    n_blk = n * TOP_K // MOE_ROWS + N_EXPERTS
    experts = jnp.arange(N_EXPERTS, dtype=jnp.int32)
    counts = counts[0, :N_EXPERTS].astype(jnp.int32)
    padded = ((counts + MOE_ROWS - 1) // MOE_ROWS) * MOE_ROWS
    pends = jnp.cumsum(padded)
    pstarts = pends - padded
    expert_id = routed[:, 0:TOP_K].astype(jnp.int32)
    rank = routed[:, 2 * TOP_K:3 * TOP_K].astype(jnp.int32)
    slot = jnp.sum(jnp.where(expert_id[:, :, None] == experts, pstarts, 0), axis=-1) + rank
    blk_start = jnp.arange(n_blk, dtype=jnp.int32) * MOE_ROWS
    blk_e = jnp.minimum(jnp.sum(blk_start[:, None] >= pends[None, :], axis=1), N_EXPERTS - 1).astype(jnp.int32)
    seg_end = jnp.sum(jnp.where(blk_e[:, None] == experts, pstarts + counts, 0), axis=1)
    nvalid = jnp.clip(seg_end - blk_start, 0, MOE_ROWS).astype(jnp.int32)

    xb = _dispatch(hp, slot[:, 0], slot[:, 1], n_blk * MOE_ROWS)
    yb = _moe_experts(xb, blk_e, nvalid, *[_cast_layer(w, layer, jnp.bfloat16) for w in (w_gate, w_up, w_down)])
    return _combine_norm(x, yb, routed, slot[:, 0], slot[:, 1], next_g, norm_dtype, emit_sum)


def kernel(x, mix_norm, ffn_norm, sgu_w_in, sgu_b_in, sgu_v_gain, sgu_v_bias, sgu_w_spatial, sgu_b_spatial,
           sgu_w_out, dil_w_qkv, dil_w_out, rel_bias, router_w_group, router_b_group, router_w_expert,
           router_b_expert, moe_w_gate, moe_w_up, moe_w_down, final_norm):
    batch, seq, d = x.shape
    n = batch * seq
    bf16 = jnp.bfloat16
    xf = x.reshape(n, d)
    depth = mix_norm.shape[0]
    n_dil = len(DIL_CONFIGS)
    h = _rmsnorm(xf, mix_norm[0], bf16)
    out = None
    for i in range(depth):
        j = i // 2
        if i % 2 == 0:
            z = _matmul(h, sgu_w_in[j], bias=sgu_b_in[j], use_gelu=True, out_dtype=bf16)
            y = _sgu_gate(z, sgu_v_gain[j], sgu_v_bias[j], sgu_w_spatial[j], sgu_b_spatial[j])
            xf = _matmul(y, sgu_w_out[j], residual=xf, out_dtype=jnp.float32)
        else:
            outs, lses = [], []
            tn = 512
            per_part = DIL_INNER // tn
            for g, (window, dilation) in enumerate(DIL_CONFIGS):
                qkv = _matmul(h, dil_w_qkv[j], out_dtype=bf16, n_out=3 * DIL_INNER, tn=tn,
                              col_block_map=lambda c, g=g: ((c // per_part) * n_dil + g) * per_part + c % per_part,
                              regroup=(batch, seq, dilation))
                bias = _band_bias(rel_bias[:, g * DIL_HEADS:(g + 1) * DIL_HEADS], window, dilation)
                o, lse = _dilated_group(qkv, bias)
                outs.append(o)
                lses.append(lse)
            merged = _merge_groups(outs, lses)
            xf = _matmul(merged, dil_w_out[j], residual=xf, out_dtype=jnp.float32)
        last = i + 1 == depth
        res = _moe_layer(xf, ffn_norm[i], final_norm if last else mix_norm[i + 1],
                         x.dtype if last else bf16, not last,
                         router_w_group[i], router_b_group[i], router_w_expert[i], router_b_expert[i],
                         moe_w_gate, moe_w_up, moe_w_down, i)
        if last:
            (out,) = res
        else:
            xf, h = res
    return out.reshape(batch, seq, d)
```

```python
import functools
import math

import jax
import jax.numpy as jnp
from jax import lax
from jax.experimental import pallas as pl
from jax.experimental.pallas import tpu as pltpu

EPS = 1e-6
SGU_CHUNK = 128
SGU_GROUP_DIM = 128
DIL_CONFIGS = ((128, 1), (512, 4), (2048, 16))
DIL_HEADS = 16
HEAD_DIM = 128
DIL_BLOCK = 128
DIL_INNER = DIL_HEADS * HEAD_DIM
NUM_BUCKETS = 32
MAX_DISTANCE = 2048
N_EXPERT_GROUPS = 4
EXPERTS_PER_GROUP = 8
N_EXPERTS = N_EXPERT_GROUPS * EXPERTS_PER_GROUP
TOP_K = 2
MASK_VALUE = -1e30

LANES = 128
MOE_ROWS = 256
VMEM_LIMIT = 52 * 1024 * 1024


def _params(sem, vmem=VMEM_LIMIT):
    return pltpu.CompilerParams(dimension_semantics=sem, vmem_limit_bytes=vmem)


def _rmsnorm_body(x_ref, g_ref, o_ref):
    x = x_ref[...]
    ms = jnp.mean(x * x, axis=-1, keepdims=True)
    o_ref[...] = (x * lax.rsqrt(ms + EPS) * g_ref[...]).astype(o_ref.dtype)


def _rmsnorm(x, g, out_dtype, rows=512):
    n, d = x.shape
    return pl.pallas_call(
        _rmsnorm_body,
        grid=(n // rows,),
        in_specs=[pl.BlockSpec((rows, d), lambda i: (i, 0)),
                  pl.BlockSpec((1, d), lambda i: (0, 0))],
        out_specs=pl.BlockSpec((rows, d), lambda i: (i, 0)),
        out_shape=jax.ShapeDtypeStruct((n, d), out_dtype),
        compiler_params=_params(("parallel",)),
        name="rmsnorm",
    )(x, g.reshape(1, d))


def _matmul_body(*refs, has_bias, use_gelu, has_residual, dilation, side_slabs):
    n_side = len(side_slabs)
    a_ref, w_ref = refs[0], refs[1]
    pos = 2
    acc = jnp.dot(a_ref[...], w_ref[...].astype(a_ref.dtype), preferred_element_type=jnp.float32)
    if has_bias:
        acc = acc + refs[pos][...]
        pos += 1
    if use_gelu:
        acc = jax.nn.gelu(acc)
    if has_residual:
        acc = refs[pos][...] + acc
        pos += 1
    side_in = refs[pos:pos + n_side]
    o_ref = refs[pos + n_side]
    side_out = refs[pos + n_side + 1:pos + 2 * n_side + 1]
    if dilation == 1:
        o_ref[...] = acc.astype(o_ref.dtype).reshape(o_ref.shape)
    else:
        acc_ref = refs[pos + 2 * n_side + 1]
        sub = acc.shape[0] // dilation
        for panel in range(acc_ref.shape[0]):
            cols = slice(panel * LANES, (panel + 1) * LANES)
            acc_ref[panel] = acc[:, cols]
            for r in range(dilation):
                o_ref[0, r, :, cols] = acc_ref[panel, pl.ds(r, sub, stride=dilation), :].astype(o_ref.dtype)
    step = pl.program_id(0) * pl.num_programs(1) + pl.program_id(1)
    for s_in, s_out, n_slabs in zip(side_in, side_out, side_slabs):
        @pl.when(step < n_slabs)
        def _(s_in=s_in, s_out=s_out):
            s_out[...] = s_in[...].astype(s_out.dtype)


def _matmul(a, w, *, bias=None, use_gelu=False, residual=None, out_dtype, col_block_map=None, n_out=None,
            regroup=None, side=(), tm=1024, tn=512):
    m, k = a.shape
    n = w.shape[1] if n_out is None else n_out
    tm, tn = min(tm, m), min(tn, n)
    grid = (m // tm, n // tn)
    w_map = (lambda i, j: (0, j)) if col_block_map is None else (lambda i, j: (0, col_block_map(j)))
    in_specs = [pl.BlockSpec((tm, k), lambda i, j: (i, 0)),
                pl.BlockSpec((k, tn), w_map)]
    args = [a, w]
    if bias is not None:
        in_specs.append(pl.BlockSpec((1, tn), lambda i, j: (0, j)))
        args.append(bias.reshape(1, n))
    if residual is not None:
        in_specs.append(pl.BlockSpec((tm, tn), lambda i, j: (i, j)))
        args.append(residual)
    scratch = []
    dilation = 1
    if regroup is None:
        out_specs = [pl.BlockSpec((tm, tn), lambda i, j: (i, j))]
        out_shape = [jax.ShapeDtypeStruct((m, n), out_dtype)]
    else:
        batch, seq, dilation = regroup
        tiles = seq // tm
        out_specs = [pl.BlockSpec((1, dilation, tm // dilation, tn), lambda i, j: (i // tiles, 0, i % tiles, j))]
        out_shape = [jax.ShapeDtypeStruct((batch, dilation, seq // dilation, n), out_dtype)]
        if dilation > 1:
            scratch = [pltpu.VMEM((tn // LANES, tm, LANES), jnp.float32)]
    side_slabs = []
    for stack, layer in side:
        cols = stack.shape[-1]
        flat = stack.reshape(-1, cols)
        layer_rows = flat.shape[0] // stack.shape[0]
        n_slabs = 1
        while n_slabs * 2 <= grid[0] * grid[1] and layer_rows % (n_slabs * 2) == 0:
            n_slabs *= 2
        slab_rows = layer_rows // n_slabs

        def slab(i, j, n_slabs=n_slabs):
            return jnp.minimum(i * grid[1] + j, n_slabs - 1)
        in_specs.append(pl.BlockSpec((slab_rows, cols),
                                     lambda i, j, slab=slab, first=layer * n_slabs: (first + slab(i, j), 0)))
        args.append(flat)
        out_specs.append(pl.BlockSpec((slab_rows, cols), lambda i, j, slab=slab: (slab(i, j), 0)))
        out_shape.append(jax.ShapeDtypeStruct((layer_rows, cols), a.dtype))
        side_slabs.append(n_slabs)
    body = functools.partial(_matmul_body, has_bias=bias is not None, use_gelu=use_gelu,
                             has_residual=residual is not None, dilation=dilation, side_slabs=tuple(side_slabs))
    outs = pl.pallas_call(
        body,
        grid=grid,
        in_specs=in_specs,
        out_specs=out_specs,
        out_shape=out_shape,
        scratch_shapes=scratch,
        compiler_params=_params(("arbitrary", "arbitrary") if side else ("parallel", "arbitrary")),
        name="matmul",
    )(*args)
    if not side:
        return outs[0]
    return outs[0], [o.reshape(stack.shape[1:]) for o, (stack, _) in zip(outs[1:], side)]


def _sgu_gate_body(u_ref, v_ref, gain_ref, vbias_ref, ws_ref, bs_ref, o_ref, *, chunks):
    v = v_ref[...].astype(jnp.float32)
    mu = jnp.mean(v, axis=-1, keepdims=True)
    vc = v - mu
    var = jnp.mean(vc * vc, axis=-1, keepdims=True)
    vn = (vc * lax.rsqrt(var + EPS) * gain_ref[...] + vbias_ref[...]).astype(jnp.bfloat16)
    n_groups = ws_ref.shape[0]
    t_idx = lax.broadcasted_iota(jnp.int32, (SGU_CHUNK, SGU_CHUNK), 0)
    s_idx = lax.broadcasted_iota(jnp.int32, (SGU_CHUNK, SGU_CHUNK), 1)
    causal = s_idx <= t_idx
    for g in range(n_groups):
        cols = slice(g * SGU_GROUP_DIM, (g + 1) * SGU_GROUP_DIM)
        w = jnp.where(causal, ws_ref[g], jnp.zeros((), ws_ref.dtype))
        for c in range(chunks):
            rows = slice(c * SGU_CHUNK, (c + 1) * SGU_CHUNK)
            vm = jnp.dot(w, vn[rows, cols], preferred_element_type=jnp.float32) + bs_ref[:, cols]
            o_ref[rows, cols] = (u_ref[rows, cols].astype(jnp.float32) * vm).astype(o_ref.dtype)


def _sgu_gate(z, v_gain, v_bias, w_s, b_s, chunks=2):
    n, two_w = z.shape
    width = two_w // 2
    rows = chunks * SGU_CHUNK
    n_groups = w_s.shape[0]
    b_full = jnp.repeat(b_s.T, SGU_GROUP_DIM, axis=1)
    body = functools.partial(_sgu_gate_body, chunks=chunks)
    return pl.pallas_call(
        body,
        grid=(n // rows,),
        in_specs=[pl.BlockSpec((rows, width), lambda i: (i, 0)),
                  pl.BlockSpec((rows, width), lambda i: (i, 1)),
                  pl.BlockSpec((1, width), lambda i: (0, 0)),
                  pl.BlockSpec((1, width), lambda i: (0, 0)),
                  pl.BlockSpec((n_groups, SGU_CHUNK, SGU_CHUNK), lambda i: (0, 0, 0)),
                  pl.BlockSpec((SGU_CHUNK, width), lambda i: (0, 0))],
        out_specs=pl.BlockSpec((rows, width), lambda i: (i, 0)),
        out_shape=jax.ShapeDtypeStruct((n, width), jnp.bfloat16),
        compiler_params=_params(("parallel",)),
        name="sgu_gate",
    )(z, z, v_gain.reshape(1, width), v_bias.reshape(1, width), w_s.astype(jnp.bfloat16), b_full)


def _t5_causal_bucket(dist):
    max_exact = NUM_BUCKETS // 2
    d = jnp.maximum(dist, 0)
    df = jnp.maximum(d, 1).astype(jnp.float32)
    large = max_exact + (jnp.log(df / max_exact) / math.log(MAX_DISTANCE / max_exact)
                         * (NUM_BUCKETS - max_exact)).astype(jnp.int32)
    large = jnp.minimum(large, NUM_BUCKETS - 1)
    return jnp.where(d < max_exact, d, large)


def _band_bias(bias_table, window, dilation):
    span = window // dilation
    qi = jnp.arange(DIL_BLOCK)[:, None]
    kj = jnp.arange(2 * DIL_BLOCK)[None, :]
    sub_dist = qi + DIL_BLOCK - kj
    band = (sub_dist >= 0) & (sub_dist <= span)
    bucket = _t5_causal_bucket(sub_dist * dilation)
    onehot = (bucket[:, :, None] == jnp.arange(NUM_BUCKETS)).astype(jnp.float32)
    bias = jnp.einsum('qkb,bh->hqk', onehot, bias_table.astype(jnp.float32), precision=lax.Precision.HIGHEST)
    return jnp.where(band[None], bias, MASK_VALUE)


def _attn_body(q_ref, kp_ref, kc_ref, vp_ref, vc_ref, bias_ref, o_ref, lse_ref):
    first = pl.program_id(2) == 0
    key_idx = lax.broadcasted_iota(jnp.int32, (DIL_BLOCK, 2 * DIL_BLOCK), 1)
    prev_invalid = jnp.logical_and(first, key_idx < DIL_BLOCK)
    lane = lax.broadcasted_iota(jnp.int32, (DIL_BLOCK, LANES), 1)
    lse_all = jnp.zeros((DIL_BLOCK, LANES), jnp.float32)
    scale = HEAD_DIM ** -0.5
    for h in range(DIL_HEADS):
        cols = slice(h * HEAD_DIM, (h + 1) * HEAD_DIM)
        q = q_ref[0, 0, :, cols]
        k = jnp.concatenate([kp_ref[0, 0, :, cols], kc_ref[0, 0, :, cols]], axis=0)
        v = jnp.concatenate([vp_ref[0, 0, :, cols], vc_ref[0, 0, :, cols]], axis=0)
        s = lax.dot_general(q, k, (((1,), (1,)), ((), ())), preferred_element_type=jnp.float32)
        s = s * scale + bias_ref[h]
        s = jnp.where(prev_invalid, MASK_VALUE, s)
        m = jnp.max(s, axis=-1, keepdims=True)
        p = jnp.exp(s - m)
        den = jnp.sum(p, axis=-1, keepdims=True)
        o = jnp.dot(p.astype(jnp.bfloat16), v, preferred_element_type=jnp.float32) / den
        o_ref[0, 0, :, cols] = o.astype(o_ref.dtype)
        lse_all = jnp.where(lane == h, m + jnp.log(den), lse_all)
    lse_ref[0, 0] = lse_all


def _dilated_group(qkv, band_bias):
    batch, dilation, sub_len, _ = qkv.shape
    nb = sub_len // DIL_BLOCK

    def spec(part, prev):
        def index_map(b, r, n):
            return (b, r, jnp.maximum(n - 1, 0) if prev else n, part)
        return pl.BlockSpec((1, 1, DIL_BLOCK, DIL_INNER), index_map)

    return pl.pallas_call(
        _attn_body,
        grid=(batch, dilation, nb),
        in_specs=[spec(0, False), spec(1, True), spec(1, False), spec(2, True), spec(2, False),
                  pl.BlockSpec((DIL_HEADS, DIL_BLOCK, 2 * DIL_BLOCK), lambda b, r, n: (0, 0, 0))],
        out_specs=[pl.BlockSpec((1, 1, DIL_BLOCK, DIL_INNER), lambda b, r, n: (b, r, n, 0)),
                   pl.BlockSpec((1, 1, DIL_BLOCK, LANES), lambda b, r, n: (b, r, n, 0))],
        out_shape=[jax.ShapeDtypeStruct((batch, dilation, sub_len, DIL_INNER), jnp.bfloat16),
                   jax.ShapeDtypeStruct((batch, dilation, sub_len, LANES), jnp.float32)],
        compiler_params=_params(("parallel", "parallel", "arbitrary")),
        name="dilated_attn",
    )(qkv, qkv, qkv, qkv, qkv, band_bias)


def _merge_body(*refs, dilations):
    n_g = len(dilations)
    o_refs, l_refs = refs[:n_g], refs[n_g:2 * n_g]
    out_ref, ltok, wtok, acc = refs[2 * n_g:]
    rows = out_ref.shape[0]

    def residue_rows(r, d):
        return pl.ds(r, rows // d, stride=d) if d > 1 else slice(None)

    for g, d in enumerate(dilations):
        for r in range(d):
            ltok[g, residue_rows(r, d), :] = l_refs[g][0, r]
    lse = [ltok[g] for g in range(n_g)]
    m = functools.reduce(jnp.maximum, lse)
    e = [jnp.exp(l - m) for l in lse]
    tot = functools.reduce(jnp.add, e)
    for g in range(n_g):
        wtok[g] = e[g] / tot
    for g, d in enumerate(dilations):
        for r in range(d):
            rr = residue_rows(r, d)
            w = wtok[g, rr, :]
            for h in range(DIL_HEADS):
                cols = slice(h * HEAD_DIM, (h + 1) * HEAD_DIM)
                term = o_refs[g][0, r, :, cols].astype(jnp.float32) * w[:, h:h + 1]
                if g == 0:
                    acc[h, rr, :] = term
                else:
                    acc[h, rr, :] = acc[h, rr, :] + term
    for h in range(DIL_HEADS):
        out_ref[:, h * HEAD_DIM:(h + 1) * HEAD_DIM] = acc[h].astype(out_ref.dtype)


def _merge_groups(outs, lses, rows=512):
    batch = outs[0].shape[0]
    dilations = tuple(o.shape[1] for o in outs)
    seq = outs[0].shape[1] * outs[0].shape[2]
    tiles = seq // rows

    def spec(d, width):
        return pl.BlockSpec((1, d, rows // d, width), lambda i: (i // tiles, 0, i % tiles, 0))

    n_g = len(outs)
    return pl.pallas_call(
        functools.partial(_merge_body, dilations=dilations),
        grid=(batch * tiles,),
        in_specs=[spec(d, DIL_INNER) for d in dilations] + [spec(d, LANES) for d in dilations],
        out_specs=pl.BlockSpec((rows, DIL_INNER), lambda i: (i, 0)),
        out_shape=jax.ShapeDtypeStruct((batch * seq, DIL_INNER), jnp.bfloat16),
        scratch_shapes=[pltpu.VMEM((n_g, rows, LANES), jnp.float32),
                        pltpu.VMEM((n_g, rows, LANES), jnp.float32),
                        pltpu.VMEM((DIL_HEADS, rows, HEAD_DIM), jnp.float32)],
        compiler_params=_params(("parallel",)),
        name="merge_groups",
    )(*outs, *lses)


def _pack_halves(v):
    c = v.shape[1] // 2
    r = v.astype(jnp.bfloat16).astype(jnp.float32)
    lo = lax.bitcast_convert_type(r[:, :c], jnp.uint32)
    hi = lax.bitcast_convert_type(r[:, c:], jnp.uint32)
    return (lo >> 16) | (hi & jnp.uint32(0xFFFF0000))


def _unpack_halves(w):
    lo = lax.bitcast_convert_type(w << 16, jnp.float32)
    hi = lax.bitcast_convert_type(w & jnp.uint32(0xFFFF0000), jnp.float32)
    return jnp.concatenate([lo, hi], axis=1)


def _router_body(x_ref, g_ref, w_ref, b_ref, o_ref, hp_ref, cnt_ref, carry):
    i = pl.program_id(0)

    @pl.when(i == 0)
    def _():
        carry[...] = jnp.zeros_like(carry)

    x = x_ref[...]
    ms = jnp.mean(x * x, axis=-1, keepdims=True)
    hf = x * lax.rsqrt(ms + EPS) * g_ref[...]
    hp_ref[...] = _pack_halves(hf)
    logits = jnp.dot(hf.astype(jnp.bfloat16), w_ref[...], preferred_element_type=jnp.float32) + b_ref[...]
    rows = logits.shape[0]
    lane = lax.broadcasted_iota(jnp.int32, logits.shape, 1)
    neg = -jnp.inf
    gl = jnp.where(lane < N_EXPERT_GROUPS, logits, neg)
    gmax = jnp.max(gl, axis=-1, keepdims=True)
    g_idx = jnp.min(jnp.where(gl == gmax, lane, LANES), axis=-1, keepdims=True)
    g_w = 1.0 / jnp.sum(jnp.exp(gl - gmax), axis=-1, keepdims=True)
    lo = N_EXPERT_GROUPS + EXPERTS_PER_GROUP * g_idx
    el = jnp.where(jnp.logical_and(lane >= lo, lane < lo + EXPERTS_PER_GROUP), logits, neg)
    v1 = jnp.max(el, axis=-1, keepdims=True)
    i1 = jnp.min(jnp.where(el == v1, lane, LANES), axis=-1, keepdims=True)
    el2 = jnp.where(lane == i1, neg, el)
    v2 = jnp.max(el2, axis=-1, keepdims=True)
    i2 = jnp.min(jnp.where(el2 == v2, lane, LANES), axis=-1, keepdims=True)
    e2 = jnp.exp(v2 - v1)
    p1 = 1.0 / (1.0 + e2)
    p2 = e2 / (1.0 + e2)
    oh1 = lane == i1 - N_EXPERT_GROUPS
    oh2 = lane == i2 - N_EXPERT_GROUPS
    oh1f, oh2f = oh1.astype(jnp.float32), oh2.astype(jnp.float32)
    before = (lax.broadcasted_iota(jnp.int32, (rows, rows), 1)
              < lax.broadcasted_iota(jnp.int32, (rows, rows), 0)).astype(jnp.bfloat16)
    pre1 = jnp.dot(before, oh1f.astype(jnp.bfloat16), preferred_element_type=jnp.float32)
    pre2 = jnp.dot(before, oh2f.astype(jnp.bfloat16), preferred_element_type=jnp.float32)
    cnt1 = jnp.sum(oh1f, axis=0, keepdims=True)
    cnt2 = jnp.sum(oh2f, axis=0, keepdims=True)
    base = carry[...]
    rank1 = jnp.sum(jnp.where(oh1, pre1 + base, 0.0), axis=-1, keepdims=True)
    rank2 = jnp.sum(jnp.where(oh2, pre2 + (base + cnt1), 0.0), axis=-1, keepdims=True)
    total = base + cnt1 + cnt2
    carry[...] = total
    cnt_ref[...] = total
    out = jnp.where(lane == 0, (i1 - N_EXPERT_GROUPS).astype(jnp.float32),
          jnp.where(lane == 1, (i2 - N_EXPERT_GROUPS).astype(jnp.float32),
          jnp.where(lane == 2, g_w * p1,
          jnp.where(lane == 3, g_w * p2,
          jnp.where(lane == 4, rank1,
          jnp.where(lane == 5, rank2, 0.0))))))
    o_ref[...] = out


def _router(x, g, w_rg, b_rg, w_re, b_re, rows=512):
    n, d = x.shape
    pad = LANES - N_EXPERT_GROUPS - N_EXPERTS
    w = jnp.concatenate([w_rg, w_re, jnp.zeros((d, pad), w_rg.dtype)], axis=1).astype(jnp.bfloat16)
    b = jnp.concatenate([b_rg, b_re, jnp.zeros((pad,), b_rg.dtype)]).reshape(1, LANES)
    return pl.pallas_call(
        _router_body,
        grid=(n // rows,),
        in_specs=[pl.BlockSpec((rows, d), lambda i: (i, 0)),
                  pl.BlockSpec((1, d), lambda i: (0, 0)),
                  pl.BlockSpec((d, LANES), lambda i: (0, 0)),
                  pl.BlockSpec((1, LANES), lambda i: (0, 0))],
        out_specs=[pl.BlockSpec((rows, LANES), lambda i: (i, 0)),
                   pl.BlockSpec((rows, d // 2), lambda i: (i, 0)),
                   pl.BlockSpec((1, LANES), lambda i: (0, 0))],
        out_shape=[jax.ShapeDtypeStruct((n, LANES), jnp.float32),
                   jax.ShapeDtypeStruct((n, d // 2), jnp.uint32),
                   jax.ShapeDtypeStruct((1, LANES), jnp.float32)],
        scratch_shapes=[pltpu.VMEM((1, LANES), jnp.float32)],
        compiler_params=_params(("arbitrary",)),
        name="router",
    )(x, g.reshape(1, d), w, b)


def _dispatch_body(s1_ref, s2_ref, hp_ref, xb_in_hbm, xb_hbm, stage, sem):
    del xb_in_hbm
    i = pl.program_id(0)
    n_steps = pl.num_programs(0)
    rows = s1_ref.shape[2]
    slot = i % 2

    def wait_slot(of_slot):
        for _ in range(TOP_K):
            pltpu.make_async_copy(stage.at[of_slot], stage.at[of_slot], sem.at[of_slot]).wait()

    @pl.when(i >= 2)
    def _():
        wait_slot(slot)

    stage[slot] = hp_ref[...]

    def issue(r, carry):
        src = stage.at[slot, pl.ds(r, 1)]
        pltpu.make_async_copy(src, xb_hbm.at[pl.ds(s1_ref[0, 0, r], 1)], sem.at[slot]).start()
        pltpu.make_async_copy(src, xb_hbm.at[pl.ds(s2_ref[0, 0, r], 1)], sem.at[slot]).start()
        return carry
    lax.fori_loop(0, rows, issue, 0, unroll=8)

    @pl.when(i == n_steps - 1)
    def _():
        @pl.when(i >= 1)
        def _():
            wait_slot(1 - slot)
        wait_slot(slot)


def _dispatch(hp, slot1, slot2, p_total, rows=512):
    n, c = hp.shape
    steps = n // rows
    tab = pl.BlockSpec((1, 1, rows), lambda i: (i, 0, 0), memory_space=pltpu.SMEM)
    any_spec = pl.BlockSpec(memory_space=pl.ANY)
    return pl.pallas_call(
        _dispatch_body,
        grid=(steps,),
        in_specs=[tab, tab, pl.BlockSpec((rows, c), lambda i: (i, 0)), any_spec],
        out_specs=any_spec,
        out_shape=jax.ShapeDtypeStruct((p_total, c), hp.dtype),
        scratch_shapes=[pltpu.VMEM((2, rows, c), hp.dtype),
                        pltpu.SemaphoreType.DMA((2,))],
        input_output_aliases={3: 0},
        compiler_params=_params(("arbitrary",)),
        name="moe_dispatch",
    )(slot1.reshape(steps, 1, rows), slot2.reshape(steps, 1, rows), hp, jnp.zeros((p_total, c), hp.dtype))


def _expert_body(blk_e_ref, nvalid_ref, xb_ref, wg_ref, wu_ref, wd_ref, yb_ref):
    del blk_e_ref
    i = pl.program_id(0)

    @pl.when(nvalid_ref[i] > 0)
    def _():
        h = _unpack_halves(xb_ref[...]).astype(jnp.bfloat16)
        a = jax.nn.silu(jnp.dot(h, wg_ref[0], preferred_element_type=jnp.float32))
        a = a * jnp.dot(h, wu_ref[0], preferred_element_type=jnp.float32)
        y = jnp.dot(a.astype(jnp.bfloat16), wd_ref[0], preferred_element_type=jnp.float32)
        yb_ref[...] = _pack_halves(y)

    @pl.when(nvalid_ref[i] == 0)
    def _():
        yb_ref[...] = jnp.zeros_like(yb_ref)


def _moe_experts(xb, blk_e, nvalid, w_gate, w_up, w_down):
    p_total, c = xb.shape
    d = 2 * c
    d_e = w_gate.shape[-1]
    grid_spec = pltpu.PrefetchScalarGridSpec(
        num_scalar_prefetch=2,
        grid=(p_total // MOE_ROWS,),
        in_specs=[pl.BlockSpec((MOE_ROWS, c), lambda i, e, nv: (i, 0)),
                  pl.BlockSpec((1, d, d_e), lambda i, e, nv: (e[i], 0, 0)),
                  pl.BlockSpec((1, d, d_e), lambda i, e, nv: (e[i], 0, 0)),
                  pl.BlockSpec((1, d_e, d), lambda i, e, nv: (e[i], 0, 0))],
        out_specs=pl.BlockSpec((MOE_ROWS, c), lambda i, e, nv: (i, 0)),
    )
    return pl.pallas_call(
        _expert_body,
        grid_spec=grid_spec,
        out_shape=jax.ShapeDtypeStruct((p_total, c), xb.dtype),
        compiler_params=_params(("arbitrary",)),
        name="moe_experts",
    )(blk_e, nvalid, xb, w_gate, w_up, w_down)


def _combine_body(s1_ref, s2_ref, s1n_ref, s2n_ref, x_ref, routed_ref, g_ref, yb_hbm, *refs, emit_sum):
    out_refs, ybuf, sem = refs[:-2], refs[-2], refs[-1]
    i = pl.program_id(0)
    n_steps = pl.num_programs(0)
    slot = i % 2
    rows = x_ref.shape[0]

    def gather(t1_ref, t2_ref, to_slot):
        def issue(r, carry):
            pltpu.make_async_copy(yb_hbm.at[pl.ds(t1_ref[0, 0, r], 1)], ybuf.at[to_slot, pl.ds(r, 1)],
                                  sem.at[to_slot]).start()
            pltpu.make_async_copy(yb_hbm.at[pl.ds(t2_ref[0, 0, r], 1)], ybuf.at[to_slot, pl.ds(rows + r, 1)],
                                  sem.at[to_slot]).start()
            return carry
        lax.fori_loop(0, rows, issue, 0, unroll=8)

    @pl.when(i == 0)
    def _():
        gather(s1_ref, s2_ref, 0)

    @pl.when(i + 1 < n_steps)
    def _():
        gather(s1n_ref, s2n_ref, 1 - slot)

    pltpu.make_async_copy(ybuf.at[slot], ybuf.at[slot], sem.at[slot]).wait()
    y1 = _unpack_halves(ybuf[slot, :rows])
    y2 = _unpack_halves(ybuf[slot, rows:])
    s = x_ref[...] + (y1 * routed_ref[:, 2:3] + y2 * routed_ref[:, 3:4])
    if emit_sum:
        out_refs[0][...] = s
    ms = jnp.mean(s * s, axis=-1, keepdims=True)
    out_refs[-1][...] = (s * lax.rsqrt(ms + EPS) * g_ref[...]).astype(out_refs[-1].dtype)


def _combine_norm(x, yb, routed, slot1, slot2, g, norm_dtype, emit_sum, rows=256):
    n, d = x.shape
    steps = n // rows
    row_spec = pl.BlockSpec((rows, d), lambda i: (i, 0))
    tab = pl.BlockSpec((1, 1, rows), lambda i: (i, 0, 0), memory_space=pltpu.SMEM)
    tab_next = pl.BlockSpec((1, 1, rows), lambda i: (jnp.minimum(i + 1, steps - 1), 0, 0),
                            memory_space=pltpu.SMEM)
    out_specs = [row_spec]
    out_shape = [jax.ShapeDtypeStruct((n, d), norm_dtype)]
    if emit_sum:
        out_specs = [row_spec] + out_specs
        out_shape = [jax.ShapeDtypeStruct((n, d), x.dtype)] + out_shape
    s1 = slot1.reshape(steps, 1, rows)
    s2 = slot2.reshape(steps, 1, rows)
    return pl.pallas_call(
        functools.partial(_combine_body, emit_sum=emit_sum),
        grid=(steps,),
        in_specs=[tab, tab, tab_next, tab_next, row_spec,
                  pl.BlockSpec((rows, LANES), lambda i: (i, 0)),
                  pl.BlockSpec((1, d), lambda i: (0, 0)),
                  pl.BlockSpec(memory_space=pl.ANY)],
        out_specs=out_specs,
        out_shape=out_shape,
        scratch_shapes=[pltpu.VMEM((2, TOP_K * rows, d // 2), yb.dtype),
                        pltpu.SemaphoreType.DMA((2,))],
        compiler_params=_params(("arbitrary",)),
        name="combine_norm",
    )(s1, s2, s1, s2, x, routed, g.reshape(1, d), yb)


def _moe_layer(x, ffn_g, next_g, norm_dtype, emit_sum, w_rg, b_rg, w_re, b_re, w_gate, w_up, w_down):
    n, d = x.shape
    routed, hp, counts = _router(x, ffn_g, w_rg, b_rg, w_re, b_re)

    n_blk = n * TOP_K // MOE_ROWS + N_EXPERTS
    experts = jnp.arange(N_EXPERTS, dtype=jnp.int32)
    counts = counts[0, :N_EXPERTS].astype(jnp.int32)
    padded = ((counts + MOE_ROWS - 1) // MOE_ROWS) * MOE_ROWS
    pends = jnp.cumsum(padded)
    pstarts = pends - padded
    expert_id = routed[:, 0:TOP_K].astype(jnp.int32)
    rank = routed[:, 2 * TOP_K:3 * TOP_K].astype(jnp.int32)
    slot = jnp.sum(jnp.where(expert_id[:, :, None] == experts, pstarts, 0), axis=-1) + rank
    blk_start = jnp.arange(n_blk, dtype=jnp.int32) * MOE_ROWS
    blk_e = jnp.minimum(jnp.sum(blk_start[:, None] >= pends[None, :], axis=1), N_EXPERTS - 1).astype(jnp.int32)
    seg_end = jnp.sum(jnp.where(blk_e[:, None] == experts, pstarts + counts, 0), axis=1)
    nvalid = jnp.clip(seg_end - blk_start, 0, MOE_ROWS).astype(jnp.int32)

    xb = _dispatch(hp, slot[:, 0], slot[:, 1], n_blk * MOE_ROWS)
    yb = _moe_experts(xb, blk_e, nvalid, w_gate, w_up, w_down)
    return _combine_norm(x, yb, routed, slot[:, 0], slot[:, 1], next_g, norm_dtype, emit_sum)


def kernel(x, mix_norm, ffn_norm, sgu_w_in, sgu_b_in, sgu_v_gain, sgu_v_bias, sgu_w_spatial, sgu_b_spatial,
           sgu_w_out, dil_w_qkv, dil_w_out, rel_bias, router_w_group, router_b_group, router_w_expert,
           router_b_expert, moe_w_gate, moe_w_up, moe_w_down, final_norm):
    batch, seq, d = x.shape
    n = batch * seq
    bf16 = jnp.bfloat16
    xf = x.reshape(n, d)
    depth = mix_norm.shape[0]
    n_dil = len(DIL_CONFIGS)
    h = _rmsnorm(xf, mix_norm[0], bf16)
    out = None
    for i in range(depth):
        j = i // 2
        if i % 2 == 0:
            z, (w_gate, w_up) = _matmul(h, sgu_w_in[j], bias=sgu_b_in[j], use_gelu=True, out_dtype=bf16,
                                        side=((moe_w_gate, i), (moe_w_up, i)))
            y = _sgu_gate(z, sgu_v_gain[j], sgu_v_bias[j], sgu_w_spatial[j], sgu_b_spatial[j])
            xf, (w_down,) = _matmul(y, sgu_w_out[j], residual=xf, out_dtype=jnp.float32, side=((moe_w_down, i),))
        else:
            outs, lses, cast = [], [], []
            tn = 512
            per_part = DIL_INNER // tn
            for g, (window, dilation) in enumerate(DIL_CONFIGS):
                qkv, (w_cast,) = _matmul(
                    h, dil_w_qkv[j], out_dtype=bf16, n_out=3 * DIL_INNER, tn=tn,
                    col_block_map=lambda c, g=g: ((c // per_part) * n_dil + g) * per_part + c % per_part,
                    regroup=(batch, seq, dilation), side=(((moe_w_gate, moe_w_up, moe_w_down)[g], i),))
                cast.append(w_cast)
                bias = _band_bias(rel_bias[:, g * DIL_HEADS:(g + 1) * DIL_HEADS], window, dilation)
                o, lse = _dilated_group(qkv, bias)
                outs.append(o)
                lses.append(lse)
            w_gate, w_up, w_down = cast
            merged = _merge_groups(outs, lses)
            xf = _matmul(merged, dil_w_out[j], residual=xf, out_dtype=jnp.float32)
        last = i + 1 == depth
        res = _moe_layer(xf, ffn_norm[i], final_norm if last else mix_norm[i + 1],
                         x.dtype if last else bf16, not last,
                         router_w_group[i], router_b_group[i], router_w_expert[i], router_b_expert[i],
                         w_gate, w_up, w_down)
        if last:
            (out,) = res
        else:
            xf, h = res
    return out.reshape(batch, seq, d)
```

```python
import functools
import math

import jax
import jax.numpy as jnp
from jax import lax
from jax.experimental import pallas as pl
from jax.experimental.pallas import tpu as pltpu

EPS = 1e-6
SGU_CHUNK = 128
SGU_GROUP_DIM = 128
DIL_CONFIGS = ((128, 1), (512, 4), (2048, 16))
DIL_HEADS = 16
HEAD_DIM = 128
DIL_BLOCK = 128
DIL_INNER = DIL_HEADS * HEAD_DIM
NUM_BUCKETS = 32
MAX_DISTANCE = 2048
N_EXPERT_GROUPS = 4
EXPERTS_PER_GROUP = 8
N_EXPERTS = N_EXPERT_GROUPS * EXPERTS_PER_GROUP
TOP_K = 2
MASK_VALUE = -1e30

LANES = 128
MOE_ROWS = 256
VMEM_LIMIT = 52 * 1024 * 1024


def _params(sem, vmem=VMEM_LIMIT):
    return pltpu.CompilerParams(dimension_semantics=sem, vmem_limit_bytes=vmem)


def _rmsnorm_body(x_ref, g_ref, o_ref):
    x = x_ref[...]
    ms = jnp.mean(x * x, axis=-1, keepdims=True)
    o_ref[...] = (x * lax.rsqrt(ms + EPS) * g_ref[...]).astype(o_ref.dtype)


def _rmsnorm(x, g, out_dtype, rows=512):
    n, d = x.shape
    return pl.pallas_call(
        _rmsnorm_body,
        grid=(n // rows,),
        in_specs=[pl.BlockSpec((rows, d), lambda i: (i, 0)),
                  pl.BlockSpec((1, d), lambda i: (0, 0))],
        out_specs=pl.BlockSpec((rows, d), lambda i: (i, 0)),
        out_shape=jax.ShapeDtypeStruct((n, d), out_dtype),
        compiler_params=_params(("parallel",)),
        name="rmsnorm",
    )(x, g.reshape(1, d))


def _regroup_pitch(dilation):
    return dilation + 4 if dilation % 8 == 0 else dilation


def _matmul_body(*refs, has_bias, use_gelu, has_residual, dilation, side_slabs, zero_slabs):
    n_side = len(side_slabs)
    a_ref, w_ref = refs[0], refs[1]
    pos = 2
    acc = jnp.dot(a_ref[...], w_ref[...].astype(a_ref.dtype), preferred_element_type=jnp.float32)
    if has_bias:
        acc = acc + refs[pos][...]
        pos += 1
    if use_gelu:
        acc = jax.nn.gelu(acc)
    if has_residual:
        acc = refs[pos][...] + acc
        pos += 1
    side_in = refs[pos:pos + n_side]
    o_ref = refs[pos + n_side]
    side_out = refs[pos + n_side + 1:pos + 2 * n_side + 1]
    pos += 2 * n_side + 1
    zero_ref = None
    if zero_slabs:
        zero_ref = refs[pos]
        pos += 1
    if dilation == 1:
        o_ref[...] = acc.astype(o_ref.dtype).reshape(o_ref.shape)
    else:
        acc_ref = refs[pos]
        sub = acc.shape[0] // dilation
        pitch = _regroup_pitch(dilation)
        for panel in range(acc_ref.shape[0]):
            cols = slice(panel * LANES, (panel + 1) * LANES)
            if pitch == dilation:
                acc_ref[panel] = acc[:, cols]
            else:
                for l in range(sub):
                    acc_ref[panel, l * pitch:l * pitch + dilation, :] = acc[l * dilation:(l + 1) * dilation, cols]
            for r in range(dilation):
                o_ref[0, r, :, cols] = acc_ref[panel, pl.ds(r, sub, stride=pitch), :].astype(o_ref.dtype)
    step = pl.program_id(0) * pl.num_programs(1) + pl.program_id(1)
    for s_in, s_out, n_slabs in zip(side_in, side_out, side_slabs):
        @pl.when(step < n_slabs)
        def _(s_in=s_in, s_out=s_out):
            s_out[...] = s_in[...].astype(s_out.dtype)
    if zero_slabs:
        @pl.when(step < zero_slabs)
        def _():
            zero_ref[...] = jnp.zeros_like(zero_ref)


def _matmul(a, w, *, bias=None, use_gelu=False, residual=None, out_dtype, col_block_map=None, n_out=None,
            regroup=None, side=(), zero_fill=None, tm=1024, tn=512):
    m, k = a.shape
    n = w.shape[1] if n_out is None else n_out
    tm, tn = min(tm, m), min(tn, n)
    grid = (m // tm, n // tn)
    w_map = (lambda i, j: (0, j)) if col_block_map is None else (lambda i, j: (0, col_block_map(j)))
    in_specs = [pl.BlockSpec((tm, k), lambda i, j: (i, 0)),
                pl.BlockSpec((k, tn), w_map)]
    args = [a, w]
    if bias is not None:
        in_specs.append(pl.BlockSpec((1, tn), lambda i, j: (0, j)))
        args.append(bias.reshape(1, n))
    if residual is not None:
        in_specs.append(pl.BlockSpec((tm, tn), lambda i, j: (i, j)))
        args.append(residual)
    scratch = []
    dilation = 1
    if regroup is None:
        out_specs = [pl.BlockSpec((tm, tn), lambda i, j: (i, j))]
        out_shape = [jax.ShapeDtypeStruct((m, n), out_dtype)]
    else:
        batch, seq, dilation = regroup
        tiles = seq // tm
        out_specs = [pl.BlockSpec((1, dilation, tm // dilation, tn), lambda i, j: (i // tiles, 0, i % tiles, j))]
        out_shape = [jax.ShapeDtypeStruct((batch, dilation, seq // dilation, n), out_dtype)]
        if dilation > 1:
            scratch = [pltpu.VMEM((tn // LANES, tm // dilation * _regroup_pitch(dilation), LANES), jnp.float32)]
    n_steps = grid[0] * grid[1]

    def slab_count(rows):
        count = 1
        while count * 2 <= n_steps and rows % (count * 2) == 0 and (rows // (count * 2)) % 8 == 0:
            count *= 2
        return count

    side_slabs = []
    for stack, layer in side:
        cols = stack.shape[-1]
        flat = stack.reshape(-1, cols)
        layer_rows = flat.shape[0] // stack.shape[0]
        n_slabs = slab_count(layer_rows)
        slab_rows = layer_rows // n_slabs

        def slab(i, j, n_slabs=n_slabs):
            return jnp.minimum(i * grid[1] + j, n_slabs - 1)
        in_specs.append(pl.BlockSpec((slab_rows, cols),
                                     lambda i, j, slab=slab, first=layer * n_slabs: (first + slab(i, j), 0)))
        args.append(flat)
        out_specs.append(pl.BlockSpec((slab_rows, cols), lambda i, j, slab=slab: (slab(i, j), 0)))
        out_shape.append(jax.ShapeDtypeStruct((layer_rows, cols), a.dtype))
        side_slabs.append(n_slabs)
    zero_slabs = 0
    if zero_fill is not None:
        z_rows, z_cols, z_dtype = zero_fill
        zero_slabs = slab_count(z_rows)
        out_specs.append(pl.BlockSpec((z_rows // zero_slabs, z_cols),
                                      lambda i, j: (jnp.minimum(i * grid[1] + j, zero_slabs - 1), 0)))
        out_shape.append(jax.ShapeDtypeStruct((z_rows, z_cols), z_dtype))
    body = functools.partial(_matmul_body, has_bias=bias is not None, use_gelu=use_gelu,
                             has_residual=residual is not None, dilation=dilation, side_slabs=tuple(side_slabs),
                             zero_slabs=zero_slabs)
    riders = bool(side) or zero_fill is not None
    outs = pl.pallas_call(
        body,
        grid=grid,
        in_specs=in_specs,
        out_specs=out_specs,
        out_shape=out_shape,
        scratch_shapes=scratch,
        compiler_params=_params(("arbitrary", "arbitrary") if riders else ("parallel", "arbitrary")),
        name="matmul",
    )(*args)
    if not riders:
        return outs[0]
    cast = [o.reshape(stack.shape[1:]) for o, (stack, _) in zip(outs[1:], side)]
    return (outs[0], cast, outs[-1]) if zero_fill is not None else (outs[0], cast)


def _sgu_gate_body(u_ref, v_ref, gain_ref, vbias_ref, ws_ref, bs_ref, o_ref, *, chunks):
    v = v_ref[...].astype(jnp.float32)
    mu = jnp.mean(v, axis=-1, keepdims=True)
    vc = v - mu
    var = jnp.mean(vc * vc, axis=-1, keepdims=True)
    vn = (vc * lax.rsqrt(var + EPS) * gain_ref[...] + vbias_ref[...]).astype(jnp.bfloat16)
    n_groups = ws_ref.shape[0]
    t_idx = lax.broadcasted_iota(jnp.int32, (SGU_CHUNK, SGU_CHUNK), 0)
    s_idx = lax.broadcasted_iota(jnp.int32, (SGU_CHUNK, SGU_CHUNK), 1)
    causal = s_idx <= t_idx
    for g in range(n_groups):
        cols = slice(g * SGU_GROUP_DIM, (g + 1) * SGU_GROUP_DIM)
        w = jnp.where(causal, ws_ref[g], jnp.zeros((), ws_ref.dtype))
        for c in range(chunks):
            rows = slice(c * SGU_CHUNK, (c + 1) * SGU_CHUNK)
            vm = jnp.dot(w, vn[rows, cols], preferred_element_type=jnp.float32) + bs_ref[:, cols]
            o_ref[rows, cols] = (u_ref[rows, cols].astype(jnp.float32) * vm).astype(o_ref.dtype)


def _sgu_gate(z, v_gain, v_bias, w_s, b_s, chunks=2):
    n, two_w = z.shape
    width = two_w // 2
    rows = chunks * SGU_CHUNK
    n_groups = w_s.shape[0]
    b_full = jnp.repeat(b_s.T, SGU_GROUP_DIM, axis=1)
    body = functools.partial(_sgu_gate_body, chunks=chunks)
    return pl.pallas_call(
        body,
        grid=(n // rows,),
        in_specs=[pl.BlockSpec((rows, width), lambda i: (i, 0)),
                  pl.BlockSpec((rows, width), lambda i: (i, 1)),
                  pl.BlockSpec((1, width), lambda i: (0, 0)),
                  pl.BlockSpec((1, width), lambda i: (0, 0)),
                  pl.BlockSpec((n_groups, SGU_CHUNK, SGU_CHUNK), lambda i: (0, 0, 0)),
                  pl.BlockSpec((SGU_CHUNK, width), lambda i: (0, 0))],
        out_specs=pl.BlockSpec((rows, width), lambda i: (i, 0)),
        out_shape=jax.ShapeDtypeStruct((n, width), jnp.bfloat16),
        compiler_params=_params(("parallel",)),
        name="sgu_gate",
    )(z, z, v_gain.reshape(1, width), v_bias.reshape(1, width), w_s.astype(jnp.bfloat16), b_full)


def _t5_causal_bucket(dist):
    max_exact = NUM_BUCKETS // 2
    d = jnp.maximum(dist, 0)
    df = jnp.maximum(d, 1).astype(jnp.float32)
    large = max_exact + (jnp.log(df / max_exact) / math.log(MAX_DISTANCE / max_exact)
                         * (NUM_BUCKETS - max_exact)).astype(jnp.int32)
    large = jnp.minimum(large, NUM_BUCKETS - 1)
    return jnp.where(d < max_exact, d, large)


def _band_bias(bias_table, window, dilation):
    span = window // dilation
    qi = jnp.arange(DIL_BLOCK)[:, None]
    kj = jnp.arange(2 * DIL_BLOCK)[None, :]
    sub_dist = qi + DIL_BLOCK - kj
    band = (sub_dist >= 0) & (sub_dist <= span)
    bucket = _t5_causal_bucket(sub_dist * dilation)
    onehot = (bucket[:, :, None] == jnp.arange(NUM_BUCKETS)).astype(jnp.float32)
    bias = jnp.einsum('qkb,bh->hqk', onehot, bias_table.astype(jnp.float32), precision=lax.Precision.HIGHEST)
    return jnp.where(band[None], bias, MASK_VALUE)


def _attn_body(q_ref, kp_ref, kc_ref, vp_ref, vc_ref, bias_ref, o_ref, lse_ref):
    first = pl.program_id(2) == 0
    key_idx = lax.broadcasted_iota(jnp.int32, (DIL_BLOCK, 2 * DIL_BLOCK), 1)
    prev_invalid = jnp.logical_and(first, key_idx < DIL_BLOCK)
    lane = lax.broadcasted_iota(jnp.int32, (DIL_BLOCK, LANES), 1)
    lse_all = jnp.zeros((DIL_BLOCK, LANES), jnp.float32)
    scale = HEAD_DIM ** -0.5
    for h in range(DIL_HEADS):
        cols = slice(h * HEAD_DIM, (h + 1) * HEAD_DIM)
        q = q_ref[0, 0, :, cols]
        k = jnp.concatenate([kp_ref[0, 0, :, cols], kc_ref[0, 0, :, cols]], axis=0)
        v = jnp.concatenate([vp_ref[0, 0, :, cols], vc_ref[0, 0, :, cols]], axis=0)
        s = lax.dot_general(q, k, (((1,), (1,)), ((), ())), preferred_element_type=jnp.float32)
        s = s * scale + bias_ref[h]
        s = jnp.where(prev_invalid, MASK_VALUE, s)
        m = jnp.max(s, axis=-1, keepdims=True)
        p = jnp.exp(s - m)
        den = jnp.sum(p, axis=-1, keepdims=True)
        o = jnp.dot(p.astype(jnp.bfloat16), v, preferred_element_type=jnp.float32) / den
        o_ref[0, 0, :, cols] = o.astype(o_ref.dtype)
        lse_all = jnp.where(lane == h, m + jnp.log(den), lse_all)
    lse_ref[0, 0] = lse_all


def _dilated_group(qkv, band_bias):
    batch, dilation, sub_len, _ = qkv.shape
    nb = sub_len // DIL_BLOCK

    def spec(part, prev):
        def index_map(b, r, n):
            return (b, r, jnp.maximum(n - 1, 0) if prev else n, part)
        return pl.BlockSpec((1, 1, DIL_BLOCK, DIL_INNER), index_map)

    return pl.pallas_call(
        _attn_body,
        grid=(batch, dilation, nb),
        in_specs=[spec(0, False), spec(1, True), spec(1, False), spec(2, True), spec(2, False),
                  pl.BlockSpec((DIL_HEADS, DIL_BLOCK, 2 * DIL_BLOCK), lambda b, r, n: (0, 0, 0))],
        out_specs=[pl.BlockSpec((1, 1, DIL_BLOCK, DIL_INNER), lambda b, r, n: (b, r, n, 0)),
                   pl.BlockSpec((1, 1, DIL_BLOCK, LANES), lambda b, r, n: (b, r, n, 0))],
        out_shape=[jax.ShapeDtypeStruct((batch, dilation, sub_len, DIL_INNER), jnp.bfloat16),
                   jax.ShapeDtypeStruct((batch, dilation, sub_len, LANES), jnp.float32)],
        compiler_params=_params(("parallel", "parallel", "arbitrary")),
        name="dilated_attn",
    )(qkv, qkv, qkv, qkv, qkv, band_bias)


def _merge_body(*refs, dilations):
    n_g = len(dilations)
    o_refs, l_refs = refs[:n_g], refs[n_g:2 * n_g]
    out_ref, ltok, wtok, acc = refs[2 * n_g:]
    rows = out_ref.shape[0]

    def residue_rows(r, d):
        return pl.ds(r, rows // d, stride=d) if d > 1 else slice(None)

    for g, d in enumerate(dilations):
        for r in range(d):
            ltok[g, residue_rows(r, d), :] = l_refs[g][0, r]
    lse = [ltok[g] for g in range(n_g)]
    m = functools.reduce(jnp.maximum, lse)
    e = [jnp.exp(l - m) for l in lse]
    tot = functools.reduce(jnp.add, e)
    for g in range(n_g):
        wtok[g] = e[g] / tot
    for g, d in enumerate(dilations):
        for r in range(d):
            rr = residue_rows(r, d)
            w = wtok[g, rr, :]
            for h in range(DIL_HEADS):
                cols = slice(h * HEAD_DIM, (h + 1) * HEAD_DIM)
                term = o_refs[g][0, r, :, cols].astype(jnp.float32) * w[:, h:h + 1]
                if g == 0:
                    acc[h, rr, :] = term
                else:
                    acc[h, rr, :] = acc[h, rr, :] + term
    for h in range(DIL_HEADS):
        out_ref[:, h * HEAD_DIM:(h + 1) * HEAD_DIM] = acc[h].astype(out_ref.dtype)


def _merge_groups(outs, lses, rows=512):
    batch = outs[0].shape[0]
    dilations = tuple(o.shape[1] for o in outs)
    seq = outs[0].shape[1] * outs[0].shape[2]
    tiles = seq // rows

    def spec(d, width):
        return pl.BlockSpec((1, d, rows // d, width), lambda i: (i // tiles, 0, i % tiles, 0))

    n_g = len(outs)
    return pl.pallas_call(
        functools.partial(_merge_body, dilations=dilations),
        grid=(batch * tiles,),
        in_specs=[spec(d, DIL_INNER) for d in dilations] + [spec(d, LANES) for d in dilations],
        out_specs=pl.BlockSpec((rows, DIL_INNER), lambda i: (i, 0)),
        out_shape=jax.ShapeDtypeStruct((batch * seq, DIL_INNER), jnp.bfloat16),
        scratch_shapes=[pltpu.VMEM((n_g, rows, LANES), jnp.float32),
                        pltpu.VMEM((n_g, rows, LANES), jnp.float32),
                        pltpu.VMEM((DIL_HEADS, rows, HEAD_DIM), jnp.float32)],
        compiler_params=_params(("parallel",)),
        name="merge_groups",
    )(*outs, *lses)


def _pack_halves(v):
    c = v.shape[1] // 2
    r = v.astype(jnp.bfloat16).astype(jnp.float32)
    lo = lax.bitcast_convert_type(r[:, :c], jnp.uint32)
    hi = lax.bitcast_convert_type(r[:, c:], jnp.uint32)
    return (lo >> 16) | (hi & jnp.uint32(0xFFFF0000))


def _unpack_halves(w):
    lo = lax.bitcast_convert_type(w << 16, jnp.float32)
    hi = lax.bitcast_convert_type(w & jnp.uint32(0xFFFF0000), jnp.float32)
    return jnp.concatenate([lo, hi], axis=1)


def _router_body(x_ref, g_ref, w_ref, b_ref, o_ref, hp_ref, cnt_ref, carry):
    i = pl.program_id(0)

    @pl.when(i == 0)
    def _():
        carry[...] = jnp.zeros_like(carry)

    x = x_ref[...]
    ms = jnp.mean(x * x, axis=-1, keepdims=True)
    hf = x * lax.rsqrt(ms + EPS) * g_ref[...]
    hp_ref[...] = _pack_halves(hf)
    logits = jnp.dot(hf.astype(jnp.bfloat16), w_ref[...], preferred_element_type=jnp.float32) + b_ref[...]
    rows = logits.shape[0]
    lane = lax.broadcasted_iota(jnp.int32, logits.shape, 1)
    neg = -jnp.inf
    gl = jnp.where(lane < N_EXPERT_GROUPS, logits, neg)
    gmax = jnp.max(gl, axis=-1, keepdims=True)
    g_idx = jnp.min(jnp.where(gl == gmax, lane, LANES), axis=-1, keepdims=True)
    g_w = 1.0 / jnp.sum(jnp.exp(gl - gmax), axis=-1, keepdims=True)
    lo = N_EXPERT_GROUPS + EXPERTS_PER_GROUP * g_idx
    el = jnp.where(jnp.logical_and(lane >= lo, lane < lo + EXPERTS_PER_GROUP), logits, neg)
    v1 = jnp.max(el, axis=-1, keepdims=True)
    i1 = jnp.min(jnp.where(el == v1, lane, LANES), axis=-1, keepdims=True)
    el2 = jnp.where(lane == i1, neg, el)
    v2 = jnp.max(el2, axis=-1, keepdims=True)
    i2 = jnp.min(jnp.where(el2 == v2, lane, LANES), axis=-1, keepdims=True)
    e2 = jnp.exp(v2 - v1)
    p1 = 1.0 / (1.0 + e2)
    p2 = e2 / (1.0 + e2)
    oh1 = lane == i1 - N_EXPERT_GROUPS
    oh2 = lane == i2 - N_EXPERT_GROUPS
    oh1f, oh2f = oh1.astype(jnp.float32), oh2.astype(jnp.float32)
    before = (lax.broadcasted_iota(jnp.int32, (rows, rows), 1)
              < lax.broadcasted_iota(jnp.int32, (rows, rows), 0)).astype(jnp.bfloat16)
    pre1 = jnp.dot(before, oh1f.astype(jnp.bfloat16), preferred_element_type=jnp.float32)
    pre2 = jnp.dot(before, oh2f.astype(jnp.bfloat16), preferred_element_type=jnp.float32)
    cnt1 = jnp.sum(oh1f, axis=0, keepdims=True)
    cnt2 = jnp.sum(oh2f, axis=0, keepdims=True)
    base = carry[...]
    rank1 = jnp.sum(jnp.where(oh1, pre1 + base, 0.0), axis=-1, keepdims=True)
    rank2 = jnp.sum(jnp.where(oh2, pre2 + (base + cnt1), 0.0), axis=-1, keepdims=True)
    total = base + cnt1 + cnt2
    carry[...] = total
    cnt_ref[...] = total
    out = jnp.where(lane == 0, (i1 - N_EXPERT_GROUPS).astype(jnp.float32),
          jnp.where(lane == 1, (i2 - N_EXPERT_GROUPS).astype(jnp.float32),
          jnp.where(lane == 2, g_w * p1,
          jnp.where(lane == 3, g_w * p2,
          jnp.where(lane == 4, rank1,
          jnp.where(lane == 5, rank2, 0.0))))))
    o_ref[...] = out


def _router(x, g, w_rg, b_rg, w_re, b_re, rows=512):
    n, d = x.shape
    pad = LANES - N_EXPERT_GROUPS - N_EXPERTS
    w = jnp.concatenate([w_rg, w_re, jnp.zeros((d, pad), w_rg.dtype)], axis=1).astype(jnp.bfloat16)
    b = jnp.concatenate([b_rg, b_re, jnp.zeros((pad,), b_rg.dtype)]).reshape(1, LANES)
    return pl.pallas_call(
        _router_body,
        grid=(n // rows,),
        in_specs=[pl.BlockSpec((rows, d), lambda i: (i, 0)),
                  pl.BlockSpec((1, d), lambda i: (0, 0)),
                  pl.BlockSpec((d, LANES), lambda i: (0, 0)),
                  pl.BlockSpec((1, LANES), lambda i: (0, 0))],
        out_specs=[pl.BlockSpec((rows, LANES), lambda i: (i, 0)),
                   pl.BlockSpec((rows, d // 2), lambda i: (i, 0)),
                   pl.BlockSpec((1, LANES), lambda i: (0, 0))],
        out_shape=[jax.ShapeDtypeStruct((n, LANES), jnp.float32),
                   jax.ShapeDtypeStruct((n, d // 2), jnp.uint32),
                   jax.ShapeDtypeStruct((1, LANES), jnp.float32)],
        scratch_shapes=[pltpu.VMEM((1, LANES), jnp.float32)],
        compiler_params=_params(("arbitrary",)),
        name="router",
    )(x, g.reshape(1, d), w, b)


def _dispatch_body(s1_ref, s2_ref, hp_ref, xb_in_hbm, xb_hbm, stage, sem):
    del xb_in_hbm
    i = pl.program_id(0)
    n_steps = pl.num_programs(0)
    rows = s1_ref.shape[2]
    slot = i % 2

    def wait_slot(of_slot):
        for _ in range(TOP_K):
            pltpu.make_async_copy(stage.at[of_slot], stage.at[of_slot], sem.at[of_slot]).wait()

    @pl.when(i >= 2)
    def _():
        wait_slot(slot)

    stage[slot] = hp_ref[...]

    def issue(r, carry):
        src = stage.at[slot, pl.ds(r, 1)]
        pltpu.make_async_copy(src, xb_hbm.at[pl.ds(s1_ref[0, 0, r], 1)], sem.at[slot]).start()
        pltpu.make_async_copy(src, xb_hbm.at[pl.ds(s2_ref[0, 0, r], 1)], sem.at[slot]).start()
        return carry
    lax.fori_loop(0, rows, issue, 0, unroll=8)

    @pl.when(i == n_steps - 1)
    def _():
        @pl.when(i >= 1)
        def _():
            wait_slot(1 - slot)
        wait_slot(slot)


def _dispatch(hp, slot1, slot2, xb_zero, rows=512):
    n, c = hp.shape
    p_total = xb_zero.shape[0]
    steps = n // rows
    tab = pl.BlockSpec((1, 1, rows), lambda i: (i, 0, 0), memory_space=pltpu.SMEM)
    any_spec = pl.BlockSpec(memory_space=pl.ANY)
    return pl.pallas_call(
        _dispatch_body,
        grid=(steps,),
        in_specs=[tab, tab, pl.BlockSpec((rows, c), lambda i: (i, 0)), any_spec],
        out_specs=any_spec,
        out_shape=jax.ShapeDtypeStruct((p_total, c), hp.dtype),
        scratch_shapes=[pltpu.VMEM((2, rows, c), hp.dtype),
                        pltpu.SemaphoreType.DMA((2,))],
        input_output_aliases={3: 0},
        compiler_params=_params(("arbitrary",)),
        name="moe_dispatch",
    )(slot1.reshape(steps, 1, rows), slot2.reshape(steps, 1, rows), hp, xb_zero)


def _expert_body(blk_e_ref, nvalid_ref, xb_ref, wg_ref, wu_ref, wd_ref, yb_ref):
    del blk_e_ref
    i = pl.program_id(0)

    @pl.when(nvalid_ref[i] > 0)
    def _():
        h = _unpack_halves(xb_ref[...]).astype(jnp.bfloat16)
        a = jax.nn.silu(jnp.dot(h, wg_ref[0], preferred_element_type=jnp.float32))
        a = a * jnp.dot(h, wu_ref[0], preferred_element_type=jnp.float32)
        y = jnp.dot(a.astype(jnp.bfloat16), wd_ref[0], preferred_element_type=jnp.float32)
        yb_ref[...] = _pack_halves(y)

    @pl.when(nvalid_ref[i] == 0)
    def _():
        yb_ref[...] = jnp.zeros_like(yb_ref)


def _moe_experts(xb, blk_e, nvalid, w_gate, w_up, w_down):
    p_total, c = xb.shape
    d = 2 * c
    d_e = w_gate.shape[-1]
    grid_spec = pltpu.PrefetchScalarGridSpec(
        num_scalar_prefetch=2,
        grid=(p_total // MOE_ROWS,),
        in_specs=[pl.BlockSpec((MOE_ROWS, c), lambda i, e, nv: (i, 0)),
                  pl.BlockSpec((1, d, d_e), lambda i, e, nv: (e[i], 0, 0)),
                  pl.BlockSpec((1, d, d_e), lambda i, e, nv: (e[i], 0, 0)),
                  pl.BlockSpec((1, d_e, d), lambda i, e, nv: (e[i], 0, 0))],
        out_specs=pl.BlockSpec((MOE_ROWS, c), lambda i, e, nv: (i, 0)),
    )
    return pl.pallas_call(
        _expert_body,
        grid_spec=grid_spec,
        out_shape=jax.ShapeDtypeStruct((p_total, c), xb.dtype),
        compiler_params=_params(("arbitrary",)),
        name="moe_experts",
    )(blk_e, nvalid, xb, w_gate, w_up, w_down)


def _combine_body(s1_ref, s2_ref, s1n_ref, s2n_ref, x_ref, routed_ref, g_ref, yb_hbm, *refs, emit_sum):
    out_refs, ybuf, sem = refs[:-2], refs[-2], refs[-1]
    i = pl.program_id(0)
    n_steps = pl.num_programs(0)
    slot = i % 2
    rows = x_ref.shape[0]

    def gather(t1_ref, t2_ref, to_slot):
        def issue(r, carry):
            pltpu.make_async_copy(yb_hbm.at[pl.ds(t1_ref[0, 0, r], 1)], ybuf.at[to_slot, pl.ds(r, 1)],
                                  sem.at[to_slot]).start()
            pltpu.make_async_copy(yb_hbm.at[pl.ds(t2_ref[0, 0, r], 1)], ybuf.at[to_slot, pl.ds(rows + r, 1)],
                                  sem.at[to_slot]).start()
            return carry
        lax.fori_loop(0, rows, issue, 0, unroll=8)

    @pl.when(i == 0)
    def _():
        gather(s1_ref, s2_ref, 0)

    @pl.when(i + 1 < n_steps)
    def _():
        gather(s1n_ref, s2n_ref, 1 - slot)

    pltpu.make_async_copy(ybuf.at[slot], ybuf.at[slot], sem.at[slot]).wait()
    y1 = _unpack_halves(ybuf[slot, :rows])
    y2 = _unpack_halves(ybuf[slot, rows:])
    s = x_ref[...] + (y1 * routed_ref[:, 2:3] + y2 * routed_ref[:, 3:4])
    if emit_sum:
        out_refs[0][...] = s
    ms = jnp.mean(s * s, axis=-1, keepdims=True)
    out_refs[-1][...] = (s * lax.rsqrt(ms + EPS) * g_ref[...]).astype(out_refs[-1].dtype)


def _combine_norm(x, yb, routed, slot1, slot2, g, norm_dtype, emit_sum, rows=256):
    n, d = x.shape
    steps = n // rows
    row_spec = pl.BlockSpec((rows, d), lambda i: (i, 0))
    tab = pl.BlockSpec((1, 1, rows), lambda i: (i, 0, 0), memory_space=pltpu.SMEM)
    tab_next = pl.BlockSpec((1, 1, rows), lambda i: (jnp.minimum(i + 1, steps - 1), 0, 0),
                            memory_space=pltpu.SMEM)
    out_specs = [row_spec]
    out_shape = [jax.ShapeDtypeStruct((n, d), norm_dtype)]
    if emit_sum:
        out_specs = [row_spec] + out_specs
        out_shape = [jax.ShapeDtypeStruct((n, d), x.dtype)] + out_shape
    s1 = slot1.reshape(steps, 1, rows)
    s2 = slot2.reshape(steps, 1, rows)
    return pl.pallas_call(
        functools.partial(_combine_body, emit_sum=emit_sum),
        grid=(steps,),
        in_specs=[tab, tab, tab_next, tab_next, row_spec,
                  pl.BlockSpec((rows, LANES), lambda i: (i, 0)),
                  pl.BlockSpec((1, d), lambda i: (0, 0)),
                  pl.BlockSpec(memory_space=pl.ANY)],
        out_specs=out_specs,
        out_shape=out_shape,
        scratch_shapes=[pltpu.VMEM((2, TOP_K * rows, d // 2), yb.dtype),
                        pltpu.SemaphoreType.DMA((2,))],
        compiler_params=_params(("arbitrary",)),
        name="combine_norm",
    )(s1, s2, s1, s2, x, routed, g.reshape(1, d), yb)


def _moe_slot_rows(n):
    return n * TOP_K + N_EXPERTS * MOE_ROWS


def _moe_layer(x, ffn_g, next_g, norm_dtype, emit_sum, w_rg, b_rg, w_re, b_re, w_gate, w_up, w_down, xb_zero):
    n, d = x.shape
    routed, hp, counts = _router(x, ffn_g, w_rg, b_rg, w_re, b_re)

    n_blk = _moe_slot_rows(n) // MOE_ROWS
    experts = jnp.arange(N_EXPERTS, dtype=jnp.int32)
    counts = counts[0, :N_EXPERTS].astype(jnp.int32)
    padded = ((counts + MOE_ROWS - 1) // MOE_ROWS) * MOE_ROWS
    pends = jnp.cumsum(padded)
    pstarts = pends - padded
    expert_id = routed[:, 0:TOP_K].astype(jnp.int32)
    rank = routed[:, 2 * TOP_K:3 * TOP_K].astype(jnp.int32)
    slot = jnp.sum(jnp.where(expert_id[:, :, None] == experts, pstarts, 0), axis=-1) + rank
    blk_start = jnp.arange(n_blk, dtype=jnp.int32) * MOE_ROWS
    blk_e = jnp.minimum(jnp.sum(blk_start[:, None] >= pends[None, :], axis=1), N_EXPERTS - 1).astype(jnp.int32)
    seg_end = jnp.sum(jnp.where(blk_e[:, None] == experts, pstarts + counts, 0), axis=1)
    nvalid = jnp.clip(seg_end - blk_start, 0, MOE_ROWS).astype(jnp.int32)

    xb = _dispatch(hp, slot[:, 0], slot[:, 1], xb_zero)
    yb = _moe_experts(xb, blk_e, nvalid, w_gate, w_up, w_down)
    return _combine_norm(x, yb, routed, slot[:, 0], slot[:, 1], next_g, norm_dtype, emit_sum)


def kernel(x, mix_norm, ffn_norm, sgu_w_in, sgu_b_in, sgu_v_gain, sgu_v_bias, sgu_w_spatial, sgu_b_spatial,
           sgu_w_out, dil_w_qkv, dil_w_out, rel_bias, router_w_group, router_b_group, router_w_expert,
           router_b_expert, moe_w_gate, moe_w_up, moe_w_down, final_norm):
    batch, seq, d = x.shape
    n = batch * seq
    bf16 = jnp.bfloat16
    xf = x.reshape(n, d)
    depth = mix_norm.shape[0]
    n_dil = len(DIL_CONFIGS)
    h = _rmsnorm(xf, mix_norm[0], bf16)
    out = None
    for i in range(depth):
        j = i // 2
        xb_shape = (_moe_slot_rows(n), d // 2, jnp.uint32)
        if i % 2 == 0:
            z, (w_gate, w_up), xb_zero = _matmul(h, sgu_w_in[j], bias=sgu_b_in[j], use_gelu=True, out_dtype=bf16,
                                                 side=((moe_w_gate, i), (moe_w_up, i)), zero_fill=xb_shape)
            y = _sgu_gate(z, sgu_v_gain[j], sgu_v_bias[j], sgu_w_spatial[j], sgu_b_spatial[j])
            xf, (w_down,) = _matmul(y, sgu_w_out[j], residual=xf, out_dtype=jnp.float32, side=((moe_w_down, i),))
        else:
            outs, lses, cast = [], [], []
            tn = 512
            per_part = DIL_INNER // tn
            for g, (window, dilation) in enumerate(DIL_CONFIGS):
                res = _matmul(
                    h, dil_w_qkv[j], out_dtype=bf16, n_out=3 * DIL_INNER, tn=tn,
                    col_block_map=lambda c, g=g: ((c // per_part) * n_dil + g) * per_part + c % per_part,
                    regroup=(batch, seq, dilation), side=(((moe_w_gate, moe_w_up, moe_w_down)[g], i),),
                    zero_fill=xb_shape if g == 0 else None)
                qkv, (w_cast,) = res[:2]
                if g == 0:
                    xb_zero = res[2]
                cast.append(w_cast)
                bias = _band_bias(rel_bias[:, g * DIL_HEADS:(g + 1) * DIL_HEADS], window, dilation)
                o, lse = _dilated_group(qkv, bias)
                outs.append(o)
                lses.append(lse)
            w_gate, w_up, w_down = cast
            merged = _merge_groups(outs, lses)
            xf = _matmul(merged, dil_w_out[j], residual=xf, out_dtype=jnp.float32)
        last = i + 1 == depth
        res = _moe_layer(xf, ffn_norm[i], final_norm if last else mix_norm[i + 1],
                         x.dtype if last else bf16, not last,
                         router_w_group[i], router_b_group[i], router_w_expert[i], router_b_expert[i],
                         w_gate, w_up, w_down, xb_zero)
        if last:
            (out,) = res
        else:
            xf, h = res
    return out.reshape(batch, seq, d)
```

```python
import functools
import math

import jax
import jax.numpy as jnp
from jax import lax
from jax.experimental import pallas as pl
from jax.experimental.pallas import tpu as pltpu

EPS = 1e-6
SGU_CHUNK = 128
SGU_GROUP_DIM = 128
DIL_CONFIGS = ((128, 1), (512, 4), (2048, 16))
DIL_HEADS = 16
HEAD_DIM = 128
DIL_BLOCK = 128
DIL_INNER = DIL_HEADS * HEAD_DIM
NUM_BUCKETS = 32
MAX_DISTANCE = 2048
N_EXPERT_GROUPS = 4
EXPERTS_PER_GROUP = 8
N_EXPERTS = N_EXPERT_GROUPS * EXPERTS_PER_GROUP
TOP_K = 2
MASK_VALUE = -1e30

LANES = 128
MOE_ROWS = 256
VMEM_LIMIT = 52 * 1024 * 1024


def _params(sem, vmem=VMEM_LIMIT):
    return pltpu.CompilerParams(dimension_semantics=sem, vmem_limit_bytes=vmem)


def _rmsnorm_body(x_ref, g_ref, o_ref):
    x = x_ref[...]
    ms = jnp.mean(x * x, axis=-1, keepdims=True)
    o_ref[...] = (x * lax.rsqrt(ms + EPS) * g_ref[...]).astype(o_ref.dtype)


def _rmsnorm(x, g, out_dtype, rows=512):
    n, d = x.shape
    return pl.pallas_call(
        _rmsnorm_body,
        grid=(n // rows,),
        in_specs=[pl.BlockSpec((rows, d), lambda i: (i, 0)),
                  pl.BlockSpec((1, d), lambda i: (0, 0))],
        out_specs=pl.BlockSpec((rows, d), lambda i: (i, 0)),
        out_shape=jax.ShapeDtypeStruct((n, d), out_dtype),
        compiler_params=_params(("parallel",)),
        name="rmsnorm",
    )(x, g.reshape(1, d))


def _regroup_pitch(dilation):
    return dilation + 4 if dilation % 8 == 0 else dilation


def _matmul_body(*refs, has_bias, use_gelu, has_residual, dilation, n_side, has_zero, col_scale):
    a_ref, w_ref = refs[0], refs[1]
    pos = 2
    acc = jnp.dot(a_ref[...], w_ref[...].astype(a_ref.dtype), preferred_element_type=jnp.float32)
    if col_scale is not None:
        blocks, scale = col_scale
        acc = acc * jnp.where(pl.program_id(1) < blocks, jnp.float32(scale), jnp.float32(1.0))
    if has_bias:
        acc = acc + refs[pos][...]
        pos += 1
    if use_gelu:
        acc = jax.nn.gelu(acc)
    if has_residual:
        acc = refs[pos][...] + acc
        pos += 1
    side_in = refs[pos:pos + n_side]
    o_ref = refs[pos + n_side]
    side_out = refs[pos + n_side + 1:pos + 2 * n_side + 1]
    pos += 2 * n_side + 1
    zero_ref = None
    if has_zero:
        zero_ref = refs[pos]
        pos += 1
    if dilation == 1:
        o_ref[...] = acc.astype(o_ref.dtype).reshape(o_ref.shape)
    else:
        acc_ref = refs[pos]
        sub = acc.shape[0] // dilation
        pitch = _regroup_pitch(dilation)
        for panel in range(acc_ref.shape[0]):
            cols = slice(panel * LANES, (panel + 1) * LANES)
            if pitch == dilation:
                acc_ref[panel] = acc[:, cols]
            else:
                for l in range(sub):
                    acc_ref[panel, l * pitch:l * pitch + dilation, :] = acc[l * dilation:(l + 1) * dilation, cols]
            for r in range(dilation):
                o_ref[0, r, :, cols] = acc_ref[panel, pl.ds(r, sub, stride=pitch), :].astype(o_ref.dtype)
    for s_in, s_out in zip(side_in, side_out):
        s_out[...] = s_in[...].astype(s_out.dtype)
    if has_zero:
        zero_ref[...] = jnp.zeros_like(zero_ref)


def _matmul(a, w, *, bias=None, use_gelu=False, residual=None, out_dtype, col_block_map=None, n_out=None,
            regroup=None, side=(), zero_fill=None, col_scale=None, tm=1024, tn=512):
    m, k = a.shape
    n = w.shape[1] if n_out is None else n_out
    tm, tn = min(tm, m), min(tn, n)
    grid = (m // tm, n // tn)
    w_map = (lambda i, j: (0, j)) if col_block_map is None else (lambda i, j: (0, col_block_map(j)))
    in_specs = [pl.BlockSpec((tm, k), lambda i, j: (i, 0)),
                pl.BlockSpec((k, tn), w_map)]
    args = [a, w]
    if bias is not None:
        in_specs.append(pl.BlockSpec((1, tn), lambda i, j: (0, j)))
        args.append(bias.reshape(1, n))
    if residual is not None:
        in_specs.append(pl.BlockSpec((tm, tn), lambda i, j: (i, j)))
        args.append(residual)
    scratch = []
    dilation = 1
    if regroup is None:
        out_specs = [pl.BlockSpec((tm, tn), lambda i, j: (i, j))]
        out_shape = [jax.ShapeDtypeStruct((m, n), out_dtype)]
    else:
        batch, seq, dilation = regroup
        tiles = seq // tm
        out_specs = [pl.BlockSpec((1, dilation, tm // dilation, tn), lambda i, j: (i // tiles, 0, i % tiles, j))]
        out_shape = [jax.ShapeDtypeStruct((batch, dilation, seq // dilation, n), out_dtype)]
        if dilation > 1:
            scratch = [pltpu.VMEM((tn // LANES, tm // dilation * _regroup_pitch(dilation), LANES), jnp.float32)]
    n_steps = grid[0] * grid[1]

    def step_of(i, j):
        return i * grid[1] + j
    s_in, s_args, s_out, s_shape = _side_cast_specs(side, n_steps, step_of, a.dtype)
    in_specs += s_in
    args += s_args
    out_specs += s_out
    out_shape += s_shape
    if zero_fill is not None:
        z_rows, z_cols, z_dtype = zero_fill
        zero_slabs = 1
        while zero_slabs * 2 <= n_steps and z_rows % (zero_slabs * 2) == 0 and (z_rows // (zero_slabs * 2)) % 8 == 0:
            zero_slabs *= 2
        out_specs.append(pl.BlockSpec((z_rows // zero_slabs, z_cols),
                                      lambda i, j: (jnp.minimum(step_of(i, j), zero_slabs - 1), 0)))
        out_shape.append(jax.ShapeDtypeStruct((z_rows, z_cols), z_dtype))
    body = functools.partial(_matmul_body, has_bias=bias is not None, use_gelu=use_gelu,
                             has_residual=residual is not None, dilation=dilation, n_side=len(side),
                             has_zero=zero_fill is not None, col_scale=col_scale)
    riders = bool(side) or zero_fill is not None
    outs = pl.pallas_call(
        body,
        grid=grid,
        in_specs=in_specs,
        out_specs=out_specs,
        out_shape=out_shape,
        scratch_shapes=scratch,
        compiler_params=_params(("arbitrary", "arbitrary") if riders else ("parallel", "arbitrary")),
        name="matmul",
    )(*args)
    if not riders:
        return outs[0]
    cast = [o.reshape(stack.shape[1:]) for o, (stack, _) in zip(outs[1:], side)]
    return (outs[0], cast, outs[-1]) if zero_fill is not None else (outs[0], cast)


def _sgu_gate_body(u_ref, v_ref, gain_ref, vbias_ref, ws_ref, bs_ref, o_ref, *, chunks):
    v = v_ref[...].astype(jnp.float32)
    mu = jnp.mean(v, axis=-1, keepdims=True)
    vc = v - mu
    var = jnp.mean(vc * vc, axis=-1, keepdims=True)
    vn = (vc * lax.rsqrt(var + EPS) * gain_ref[...] + vbias_ref[...]).astype(jnp.bfloat16)
    n_groups = ws_ref.shape[0]
    t_idx = lax.broadcasted_iota(jnp.int32, (SGU_CHUNK, SGU_CHUNK), 0)
    s_idx = lax.broadcasted_iota(jnp.int32, (SGU_CHUNK, SGU_CHUNK), 1)
    causal = s_idx <= t_idx
    for g in range(n_groups):
        cols = slice(g * SGU_GROUP_DIM, (g + 1) * SGU_GROUP_DIM)
        w = jnp.where(causal, ws_ref[g], jnp.zeros((), ws_ref.dtype))
        for c in range(chunks):
            rows = slice(c * SGU_CHUNK, (c + 1) * SGU_CHUNK)
            vm = jnp.dot(w, vn[rows, cols], preferred_element_type=jnp.float32) + bs_ref[:, cols]
            o_ref[rows, cols] = (u_ref[rows, cols].astype(jnp.float32) * vm).astype(o_ref.dtype)


def _sgu_gate(z, v_gain, v_bias, w_s, b_s, chunks=2):
    n, two_w = z.shape
    width = two_w // 2
    rows = chunks * SGU_CHUNK
    n_groups = w_s.shape[0]
    b_full = jnp.repeat(b_s.T, SGU_GROUP_DIM, axis=1)
    body = functools.partial(_sgu_gate_body, chunks=chunks)
    return pl.pallas_call(
        body,
        grid=(n // rows,),
        in_specs=[pl.BlockSpec((rows, width), lambda i: (i, 0)),
                  pl.BlockSpec((rows, width), lambda i: (i, 1)),
                  pl.BlockSpec((1, width), lambda i: (0, 0)),
                  pl.BlockSpec((1, width), lambda i: (0, 0)),
                  pl.BlockSpec((n_groups, SGU_CHUNK, SGU_CHUNK), lambda i: (0, 0, 0)),
                  pl.BlockSpec((SGU_CHUNK, width), lambda i: (0, 0))],
        out_specs=pl.BlockSpec((rows, width), lambda i: (i, 0)),
        out_shape=jax.ShapeDtypeStruct((n, width), jnp.bfloat16),
        compiler_params=_params(("parallel",)),
        name="sgu_gate",
    )(z, z, v_gain.reshape(1, width), v_bias.reshape(1, width), w_s.astype(jnp.bfloat16), b_full)


def _t5_causal_bucket(dist):
    max_exact = NUM_BUCKETS // 2
    d = jnp.maximum(dist, 0)
    df = jnp.maximum(d, 1).astype(jnp.float32)
    large = max_exact + (jnp.log(df / max_exact) / math.log(MAX_DISTANCE / max_exact)
                         * (NUM_BUCKETS - max_exact)).astype(jnp.int32)
    large = jnp.minimum(large, NUM_BUCKETS - 1)
    return jnp.where(d < max_exact, d, large)


def _band_bias(bias_table, window, dilation):
    span = window // dilation
    qi = jnp.arange(DIL_BLOCK)[:, None]
    kj = jnp.arange(2 * DIL_BLOCK)[None, :]
    sub_dist = qi + DIL_BLOCK - kj
    band = (sub_dist >= 0) & (sub_dist <= span)
    bucket = _t5_causal_bucket(sub_dist * dilation)
    onehot = (bucket[:, :, None] == jnp.arange(NUM_BUCKETS)).astype(jnp.float32)
    bias = jnp.einsum('qkb,bh->hqk', onehot, bias_table.astype(jnp.float32), precision=lax.Precision.HIGHEST)
    bias = jnp.where(band[None], bias, MASK_VALUE)
    return jnp.stack([bias, jnp.where(kj[None] < DIL_BLOCK, MASK_VALUE, bias)])


def _attn_body(q_ref, kp_ref, kc_ref, vp_ref, vc_ref, bias_ref, o_ref, lse_ref):
    table = jnp.where(pl.program_id(2) == 0, 1, 0)
    lane = lax.broadcasted_iota(jnp.int32, (DIL_BLOCK, LANES), 1)
    lse_all = jnp.zeros((DIL_BLOCK, LANES), jnp.float32)
    for h in range(DIL_HEADS):
        cols = slice(h * HEAD_DIM, (h + 1) * HEAD_DIM)
        q = q_ref[0, 0, :, cols]
        k = jnp.concatenate([kp_ref[0, 0, :, cols], kc_ref[0, 0, :, cols]], axis=0)
        v = jnp.concatenate([vp_ref[0, 0, :, cols], vc_ref[0, 0, :, cols]], axis=0)
        s = lax.dot_general(q, k, (((1,), (1,)), ((), ())), preferred_element_type=jnp.float32)
        s = s + bias_ref[table, h]
        m = jnp.max(s, axis=-1, keepdims=True)
        p = jnp.exp(s - m)
        den = jnp.sum(p, axis=-1, keepdims=True)
        o = jnp.dot(p.astype(jnp.bfloat16), v, preferred_element_type=jnp.float32) / den
        o_ref[0, 0, :, cols] = o.astype(o_ref.dtype)
        lse_all = jnp.where(lane == h, m + jnp.log(den), lse_all)
    lse_ref[0, 0] = lse_all


def _dilated_group(qkv, band_bias):
    batch, dilation, sub_len, _ = qkv.shape
    nb = sub_len // DIL_BLOCK

    def spec(part, prev):
        def index_map(b, r, n):
            return (b, r, jnp.maximum(n - 1, 0) if prev else n, part)
        return pl.BlockSpec((1, 1, DIL_BLOCK, DIL_INNER), index_map)

    return pl.pallas_call(
        _attn_body,
        grid=(batch, dilation, nb),
        in_specs=[spec(0, False), spec(1, True), spec(1, False), spec(2, True), spec(2, False),
                  pl.BlockSpec((2, DIL_HEADS, DIL_BLOCK, 2 * DIL_BLOCK), lambda b, r, n: (0, 0, 0, 0))],
        out_specs=[pl.BlockSpec((1, 1, DIL_BLOCK, DIL_INNER), lambda b, r, n: (b, r, n, 0)),
                   pl.BlockSpec((1, 1, DIL_BLOCK, LANES), lambda b, r, n: (b, r, n, 0))],
        out_shape=[jax.ShapeDtypeStruct((batch, dilation, sub_len, DIL_INNER), jnp.bfloat16),
                   jax.ShapeDtypeStruct((batch, dilation, sub_len, LANES), jnp.float32)],
        compiler_params=_params(("parallel", "parallel", "arbitrary")),
        name="dilated_attn",
    )(qkv, qkv, qkv, qkv, qkv, band_bias)


def _merge_body(*refs, dilations):
    n_g = len(dilations)
    o_refs, l_refs = refs[:n_g], refs[n_g:2 * n_g]
    out_ref, ltok, wtok, acc = refs[2 * n_g:]
    rows = out_ref.shape[0]

    def residue_rows(r, d):
        return pl.ds(r, rows // d, stride=d) if d > 1 else slice(None)

    for g, d in enumerate(dilations):
        for r in range(d):
            ltok[g, residue_rows(r, d), :] = l_refs[g][0, r]
    lse = [ltok[g] for g in range(n_g)]
    m = functools.reduce(jnp.maximum, lse)
    e = [jnp.exp(l - m) for l in lse]
    tot = functools.reduce(jnp.add, e)
    for g in range(n_g):
        wtok[g] = e[g] / tot
    for g, d in enumerate(dilations):
        for r in range(d):
            rr = residue_rows(r, d)
            w = wtok[g, rr, :]
            for h in range(DIL_HEADS):
                cols = slice(h * HEAD_DIM, (h + 1) * HEAD_DIM)
                term = o_refs[g][0, r, :, cols].astype(jnp.float32) * w[:, h:h + 1]
                if g == 0:
                    acc[h, rr, :] = term
                else:
                    acc[h, rr, :] = acc[h, rr, :] + term
    for h in range(DIL_HEADS):
        out_ref[:, h * HEAD_DIM:(h + 1) * HEAD_DIM] = acc[h].astype(out_ref.dtype)


def _merge_groups(outs, lses, rows=512):
    batch = outs[0].shape[0]
    dilations = tuple(o.shape[1] for o in outs)
    seq = outs[0].shape[1] * outs[0].shape[2]
    tiles = seq // rows

    def spec(d, width):
        return pl.BlockSpec((1, d, rows // d, width), lambda i: (i // tiles, 0, i % tiles, 0))

    n_g = len(outs)
    return pl.pallas_call(
        functools.partial(_merge_body, dilations=dilations),
        grid=(batch * tiles,),
        in_specs=[spec(d, DIL_INNER) for d in dilations] + [spec(d, LANES) for d in dilations],
        out_specs=pl.BlockSpec((rows, DIL_INNER), lambda i: (i, 0)),
        out_shape=jax.ShapeDtypeStruct((batch * seq, DIL_INNER), jnp.bfloat16),
        scratch_shapes=[pltpu.VMEM((n_g, rows, LANES), jnp.float32),
                        pltpu.VMEM((n_g, rows, LANES), jnp.float32),
                        pltpu.VMEM((DIL_HEADS, rows, HEAD_DIM), jnp.float32)],
        compiler_params=_params(("parallel",)),
        name="merge_groups",
    )(*outs, *lses)


def _pack_halves(v):
    c = v.shape[1] // 2
    r = v.astype(jnp.bfloat16).astype(jnp.float32)
    lo = lax.bitcast_convert_type(r[:, :c], jnp.uint32)
    hi = lax.bitcast_convert_type(r[:, c:], jnp.uint32)
    return (lo >> 16) | (hi & jnp.uint32(0xFFFF0000))


def _unpack_halves(w):
    lo = lax.bitcast_convert_type(w << 16, jnp.float32)
    hi = lax.bitcast_convert_type(w & jnp.uint32(0xFFFF0000), jnp.float32)
    return jnp.concatenate([lo, hi], axis=1)


def _router_body(x_ref, g_ref, w_ref, b_ref, o_ref, hp_ref, cnt_ref, carry):
    i = pl.program_id(0)

    @pl.when(i == 0)
    def _():
        carry[...] = jnp.zeros_like(carry)

    x = x_ref[...]
    ms = jnp.mean(x * x, axis=-1, keepdims=True)
    hf = x * lax.rsqrt(ms + EPS) * g_ref[...]
    hp_ref[...] = _pack_halves(hf)
    logits = jnp.dot(hf.astype(jnp.bfloat16), w_ref[...], preferred_element_type=jnp.float32) + b_ref[...]
    rows = logits.shape[0]
    lane = lax.broadcasted_iota(jnp.int32, logits.shape, 1)
    neg = -jnp.inf
    gl = jnp.where(lane < N_EXPERT_GROUPS, logits, neg)
    gmax = jnp.max(gl, axis=-1, keepdims=True)
    g_idx = jnp.min(jnp.where(gl == gmax, lane, LANES), axis=-1, keepdims=True)
    g_w = 1.0 / jnp.sum(jnp.exp(gl - gmax), axis=-1, keepdims=True)
    lo = N_EXPERT_GROUPS + EXPERTS_PER_GROUP * g_idx
    el = jnp.where(jnp.logical_and(lane >= lo, lane < lo + EXPERTS_PER_GROUP), logits, neg)
    v1 = jnp.max(el, axis=-1, keepdims=True)
    i1 = jnp.min(jnp.where(el == v1, lane, LANES), axis=-1, keepdims=True)
    el2 = jnp.where(lane == i1, neg, el)
    v2 = jnp.max(el2, axis=-1, keepdims=True)
    i2 = jnp.min(jnp.where(el2 == v2, lane, LANES), axis=-1, keepdims=True)
    e2 = jnp.exp(v2 - v1)
    p1 = 1.0 / (1.0 + e2)
    p2 = e2 / (1.0 + e2)
    oh1 = lane == i1 - N_EXPERT_GROUPS
    oh2 = lane == i2 - N_EXPERT_GROUPS
    oh1f, oh2f = oh1.astype(jnp.float32), oh2.astype(jnp.float32)
    before = (lax.broadcasted_iota(jnp.int32, (rows, rows), 1)
              < lax.broadcasted_iota(jnp.int32, (rows, rows), 0)).astype(jnp.bfloat16)
    pre1 = jnp.dot(before, oh1f.astype(jnp.bfloat16), preferred_element_type=jnp.float32)
    pre2 = jnp.dot(before, oh2f.astype(jnp.bfloat16), preferred_element_type=jnp.float32)
    cnt1 = jnp.sum(oh1f, axis=0, keepdims=True)
    cnt2 = jnp.sum(oh2f, axis=0, keepdims=True)
    base = carry[...]
    rank1 = jnp.sum(jnp.where(oh1, pre1 + base, 0.0), axis=-1, keepdims=True)
    rank2 = jnp.sum(jnp.where(oh2, pre2 + (base + cnt1), 0.0), axis=-1, keepdims=True)
    total = base + cnt1 + cnt2
    carry[...] = total
    cnt_ref[...] = total
    out = jnp.where(lane == 0, (i1 - N_EXPERT_GROUPS).astype(jnp.float32),
          jnp.where(lane == 1, (i2 - N_EXPERT_GROUPS).astype(jnp.float32),
          jnp.where(lane == 2, g_w * p1,
          jnp.where(lane == 3, g_w * p2,
          jnp.where(lane == 4, rank1,
          jnp.where(lane == 5, rank2, 0.0))))))
    o_ref[...] = out


def _router(x, g, w_rg, b_rg, w_re, b_re, rows=512):
    n, d = x.shape
    pad = LANES - N_EXPERT_GROUPS - N_EXPERTS
    w = jnp.concatenate([w_rg, w_re, jnp.zeros((d, pad), w_rg.dtype)], axis=1).astype(jnp.bfloat16)
    b = jnp.concatenate([b_rg, b_re, jnp.zeros((pad,), b_rg.dtype)]).reshape(1, LANES)
    return pl.pallas_call(
        _router_body,
        grid=(n // rows,),
        in_specs=[pl.BlockSpec((rows, d), lambda i: (i, 0)),
                  pl.BlockSpec((1, d), lambda i: (0, 0)),
                  pl.BlockSpec((d, LANES), lambda i: (0, 0)),
                  pl.BlockSpec((1, LANES), lambda i: (0, 0))],
        out_specs=[pl.BlockSpec((rows, LANES), lambda i: (i, 0)),
                   pl.BlockSpec((rows, d // 2), lambda i: (i, 0)),
                   pl.BlockSpec((1, LANES), lambda i: (0, 0))],
        out_shape=[jax.ShapeDtypeStruct((n, LANES), jnp.float32),
                   jax.ShapeDtypeStruct((n, d // 2), jnp.uint32),
                   jax.ShapeDtypeStruct((1, LANES), jnp.float32)],
        scratch_shapes=[pltpu.VMEM((1, LANES), jnp.float32)],
        compiler_params=_params(("arbitrary",)),
        name="router",
    )(x, g.reshape(1, d), w, b)


def _dispatch_body(s1_ref, s2_ref, hp_ref, xb_in_hbm, xb_hbm, stage, sem):
    del xb_in_hbm
    i = pl.program_id(0)
    n_steps = pl.num_programs(0)
    rows = s1_ref.shape[2]
    slot = i % 2

    def wait_slot(of_slot):
        for _ in range(TOP_K):
            pltpu.make_async_copy(stage.at[of_slot], stage.at[of_slot], sem.at[of_slot]).wait()

    @pl.when(i >= 2)
    def _():
        wait_slot(slot)

    stage[slot] = hp_ref[...]

    def issue(r, carry):
        src = stage.at[slot, pl.ds(r, 1)]
        pltpu.make_async_copy(src, xb_hbm.at[pl.ds(s1_ref[0, 0, r], 1)], sem.at[slot]).start()
        pltpu.make_async_copy(src, xb_hbm.at[pl.ds(s2_ref[0, 0, r], 1)], sem.at[slot]).start()
        return carry
    lax.fori_loop(0, rows, issue, 0, unroll=8)

    @pl.when(i == n_steps - 1)
    def _():
        @pl.when(i >= 1)
        def _():
            wait_slot(1 - slot)
        wait_slot(slot)


def _dispatch(hp, slot1, slot2, xb_zero, rows=512):
    n, c = hp.shape
    p_total = xb_zero.shape[0]
    steps = n // rows
    tab = pl.BlockSpec((1, 1, rows), lambda i: (i, 0, 0), memory_space=pltpu.SMEM)
    any_spec = pl.BlockSpec(memory_space=pl.ANY)
    return pl.pallas_call(
        _dispatch_body,
        grid=(steps,),
        in_specs=[tab, tab, pl.BlockSpec((rows, c), lambda i: (i, 0)), any_spec],
        out_specs=any_spec,
        out_shape=jax.ShapeDtypeStruct((p_total, c), hp.dtype),
        scratch_shapes=[pltpu.VMEM((2, rows, c), hp.dtype),
                        pltpu.SemaphoreType.DMA((2,))],
        input_output_aliases={3: 0},
        compiler_params=_params(("arbitrary",)),
        name="moe_dispatch",
    )(slot1.reshape(steps, 1, rows), slot2.reshape(steps, 1, rows), hp, xb_zero)


def _side_cast_specs(side, n_steps, index_of_step, dtype):
    in_specs, args, out_specs, out_shape = [], [], [], []
    for stack, layer in side:
        cols = stack.shape[-1]
        flat = stack.reshape(-1, cols)
        layer_rows = flat.shape[0] // stack.shape[0]
        n_slabs = 1
        while (n_slabs * 2 <= n_steps and layer_rows % (n_slabs * 2) == 0
               and (layer_rows // (n_slabs * 2)) % 16 == 0):
            n_slabs *= 2
        slab_rows = layer_rows // n_slabs

        def slab(*idx, last=n_slabs - 1):
            return jnp.minimum(index_of_step(*idx), last)
        in_specs.append(pl.BlockSpec((slab_rows, cols),
                                     lambda *idx, slab=slab, first=layer * n_slabs: (first + slab(*idx), 0)))
        args.append(flat)
        out_specs.append(pl.BlockSpec((slab_rows, cols), lambda *idx, slab=slab: (slab(*idx), 0)))
        out_shape.append(jax.ShapeDtypeStruct((layer_rows, cols), dtype))
    return in_specs, args, out_specs, out_shape


def _expert_body(blk_e_ref, nvalid_ref, xb_ref, wg_ref, wu_ref, wd_ref, *refs):
    del blk_e_ref
    n_side = (len(refs) - 1) // 2
    side_in, yb_ref, side_out = refs[:n_side], refs[n_side], refs[n_side + 1:]
    i = pl.program_id(0)
    for s_in, s_out in zip(side_in, side_out):
        s_out[...] = s_in[...].astype(s_out.dtype)

    @pl.when(nvalid_ref[i] > 0)
    def _():
        h = _unpack_halves(xb_ref[...]).astype(jnp.bfloat16)
        a = jax.nn.silu(jnp.dot(h, wg_ref[0], preferred_element_type=jnp.float32))
        a = a * jnp.dot(h, wu_ref[0], preferred_element_type=jnp.float32)
        y = jnp.dot(a.astype(jnp.bfloat16), wd_ref[0], preferred_element_type=jnp.float32)
        yb_ref[...] = _pack_halves(y)

    @pl.when(nvalid_ref[i] == 0)
    def _():
        yb_ref[...] = jnp.zeros_like(yb_ref)


def _moe_experts(xb, blk_e, nvalid, w_gate, w_up, w_down, side=()):
    p_total, c = xb.shape
    d = 2 * c
    d_e = w_gate.shape[-1]
    n_blk = p_total // MOE_ROWS
    s_in, s_args, s_out, s_shape = _side_cast_specs(side, n_blk, lambda i, *_: i, jnp.bfloat16)
    grid_spec = pltpu.PrefetchScalarGridSpec(
        num_scalar_prefetch=2,
        grid=(n_blk,),
        in_specs=[pl.BlockSpec((MOE_ROWS, c), lambda i, e, nv: (i, 0)),
                  pl.BlockSpec((1, d, d_e), lambda i, e, nv: (e[i], 0, 0)),
                  pl.BlockSpec((1, d, d_e), lambda i, e, nv: (e[i], 0, 0)),
                  pl.BlockSpec((1, d_e, d), lambda i, e, nv: (e[i], 0, 0))] + s_in,
        out_specs=[pl.BlockSpec((MOE_ROWS, c), lambda i, e, nv: (i, 0))] + s_out,
    )
    outs = pl.pallas_call(
        _expert_body,
        grid_spec=grid_spec,
        out_shape=[jax.ShapeDtypeStruct((p_total, c), xb.dtype)] + s_shape,
        compiler_params=_params(("arbitrary",)),
        name="moe_experts",
    )(blk_e, nvalid, xb, w_gate, w_up, w_down, *s_args)
    return outs[0], [o.reshape(stack.shape[1:]) for o, (stack, _) in zip(outs[1:], side)]


def _combine_body(s1_ref, s2_ref, s1n_ref, s2n_ref, x_ref, routed_ref, g_ref, yb_hbm, *refs, emit_sum):
    out_refs, ybuf, sem = refs[:-2], refs[-2], refs[-1]
    i = pl.program_id(0)
    n_steps = pl.num_programs(0)
    slot = i % 2
    rows = x_ref.shape[0]

    def gather(t1_ref, t2_ref, to_slot):
        def issue(r, carry):
            pltpu.make_async_copy(yb_hbm.at[pl.ds(t1_ref[0, 0, r], 1)], ybuf.at[to_slot, pl.ds(r, 1)],
                                  sem.at[to_slot]).start()
            pltpu.make_async_copy(yb_hbm.at[pl.ds(t2_ref[0, 0, r], 1)], ybuf.at[to_slot, pl.ds(rows + r, 1)],
                                  sem.at[to_slot]).start()
            return carry
        lax.fori_loop(0, rows, issue, 0, unroll=8)

    @pl.when(i == 0)
    def _():
        gather(s1_ref, s2_ref, 0)

    @pl.when(i + 1 < n_steps)
    def _():
        gather(s1n_ref, s2n_ref, 1 - slot)

    pltpu.make_async_copy(ybuf.at[slot], ybuf.at[slot], sem.at[slot]).wait()
    y1 = _unpack_halves(ybuf[slot, :rows])
    y2 = _unpack_halves(ybuf[slot, rows:])
    s = x_ref[...] + (y1 * routed_ref[:, 2:3] + y2 * routed_ref[:, 3:4])
    if emit_sum:
        out_refs[0][...] = s
    ms = jnp.mean(s * s, axis=-1, keepdims=True)
    out_refs[-1][...] = (s * lax.rsqrt(ms + EPS) * g_ref[...]).astype(out_refs[-1].dtype)


def _combine_norm(x, yb, routed, slot1, slot2, g, norm_dtype, emit_sum, rows=256):
    n, d = x.shape
    steps = n // rows
    row_spec = pl.BlockSpec((rows, d), lambda i: (i, 0))
    tab = pl.BlockSpec((1, 1, rows), lambda i: (i, 0, 0), memory_space=pltpu.SMEM)
    tab_next = pl.BlockSpec((1, 1, rows), lambda i: (jnp.minimum(i + 1, steps - 1), 0, 0),
                            memory_space=pltpu.SMEM)
    out_specs = [row_spec]
    out_shape = [jax.ShapeDtypeStruct((n, d), norm_dtype)]
    if emit_sum:
        out_specs = [row_spec] + out_specs
        out_shape = [jax.ShapeDtypeStruct((n, d), x.dtype)] + out_shape
    s1 = slot1.reshape(steps, 1, rows)
    s2 = slot2.reshape(steps, 1, rows)
    return pl.pallas_call(
        functools.partial(_combine_body, emit_sum=emit_sum),
        grid=(steps,),
        in_specs=[tab, tab, tab_next, tab_next, row_spec,
                  pl.BlockSpec((rows, LANES), lambda i: (i, 0)),
                  pl.BlockSpec((1, d), lambda i: (0, 0)),
                  pl.BlockSpec(memory_space=pl.ANY)],
        out_specs=out_specs,
        out_shape=out_shape,
        scratch_shapes=[pltpu.VMEM((2, TOP_K * rows, d // 2), yb.dtype),
                        pltpu.SemaphoreType.DMA((2,))],
        compiler_params=_params(("arbitrary",)),
        name="combine_norm",
    )(s1, s2, s1, s2, x, routed, g.reshape(1, d), yb)


def _moe_slot_rows(n):
    return n * TOP_K + N_EXPERTS * MOE_ROWS


def _moe_layer(x, ffn_g, next_g, norm_dtype, emit_sum, w_rg, b_rg, w_re, b_re, w_gate, w_up, w_down, xb_zero,
               side=()):
    n, d = x.shape
    routed, hp, counts = _router(x, ffn_g, w_rg, b_rg, w_re, b_re)

    n_blk = _moe_slot_rows(n) // MOE_ROWS
    experts = jnp.arange(N_EXPERTS, dtype=jnp.int32)
    counts = counts[0, :N_EXPERTS].astype(jnp.int32)
    padded = ((counts + MOE_ROWS - 1) // MOE_ROWS) * MOE_ROWS
    pends = jnp.cumsum(padded)
    pstarts = pends - padded
    expert_id = routed[:, 0:TOP_K].astype(jnp.int32)
    rank = routed[:, 2 * TOP_K:3 * TOP_K].astype(jnp.int32)
    slot = jnp.sum(jnp.where(expert_id[:, :, None] == experts, pstarts, 0), axis=-1) + rank
    blk_start = jnp.arange(n_blk, dtype=jnp.int32) * MOE_ROWS
    blk_e = jnp.minimum(jnp.sum(blk_start[:, None] >= pends[None, :], axis=1), N_EXPERTS - 1).astype(jnp.int32)
    seg_end = jnp.sum(jnp.where(blk_e[:, None] == experts, pstarts + counts, 0), axis=1)
    nvalid = jnp.clip(seg_end - blk_start, 0, MOE_ROWS).astype(jnp.int32)

    xb = _dispatch(hp, slot[:, 0], slot[:, 1], xb_zero)
    yb, cast = _moe_experts(xb, blk_e, nvalid, w_gate, w_up, w_down, side)
    return _combine_norm(x, yb, routed, slot[:, 0], slot[:, 1], next_g, norm_dtype, emit_sum), cast


def kernel(x, mix_norm, ffn_norm, sgu_w_in, sgu_b_in, sgu_v_gain, sgu_v_bias, sgu_w_spatial, sgu_b_spatial,
           sgu_w_out, dil_w_qkv, dil_w_out, rel_bias, router_w_group, router_b_group, router_w_expert,
           router_b_expert, moe_w_gate, moe_w_up, moe_w_down, final_norm):
    batch, seq, d = x.shape
    n = batch * seq
    bf16 = jnp.bfloat16
    xf = x.reshape(n, d)
    depth = mix_norm.shape[0]
    n_dil = len(DIL_CONFIGS)
    h = _rmsnorm(xf, mix_norm[0], bf16)
    out = None
    dense = {}
    for i in range(depth):
        j = i // 2
        xb_shape = (_moe_slot_rows(n), d // 2, jnp.uint32)
        if i % 2 == 0:
            w_in, w_out = dense.get(i, (sgu_w_in[j], sgu_w_out[j]))
            z, (w_gate, w_up), xb_zero = _matmul(h, w_in, bias=sgu_b_in[j], use_gelu=True, out_dtype=bf16,
                                                 side=((moe_w_gate, i), (moe_w_up, i)), zero_fill=xb_shape)
            y = _sgu_gate(z, sgu_v_gain[j], sgu_v_bias[j], sgu_w_spatial[j], sgu_b_spatial[j])
            xf, (w_down,) = _matmul(y, w_out, residual=xf, out_dtype=jnp.float32, side=((moe_w_down, i),))
        else:
            w_qkv, w_out = dense.get(i, (dil_w_qkv[j], dil_w_out[j]))
            outs, lses, cast = [], [], []
            tn = 512
            per_part = DIL_INNER // tn
            for g, (window, dilation) in enumerate(DIL_CONFIGS):
                res = _matmul(
                    h, w_qkv, out_dtype=bf16, n_out=3 * DIL_INNER, tn=tn,
                    col_block_map=lambda c, g=g: ((c // per_part) * n_dil + g) * per_part + c % per_part,
                    col_scale=(per_part, HEAD_DIM ** -0.5),
                    regroup=(batch, seq, dilation), side=(((moe_w_gate, moe_w_up, moe_w_down)[g], i),),
                    zero_fill=xb_shape if g == 0 else None)
                qkv, (w_cast,) = res[:2]
                if g == 0:
                    xb_zero = res[2]
                cast.append(w_cast)
                bias = _band_bias(rel_bias[:, g * DIL_HEADS:(g + 1) * DIL_HEADS], window, dilation)
                o, lse = _dilated_group(qkv, bias)
                outs.append(o)
                lses.append(lse)
            w_gate, w_up, w_down = cast
            merged = _merge_groups(outs, lses)
            xf = _matmul(merged, w_out, residual=xf, out_dtype=jnp.float32)
        last = i + 1 == depth
        side = ()
        if not last:
            j_next = (i + 1) // 2
            side = (((sgu_w_in, j_next), (sgu_w_out, j_next)) if (i + 1) % 2 == 0
                    else ((dil_w_qkv, j_next), (dil_w_out, j_next)))
        res, cast = _moe_layer(xf, ffn_norm[i], final_norm if last else mix_norm[i + 1],
                               x.dtype if last else bf16, not last,
                               router_w_group[i], router_b_group[i], router_w_expert[i], router_b_expert[i],
                               w_gate, w_up, w_down, xb_zero, side)
        if cast:
            dense[i + 1] = tuple(cast)
        if last:
            (out,) = res
        else:
            xf, h = res
    return out.reshape(batch, seq, d)
```

```python
import functools
import math

import jax
import jax.numpy as jnp
from jax import lax
from jax.experimental import pallas as pl
from jax.experimental.pallas import tpu as pltpu

EPS = 1e-6
SGU_CHUNK = 128
SGU_GROUP_DIM = 128
DIL_CONFIGS = ((128, 1), (512, 4), (2048, 16))
DIL_HEADS = 16
HEAD_DIM = 128
DIL_BLOCK = 128
DIL_INNER = DIL_HEADS * HEAD_DIM
NUM_BUCKETS = 32
MAX_DISTANCE = 2048
N_EXPERT_GROUPS = 4
EXPERTS_PER_GROUP = 8
N_EXPERTS = N_EXPERT_GROUPS * EXPERTS_PER_GROUP
TOP_K = 2
MASK_VALUE = -1e30

LANES = 128
MOE_ROWS = 256
VMEM_LIMIT = 52 * 1024 * 1024


def _params(sem, vmem=VMEM_LIMIT):
    return pltpu.CompilerParams(dimension_semantics=sem, vmem_limit_bytes=vmem)


def _rmsnorm_body(x_ref, g_ref, o_ref):
    x = x_ref[...]
    ms = jnp.mean(x * x, axis=-1, keepdims=True)
    o_ref[...] = (x * lax.rsqrt(ms + EPS) * g_ref[...]).astype(o_ref.dtype)


def _rmsnorm(x, g, out_dtype, rows=512):
    n, d = x.shape
    return pl.pallas_call(
        _rmsnorm_body,
        grid=(n // rows,),
        in_specs=[pl.BlockSpec((rows, d), lambda i: (i, 0)),
                  pl.BlockSpec((1, d), lambda i: (0, 0))],
        out_specs=pl.BlockSpec((rows, d), lambda i: (i, 0)),
        out_shape=jax.ShapeDtypeStruct((n, d), out_dtype),
        compiler_params=_params(("parallel",)),
        name="rmsnorm",
    )(x, g.reshape(1, d))


def _regroup_pitch(dilation):
    return dilation + 4 if dilation % 8 == 0 else dilation


def _matmul_body(*refs, has_bias, use_gelu, has_residual, dilation, n_side, has_zero, col_scale):
    a_ref, w_ref = refs[0], refs[1]
    pos = 2
    acc = jnp.dot(a_ref[...], w_ref[...].astype(a_ref.dtype), preferred_element_type=jnp.float32)
    if col_scale is not None:
        blocks, scale = col_scale
        acc = acc * jnp.where(pl.program_id(1) < blocks, jnp.float32(scale), jnp.float32(1.0))
    if has_bias:
        acc = acc + refs[pos][...]
        pos += 1
    if use_gelu:
        acc = jax.nn.gelu(acc)
    if has_residual:
        acc = refs[pos][...] + acc
        pos += 1
    side_in = refs[pos:pos + n_side]
    o_ref = refs[pos + n_side]
    side_out = refs[pos + n_side + 1:pos + 2 * n_side + 1]
    pos += 2 * n_side + 1
    zero_ref = None
    if has_zero:
        zero_ref = refs[pos]
        pos += 1
    if dilation == 1:
        o_ref[...] = acc.astype(o_ref.dtype).reshape(o_ref.shape)
    else:
        acc_ref = refs[pos]
        sub = acc.shape[0] // dilation
        pitch = _regroup_pitch(dilation)
        for panel in range(acc_ref.shape[0]):
            cols = slice(panel * LANES, (panel + 1) * LANES)
            if pitch == dilation:
                acc_ref[panel] = acc[:, cols]
            else:
                for l in range(sub):
                    acc_ref[panel, l * pitch:l * pitch + dilation, :] = acc[l * dilation:(l + 1) * dilation, cols]
            for r in range(dilation):
                o_ref[0, r, :, cols] = acc_ref[panel, pl.ds(r, sub, stride=pitch), :].astype(o_ref.dtype)
    for s_in, s_out in zip(side_in, side_out):
        s_out[...] = s_in[...].astype(s_out.dtype)
    if has_zero:
        zero_ref[...] = jnp.zeros_like(zero_ref)


def _matmul(a, w, *, bias=None, use_gelu=False, residual=None, out_dtype, col_block_map=None, n_out=None,
            regroup=None, side=(), zero_fill=None, col_scale=None, tm=1024, tn=512):
    m, k = a.shape
    n = w.shape[1] if n_out is None else n_out
    tm, tn = min(tm, m), min(tn, n)
    grid = (m // tm, n // tn)
    w_map = (lambda i, j: (0, j)) if col_block_map is None else (lambda i, j: (0, col_block_map(j)))
    in_specs = [pl.BlockSpec((tm, k), lambda i, j: (i, 0)),
                pl.BlockSpec((k, tn), w_map)]
    args = [a, w]
    if bias is not None:
        in_specs.append(pl.BlockSpec((1, tn), lambda i, j: (0, j)))
        args.append(bias.reshape(1, n))
    if residual is not None:
        in_specs.append(pl.BlockSpec((tm, tn), lambda i, j: (i, j)))
        args.append(residual)
    scratch = []
    dilation = 1
    if regroup is None:
        out_specs = [pl.BlockSpec((tm, tn), lambda i, j: (i, j))]
        out_shape = [jax.ShapeDtypeStruct((m, n), out_dtype)]
    else:
        batch, seq, dilation = regroup
        tiles = seq // tm
        out_specs = [pl.BlockSpec((1, dilation, tm // dilation, tn), lambda i, j: (i // tiles, 0, i % tiles, j))]
        out_shape = [jax.ShapeDtypeStruct((batch, dilation, seq // dilation, n), out_dtype)]
        if dilation > 1:
            scratch = [pltpu.VMEM((tn // LANES, tm // dilation * _regroup_pitch(dilation), LANES), jnp.float32)]
    n_steps = grid[0] * grid[1]

    def step_of(i, j):
        return i * grid[1] + j
    s_in, s_args, s_out, s_shape = _side_cast_specs(side, n_steps, step_of, a.dtype)
    in_specs += s_in
    args += s_args
    out_specs += s_out
    out_shape += s_shape
    if zero_fill is not None:
        z_rows, z_cols, z_dtype = zero_fill
        zero_slabs = 1
        while zero_slabs * 2 <= n_steps and z_rows % (zero_slabs * 2) == 0 and (z_rows // (zero_slabs * 2)) % 8 == 0:
            zero_slabs *= 2
        out_specs.append(pl.BlockSpec((z_rows // zero_slabs, z_cols),
                                      lambda i, j: (jnp.minimum(step_of(i, j), zero_slabs - 1), 0)))
        out_shape.append(jax.ShapeDtypeStruct((z_rows, z_cols), z_dtype))
    body = functools.partial(_matmul_body, has_bias=bias is not None, use_gelu=use_gelu,
                             has_residual=residual is not None, dilation=dilation, n_side=len(side),
                             has_zero=zero_fill is not None, col_scale=col_scale)
    riders = bool(side) or zero_fill is not None
    outs = pl.pallas_call(
        body,
        grid=grid,
        in_specs=in_specs,
        out_specs=out_specs,
        out_shape=out_shape,
        scratch_shapes=scratch,
        compiler_params=_params(("arbitrary", "arbitrary") if riders else ("parallel", "arbitrary")),
        name="matmul",
    )(*args)
    if not riders:
        return outs[0]
    cast = [o.reshape(stack.shape[1:]) for o, (stack, _) in zip(outs[1:], side)]
    return (outs[0], cast, outs[-1]) if zero_fill is not None else (outs[0], cast)


def _sgu_gate_body(u_ref, v_ref, gain_ref, vbias_ref, ws_ref, bs_ref, o_ref, *, chunks):
    v = v_ref[...].astype(jnp.float32)
    mu = jnp.mean(v, axis=-1, keepdims=True)
    vc = v - mu
    var = jnp.mean(vc * vc, axis=-1, keepdims=True)
    vn = (vc * lax.rsqrt(var + EPS) * gain_ref[...] + vbias_ref[...]).astype(jnp.bfloat16)
    n_groups = ws_ref.shape[0]
    t_idx = lax.broadcasted_iota(jnp.int32, (SGU_CHUNK, SGU_CHUNK), 0)
    s_idx = lax.broadcasted_iota(jnp.int32, (SGU_CHUNK, SGU_CHUNK), 1)
    causal = s_idx <= t_idx
    for g in range(n_groups):
        cols = slice(g * SGU_GROUP_DIM, (g + 1) * SGU_GROUP_DIM)
        w = jnp.where(causal, ws_ref[g], jnp.zeros((), ws_ref.dtype))
        for c in range(chunks):
            rows = slice(c * SGU_CHUNK, (c + 1) * SGU_CHUNK)
            vm = jnp.dot(w, vn[rows, cols], preferred_element_type=jnp.float32) + bs_ref[:, cols]
            o_ref[rows, cols] = (u_ref[rows, cols].astype(jnp.float32) * vm).astype(o_ref.dtype)


def _sgu_gate(z, v_gain, v_bias, w_s, b_s, chunks=2):
    n, two_w = z.shape
    width = two_w // 2
    rows = chunks * SGU_CHUNK
    n_groups = w_s.shape[0]
    b_full = jnp.repeat(b_s.T, SGU_GROUP_DIM, axis=1)
    body = functools.partial(_sgu_gate_body, chunks=chunks)
    return pl.pallas_call(
        body,
        grid=(n // rows,),
        in_specs=[pl.BlockSpec((rows, width), lambda i: (i, 0)),
                  pl.BlockSpec((rows, width), lambda i: (i, 1)),
                  pl.BlockSpec((1, width), lambda i: (0, 0)),
                  pl.BlockSpec((1, width), lambda i: (0, 0)),
                  pl.BlockSpec((n_groups, SGU_CHUNK, SGU_CHUNK), lambda i: (0, 0, 0)),
                  pl.BlockSpec((SGU_CHUNK, width), lambda i: (0, 0))],
        out_specs=pl.BlockSpec((rows, width), lambda i: (i, 0)),
        out_shape=jax.ShapeDtypeStruct((n, width), jnp.bfloat16),
        compiler_params=_params(("parallel",)),
        name="sgu_gate",
    )(z, z, v_gain.reshape(1, width), v_bias.reshape(1, width), w_s.astype(jnp.bfloat16), b_full)


def _t5_causal_bucket(dist):
    max_exact = NUM_BUCKETS // 2
    d = jnp.maximum(dist, 0)
    df = jnp.maximum(d, 1).astype(jnp.float32)
    large = max_exact + (jnp.log(df / max_exact) / math.log(MAX_DISTANCE / max_exact)
                         * (NUM_BUCKETS - max_exact)).astype(jnp.int32)
    large = jnp.minimum(large, NUM_BUCKETS - 1)
    return jnp.where(d < max_exact, d, large)


def _band_bias(bias_table, window, dilation):
    span = window // dilation
    qi = jnp.arange(DIL_BLOCK)[:, None]
    kj = jnp.arange(2 * DIL_BLOCK)[None, :]
    sub_dist = qi + DIL_BLOCK - kj
    band = (sub_dist >= 0) & (sub_dist <= span)
    bucket = _t5_causal_bucket(sub_dist * dilation)
    onehot = (bucket[:, :, None] == jnp.arange(NUM_BUCKETS)).astype(jnp.float32)
    bias = jnp.einsum('qkb,bh->hqk', onehot, bias_table.astype(jnp.float32), precision=lax.Precision.HIGHEST)
    bias = jnp.where(band[None], bias, MASK_VALUE)
    return jnp.stack([bias, jnp.where(kj[None] < DIL_BLOCK, MASK_VALUE, bias)])


def _attn_body(q_ref, kp_ref, kc_ref, vp_ref, vc_ref, bias_ref, o_ref, lse_ref):
    table = jnp.where(pl.program_id(2) == 0, 1, 0)
    lane = lax.broadcasted_iota(jnp.int32, (DIL_BLOCK, LANES), 1)
    lse_all = jnp.zeros((DIL_BLOCK, LANES), jnp.float32)
    for h in range(DIL_HEADS):
        cols = slice(h * HEAD_DIM, (h + 1) * HEAD_DIM)
        q = q_ref[0, 0, :, cols]
        k = jnp.concatenate([kp_ref[0, 0, :, cols], kc_ref[0, 0, :, cols]], axis=0)
        v = jnp.concatenate([vp_ref[0, 0, :, cols], vc_ref[0, 0, :, cols]], axis=0)
        s = lax.dot_general(q, k, (((1,), (1,)), ((), ())), preferred_element_type=jnp.float32)
        s = s + bias_ref[table, h]
        m = jnp.max(s, axis=-1, keepdims=True)
        p = jnp.exp(s - m)
        den = jnp.sum(p, axis=-1, keepdims=True)
        o = jnp.dot(p.astype(jnp.bfloat16), v, preferred_element_type=jnp.float32) / den
        o_ref[0, 0, :, cols] = o.astype(o_ref.dtype)
        lse_all = jnp.where(lane == h, m + jnp.log(den), lse_all)
    lse_ref[0, 0] = lse_all


def _dilated_group(qkv, band_bias):
    batch, dilation, sub_len, _ = qkv.shape
    nb = sub_len // DIL_BLOCK

    def spec(part, prev):
        def index_map(b, r, n):
            return (b, r, jnp.maximum(n - 1, 0) if prev else n, part)
        return pl.BlockSpec((1, 1, DIL_BLOCK, DIL_INNER), index_map)

    return pl.pallas_call(
        _attn_body,
        grid=(batch, dilation, nb),
        in_specs=[spec(0, False), spec(1, True), spec(1, False), spec(2, True), spec(2, False),
                  pl.BlockSpec((2, DIL_HEADS, DIL_BLOCK, 2 * DIL_BLOCK), lambda b, r, n: (0, 0, 0, 0))],
        out_specs=[pl.BlockSpec((1, 1, DIL_BLOCK, DIL_INNER), lambda b, r, n: (b, r, n, 0)),
                   pl.BlockSpec((1, 1, DIL_BLOCK, LANES), lambda b, r, n: (b, r, n, 0))],
        out_shape=[jax.ShapeDtypeStruct((batch, dilation, sub_len, DIL_INNER), jnp.bfloat16),
                   jax.ShapeDtypeStruct((batch, dilation, sub_len, LANES), jnp.float32)],
        compiler_params=_params(("parallel", "parallel", "arbitrary")),
        name="dilated_attn",
    )(qkv, qkv, qkv, qkv, qkv, band_bias)


def _merge_body(*refs, dilations):
    n_g = len(dilations)
    o_refs, l_refs = refs[:n_g], refs[n_g:2 * n_g]
    out_ref, ltok, wtok, acc = refs[2 * n_g:]
    rows = out_ref.shape[0]

    def residue_rows(r, d):
        return pl.ds(r, rows // d, stride=d) if d > 1 else slice(None)

    for g, d in enumerate(dilations):
        for r in range(d):
            ltok[g, residue_rows(r, d), :] = l_refs[g][0, r]
    lse = [ltok[g] for g in range(n_g)]
    m = functools.reduce(jnp.maximum, lse)
    e = [jnp.exp(l - m) for l in lse]
    tot = functools.reduce(jnp.add, e)
    for g in range(n_g):
        wtok[g] = e[g] / tot
    for g, d in enumerate(dilations):
        for r in range(d):
            rr = residue_rows(r, d)
            w = wtok[g, rr, :]
            for h in range(DIL_HEADS):
                cols = slice(h * HEAD_DIM, (h + 1) * HEAD_DIM)
                term = o_refs[g][0, r, :, cols].astype(jnp.float32) * w[:, h:h + 1]
                if g == 0:
                    acc[h, rr, :] = term
                else:
                    acc[h, rr, :] = acc[h, rr, :] + term
    for h in range(DIL_HEADS):
        out_ref[:, h * HEAD_DIM:(h + 1) * HEAD_DIM] = acc[h].astype(out_ref.dtype)


def _merge_groups(outs, lses, rows=512):
    batch = outs[0].shape[0]
    dilations = tuple(o.shape[1] for o in outs)
    seq = outs[0].shape[1] * outs[0].shape[2]
    tiles = seq // rows

    def spec(d, width):
        return pl.BlockSpec((1, d, rows // d, width), lambda i: (i // tiles, 0, i % tiles, 0))

    n_g = len(outs)
    return pl.pallas_call(
        functools.partial(_merge_body, dilations=dilations),
        grid=(batch * tiles,),
        in_specs=[spec(d, DIL_INNER) for d in dilations] + [spec(d, LANES) for d in dilations],
        out_specs=pl.BlockSpec((rows, DIL_INNER), lambda i: (i, 0)),
        out_shape=jax.ShapeDtypeStruct((batch * seq, DIL_INNER), jnp.bfloat16),
        scratch_shapes=[pltpu.VMEM((n_g, rows, LANES), jnp.float32),
                        pltpu.VMEM((n_g, rows, LANES), jnp.float32),
                        pltpu.VMEM((DIL_HEADS, rows, HEAD_DIM), jnp.float32)],
        compiler_params=_params(("parallel",)),
        name="merge_groups",
    )(*outs, *lses)


def _pack_halves(v):
    c = v.shape[1] // 2
    r = v.astype(jnp.bfloat16).astype(jnp.float32)
    lo = lax.bitcast_convert_type(r[:, :c], jnp.uint32)
    hi = lax.bitcast_convert_type(r[:, c:], jnp.uint32)
    return (lo >> 16) | (hi & jnp.uint32(0xFFFF0000))


def _unpack_halves(w):
    lo = lax.bitcast_convert_type(w << 16, jnp.float32)
    hi = lax.bitcast_convert_type(w & jnp.uint32(0xFFFF0000), jnp.float32)
    return jnp.concatenate([lo, hi], axis=1)


def _router_body(x_ref, g_ref, w_ref, b_ref, o_ref, hp_ref, cnt_ref, carry):
    i = pl.program_id(0)

    @pl.when(i == 0)
    def _():
        carry[...] = jnp.zeros_like(carry)

    x = x_ref[...]
    ms = jnp.mean(x * x, axis=-1, keepdims=True)
    hf = x * lax.rsqrt(ms + EPS) * g_ref[...]
    hp_ref[...] = _pack_halves(hf)
    logits = jnp.dot(hf.astype(jnp.bfloat16), w_ref[...], preferred_element_type=jnp.float32) + b_ref[...]
    rows = logits.shape[0]
    lane = lax.broadcasted_iota(jnp.int32, logits.shape, 1)
    neg = -jnp.inf
    gl = jnp.where(lane < N_EXPERT_GROUPS, logits, neg)
    gmax = jnp.max(gl, axis=-1, keepdims=True)
    g_idx = jnp.min(jnp.where(gl == gmax, lane, LANES), axis=-1, keepdims=True)
    g_w = 1.0 / jnp.sum(jnp.exp(gl - gmax), axis=-1, keepdims=True)
    lo = N_EXPERT_GROUPS + EXPERTS_PER_GROUP * g_idx
    el = jnp.where(jnp.logical_and(lane >= lo, lane < lo + EXPERTS_PER_GROUP), logits, neg)
    v1 = jnp.max(el, axis=-1, keepdims=True)
    i1 = jnp.min(jnp.where(el == v1, lane, LANES), axis=-1, keepdims=True)
    el2 = jnp.where(lane == i1, neg, el)
    v2 = jnp.max(el2, axis=-1, keepdims=True)
    i2 = jnp.min(jnp.where(el2 == v2, lane, LANES), axis=-1, keepdims=True)
    e2 = jnp.exp(v2 - v1)
    p1 = 1.0 / (1.0 + e2)
    p2 = e2 / (1.0 + e2)
    oh1 = lane == i1 - N_EXPERT_GROUPS
    oh2 = lane == i2 - N_EXPERT_GROUPS
    oh1f, oh2f = oh1.astype(jnp.float32), oh2.astype(jnp.float32)
    before = (lax.broadcasted_iota(jnp.int32, (rows, rows), 1)
              < lax.broadcasted_iota(jnp.int32, (rows, rows), 0)).astype(jnp.bfloat16)
    pre1 = jnp.dot(before, oh1f.astype(jnp.bfloat16), preferred_element_type=jnp.float32)
    pre2 = jnp.dot(before, oh2f.astype(jnp.bfloat16), preferred_element_type=jnp.float32)
    cnt1 = jnp.sum(oh1f, axis=0, keepdims=True)
    cnt2 = jnp.sum(oh2f, axis=0, keepdims=True)
    base = carry[...]
    rank1 = jnp.sum(jnp.where(oh1, pre1 + base, 0.0), axis=-1, keepdims=True)
    rank2 = jnp.sum(jnp.where(oh2, pre2 + (base + cnt1), 0.0), axis=-1, keepdims=True)
    total = base + cnt1 + cnt2
    carry[...] = total
    cnt_ref[...] = total
    out = jnp.where(lane == 0, (i1 - N_EXPERT_GROUPS).astype(jnp.float32),
          jnp.where(lane == 1, (i2 - N_EXPERT_GROUPS).astype(jnp.float32),
          jnp.where(lane == 2, g_w * p1,
          jnp.where(lane == 3, g_w * p2,
          jnp.where(lane == 4, rank1,
          jnp.where(lane == 5, rank2, 0.0))))))
    o_ref[...] = out


def _router(x, g, w_rg, b_rg, w_re, b_re, rows=512):
    n, d = x.shape
    pad = LANES - N_EXPERT_GROUPS - N_EXPERTS
    w = jnp.concatenate([w_rg, w_re, jnp.zeros((d, pad), w_rg.dtype)], axis=1).astype(jnp.bfloat16)
    b = jnp.concatenate([b_rg, b_re, jnp.zeros((pad,), b_rg.dtype)]).reshape(1, LANES)
    return pl.pallas_call(
        _router_body,
        grid=(n // rows,),
        in_specs=[pl.BlockSpec((rows, d), lambda i: (i, 0)),
                  pl.BlockSpec((1, d), lambda i: (0, 0)),
                  pl.BlockSpec((d, LANES), lambda i: (0, 0)),
                  pl.BlockSpec((1, LANES), lambda i: (0, 0))],
        out_specs=[pl.BlockSpec((rows, LANES), lambda i: (i, 0)),
                   pl.BlockSpec((rows, d // 2), lambda i: (i, 0)),
                   pl.BlockSpec((1, LANES), lambda i: (0, 0))],
        out_shape=[jax.ShapeDtypeStruct((n, LANES), jnp.float32),
                   jax.ShapeDtypeStruct((n, d // 2), jnp.uint32),
                   jax.ShapeDtypeStruct((1, LANES), jnp.float32)],
        scratch_shapes=[pltpu.VMEM((1, LANES), jnp.float32)],
        compiler_params=_params(("arbitrary",)),
        name="router",
    )(x, g.reshape(1, d), w, b)


def _dispatch_body(s1_ref, s2_ref, hp_ref, xb_in_hbm, xb_hbm, stage, sem):
    del xb_in_hbm
    i = pl.program_id(0)
    n_steps = pl.num_programs(0)
    rows = s1_ref.shape[2]
    slot = i % 2

    def wait_slot(of_slot):
        for _ in range(TOP_K):
            pltpu.make_async_copy(stage.at[of_slot], stage.at[of_slot], sem.at[of_slot]).wait()

    @pl.when(i >= 2)
    def _():
        wait_slot(slot)

    stage[slot] = hp_ref[...]

    def issue(r, carry):
        src = stage.at[slot, pl.ds(r, 1)]
        pltpu.make_async_copy(src, xb_hbm.at[pl.ds(s1_ref[0, 0, r], 1)], sem.at[slot]).start(priority=0)
        pltpu.make_async_copy(src, xb_hbm.at[pl.ds(s2_ref[0, 0, r], 1)], sem.at[slot]).start(priority=1)
        return carry
    lax.fori_loop(0, rows, issue, 0, unroll=8)

    @pl.when(i == n_steps - 1)
    def _():
        @pl.when(i >= 1)
        def _():
            wait_slot(1 - slot)
        wait_slot(slot)


def _dispatch(hp, slot1, slot2, xb_zero, rows=512):
    n, c = hp.shape
    p_total = xb_zero.shape[0]
    steps = n // rows
    tab = pl.BlockSpec((1, 1, rows), lambda i: (i, 0, 0), memory_space=pltpu.SMEM)
    any_spec = pl.BlockSpec(memory_space=pl.ANY)
    return pl.pallas_call(
        _dispatch_body,
        grid=(steps,),
        in_specs=[tab, tab, pl.BlockSpec((rows, c), lambda i: (i, 0)), any_spec],
        out_specs=any_spec,
        out_shape=jax.ShapeDtypeStruct((p_total, c), hp.dtype),
        scratch_shapes=[pltpu.VMEM((2, rows, c), hp.dtype),
                        pltpu.SemaphoreType.DMA((2,))],
        input_output_aliases={3: 0},
        compiler_params=_params(("arbitrary",)),
        name="moe_dispatch",
    )(slot1.reshape(steps, 1, rows), slot2.reshape(steps, 1, rows), hp, xb_zero)


def _side_cast_specs(side, n_steps, index_of_step, dtype):
    in_specs, args, out_specs, out_shape = [], [], [], []
    for stack, layer in side:
        cols = stack.shape[-1]
        flat = stack.reshape(-1, cols)
        layer_rows = flat.shape[0] // stack.shape[0]
        n_slabs = 1
        while (n_slabs * 2 <= n_steps and layer_rows % (n_slabs * 2) == 0
               and (layer_rows // (n_slabs * 2)) % 16 == 0):
            n_slabs *= 2
        slab_rows = layer_rows // n_slabs

        def slab(*idx, last=n_slabs - 1):
            return jnp.minimum(index_of_step(*idx), last)
        in_specs.append(pl.BlockSpec((slab_rows, cols),
                                     lambda *idx, slab=slab, first=layer * n_slabs: (first + slab(*idx), 0)))
        args.append(flat)
        out_specs.append(pl.BlockSpec((slab_rows, cols), lambda *idx, slab=slab: (slab(*idx), 0)))
        out_shape.append(jax.ShapeDtypeStruct((layer_rows, cols), dtype))
    return in_specs, args, out_specs, out_shape


def _expert_body(blk_e_ref, nvalid_ref, xb_ref, wg_ref, wu_ref, wd_ref, *refs):
    del blk_e_ref
    n_side = (len(refs) - 1) // 2
    side_in, yb_ref, side_out = refs[:n_side], refs[n_side], refs[n_side + 1:]
    i = pl.program_id(0)
    for s_in, s_out in zip(side_in, side_out):
        s_out[...] = s_in[...].astype(s_out.dtype)

    @pl.when(nvalid_ref[i] > 0)
    def _():
        h = _unpack_halves(xb_ref[...]).astype(jnp.bfloat16)
        a = jax.nn.silu(jnp.dot(h, wg_ref[0], preferred_element_type=jnp.float32))
        a = a * jnp.dot(h, wu_ref[0], preferred_element_type=jnp.float32)
        y = jnp.dot(a.astype(jnp.bfloat16), wd_ref[0], preferred_element_type=jnp.float32)
        yb_ref[...] = _pack_halves(y)

    @pl.when(nvalid_ref[i] == 0)
    def _():
        yb_ref[...] = jnp.zeros_like(yb_ref)


def _moe_experts(xb, blk_e, nvalid, w_gate, w_up, w_down, side=()):
    p_total, c = xb.shape
    d = 2 * c
    d_e = w_gate.shape[-1]
    n_blk = p_total // MOE_ROWS
    s_in, s_args, s_out, s_shape = _side_cast_specs(side, n_blk, lambda i, *_: i, jnp.bfloat16)
    grid_spec = pltpu.PrefetchScalarGridSpec(
        num_scalar_prefetch=2,
        grid=(n_blk,),
        in_specs=[pl.BlockSpec((MOE_ROWS, c), lambda i, e, nv: (i, 0)),
                  pl.BlockSpec((1, d, d_e), lambda i, e, nv: (e[i], 0, 0)),
                  pl.BlockSpec((1, d, d_e), lambda i, e, nv: (e[i], 0, 0)),
                  pl.BlockSpec((1, d_e, d), lambda i, e, nv: (e[i], 0, 0))] + s_in,
        out_specs=[pl.BlockSpec((MOE_ROWS, c), lambda i, e, nv: (i, 0))] + s_out,
    )
    outs = pl.pallas_call(
        _expert_body,
        grid_spec=grid_spec,
        out_shape=[jax.ShapeDtypeStruct((p_total, c), xb.dtype)] + s_shape,
        compiler_params=_params(("arbitrary",)),
        name="moe_experts",
    )(blk_e, nvalid, xb, w_gate, w_up, w_down, *s_args)
    return outs[0], [o.reshape(stack.shape[1:]) for o, (stack, _) in zip(outs[1:], side)]


def _combine_body(s1_ref, s2_ref, s1n_ref, s2n_ref, x_ref, routed_ref, g_ref, yb_hbm, *refs, emit_sum):
    out_refs, ybuf, sem = refs[:-2], refs[-2], refs[-1]
    i = pl.program_id(0)
    n_steps = pl.num_programs(0)
    slot = i % 2
    rows = x_ref.shape[0]

    def gather(t1_ref, t2_ref, to_slot):
        def issue(r, carry):
            pltpu.make_async_copy(yb_hbm.at[pl.ds(t1_ref[0, 0, r], 1)], ybuf.at[to_slot, pl.ds(r, 1)],
                                  sem.at[to_slot]).start()
            pltpu.make_async_copy(yb_hbm.at[pl.ds(t2_ref[0, 0, r], 1)], ybuf.at[to_slot, pl.ds(rows + r, 1)],
                                  sem.at[to_slot]).start()
            return carry
        lax.fori_loop(0, rows, issue, 0, unroll=8)

    @pl.when(i == 0)
    def _():
        gather(s1_ref, s2_ref, 0)

    @pl.when(i + 1 < n_steps)
    def _():
        gather(s1n_ref, s2n_ref, 1 - slot)

    pltpu.make_async_copy(ybuf.at[slot], ybuf.at[slot], sem.at[slot]).wait()
    y1 = _unpack_halves(ybuf[slot, :rows])
    y2 = _unpack_halves(ybuf[slot, rows:])
    s = x_ref[...] + (y1 * routed_ref[:, 2:3] + y2 * routed_ref[:, 3:4])
    if emit_sum:
        out_refs[0][...] = s
    ms = jnp.mean(s * s, axis=-1, keepdims=True)
    out_refs[-1][...] = (s * lax.rsqrt(ms + EPS) * g_ref[...]).astype(out_refs[-1].dtype)


def _combine_norm(x, yb, routed, slot1, slot2, g, norm_dtype, emit_sum, rows=256):
    n, d = x.shape
    steps = n // rows
    row_spec = pl.BlockSpec((rows, d), lambda i: (i, 0))
    tab = pl.BlockSpec((1, 1, rows), lambda i: (i, 0, 0), memory_space=pltpu.SMEM)
    tab_next = pl.BlockSpec((1, 1, rows), lambda i: (jnp.minimum(i + 1, steps - 1), 0, 0),
                            memory_space=pltpu.SMEM)
    out_specs = [row_spec]
    out_shape = [jax.ShapeDtypeStruct((n, d), norm_dtype)]
    if emit_sum:
        out_specs = [row_spec] + out_specs
        out_shape = [jax.ShapeDtypeStruct((n, d), x.dtype)] + out_shape
    s1 = slot1.reshape(steps, 1, rows)
    s2 = slot2.reshape(steps, 1, rows)
    return pl.pallas_call(
        functools.partial(_combine_body, emit_sum=emit_sum),
        grid=(steps,),
        in_specs=[tab, tab, tab_next, tab_next, row_spec,
                  pl.BlockSpec((rows, LANES), lambda i: (i, 0)),
                  pl.BlockSpec((1, d), lambda i: (0, 0)),
                  pl.BlockSpec(memory_space=pl.ANY)],
        out_specs=out_specs,
        out_shape=out_shape,
        scratch_shapes=[pltpu.VMEM((2, TOP_K * rows, d // 2), yb.dtype),
                        pltpu.SemaphoreType.DMA((2,))],
        compiler_params=_params(("arbitrary",)),
        name="combine_norm",
    )(s1, s2, s1, s2, x, routed, g.reshape(1, d), yb)


def _moe_slot_rows(n):
    return n * TOP_K + N_EXPERTS * MOE_ROWS


def _moe_layer(x, ffn_g, next_g, norm_dtype, emit_sum, w_rg, b_rg, w_re, b_re, w_gate, w_up, w_down, xb_zero,
               side=()):
    n, d = x.shape
    routed, hp, counts = _router(x, ffn_g, w_rg, b_rg, w_re, b_re)

    n_blk = _moe_slot_rows(n) // MOE_ROWS
    experts = jnp.arange(N_EXPERTS, dtype=jnp.int32)
    counts = counts[0, :N_EXPERTS].astype(jnp.int32)
    padded = ((counts + MOE_ROWS - 1) // MOE_ROWS) * MOE_ROWS
    pends = jnp.cumsum(padded)
    pstarts = pends - padded
    expert_id = routed[:, 0:TOP_K].astype(jnp.int32)
    rank = routed[:, 2 * TOP_K:3 * TOP_K].astype(jnp.int32)
    slot = jnp.sum(jnp.where(expert_id[:, :, None] == experts, pstarts, 0), axis=-1) + rank
    blk_start = jnp.arange(n_blk, dtype=jnp.int32) * MOE_ROWS
    blk_e = jnp.minimum(jnp.sum(blk_start[:, None] >= pends[None, :], axis=1), N_EXPERTS - 1).astype(jnp.int32)
    seg_end = jnp.sum(jnp.where(blk_e[:, None] == experts, pstarts + counts, 0), axis=1)
    nvalid = jnp.clip(seg_end - blk_start, 0, MOE_ROWS).astype(jnp.int32)

    xb = _dispatch(hp, slot[:, 0], slot[:, 1], xb_zero)
    yb, cast = _moe_experts(xb, blk_e, nvalid, w_gate, w_up, w_down, side)
    return _combine_norm(x, yb, routed, slot[:, 0], slot[:, 1], next_g, norm_dtype, emit_sum), cast


def kernel(x, mix_norm, ffn_norm, sgu_w_in, sgu_b_in, sgu_v_gain, sgu_v_bias, sgu_w_spatial, sgu_b_spatial,
           sgu_w_out, dil_w_qkv, dil_w_out, rel_bias, router_w_group, router_b_group, router_w_expert,
           router_b_expert, moe_w_gate, moe_w_up, moe_w_down, final_norm):
    batch, seq, d = x.shape
    n = batch * seq
    bf16 = jnp.bfloat16
    xf = x.reshape(n, d)
    depth = mix_norm.shape[0]
    n_dil = len(DIL_CONFIGS)
    h = _rmsnorm(xf, mix_norm[0], bf16)
    out = None
    dense = {}
    for i in range(depth):
        j = i // 2
        xb_shape = (_moe_slot_rows(n), d // 2, jnp.uint32)
        if i % 2 == 0:
            w_in, w_out = dense.get(i, (sgu_w_in[j], sgu_w_out[j]))
            z, (w_gate, w_up), xb_zero = _matmul(h, w_in, bias=sgu_b_in[j], use_gelu=True, out_dtype=bf16,
                                                 side=((moe_w_gate, i), (moe_w_up, i)), zero_fill=xb_shape)
            y = _sgu_gate(z, sgu_v_gain[j], sgu_v_bias[j], sgu_w_spatial[j], sgu_b_spatial[j])
            xf, (w_down,) = _matmul(y, w_out, residual=xf, out_dtype=jnp.float32, side=((moe_w_down, i),))
        else:
            w_qkv, w_out = dense.get(i, (dil_w_qkv[j], dil_w_out[j]))
            outs, lses, cast = [], [], []
            tn = 512
            per_part = DIL_INNER // tn
            for g, (window, dilation) in enumerate(DIL_CONFIGS):
                res = _matmul(
                    h, w_qkv, out_dtype=bf16, n_out=3 * DIL_INNER, tn=tn,
                    col_block_map=lambda c, g=g: ((c // per_part) * n_dil + g) * per_part + c % per_part,
                    col_scale=(per_part, HEAD_DIM ** -0.5),
                    regroup=(batch, seq, dilation), side=(((moe_w_gate, moe_w_up, moe_w_down)[g], i),),
                    zero_fill=xb_shape if g == 0 else None)
                qkv, (w_cast,) = res[:2]
                if g == 0:
                    xb_zero = res[2]
                cast.append(w_cast)
                bias = _band_bias(rel_bias[:, g * DIL_HEADS:(g + 1) * DIL_HEADS], window, dilation)
                o, lse = _dilated_group(qkv, bias)
                outs.append(o)
                lses.append(lse)
            w_gate, w_up, w_down = cast
            merged = _merge_groups(outs, lses)
            xf = _matmul(merged, w_out, residual=xf, out_dtype=jnp.float32)
        last = i + 1 == depth
        side = ()
        if not last:
            j_next = (i + 1) // 2
            side = (((sgu_w_in, j_next), (sgu_w_out, j_next)) if (i + 1) % 2 == 0
                    else ((dil_w_qkv, j_next), (dil_w_out, j_next)))
        res, cast = _moe_layer(xf, ffn_norm[i], final_norm if last else mix_norm[i + 1],
                               x.dtype if last else bf16, not last,
                               router_w_group[i], router_b_group[i], router_w_expert[i], router_b_expert[i],
                               w_gate, w_up, w_down, xb_zero, side)
        if cast:
            dense[i + 1] = tuple(cast)
        if last:
            (out,) = res
        else:
            xf, h = res
    return out.reshape(batch, seq, d)
```

```python
import functools
import math

import jax
import jax.numpy as jnp
from jax import lax
from jax.experimental import pallas as pl
from jax.experimental.pallas import tpu as pltpu

EPS = 1e-6
SGU_CHUNK = 128
SGU_GROUP_DIM = 128
DIL_CONFIGS = ((128, 1), (512, 4), (2048, 16))
DIL_HEADS = 16
HEAD_DIM = 128
DIL_BLOCK = 128
DIL_INNER = DIL_HEADS * HEAD_DIM
NUM_BUCKETS = 32
MAX_DISTANCE = 2048
N_EXPERT_GROUPS = 4
EXPERTS_PER_GROUP = 8
N_EXPERTS = N_EXPERT_GROUPS * EXPERTS_PER_GROUP
TOP_K = 2
MASK_VALUE = -1e30

LANES = 128
MOE_ROWS = 256
VMEM_LIMIT = 52 * 1024 * 1024
W_RING_SLOTS = 3
W_RING_BYTES = 12 * 1024 * 1024


def _params(sem, vmem=VMEM_LIMIT):
    return pltpu.CompilerParams(dimension_semantics=sem, vmem_limit_bytes=vmem)


def _rmsnorm_body(x_ref, g_ref, o_ref):
    x = x_ref[...]
    ms = jnp.mean(x * x, axis=-1, keepdims=True)
    o_ref[...] = (x * lax.rsqrt(ms + EPS) * g_ref[...]).astype(o_ref.dtype)


def _rmsnorm(x, g, out_dtype, rows=512):
    n, d = x.shape
    return pl.pallas_call(
        _rmsnorm_body,
        grid=(n // rows,),
        in_specs=[pl.BlockSpec((rows, d), lambda i: (i, 0)),
                  pl.BlockSpec((1, d), lambda i: (0, 0))],
        out_specs=pl.BlockSpec((rows, d), lambda i: (i, 0)),
        out_shape=jax.ShapeDtypeStruct((n, d), out_dtype),
        compiler_params=_params(("parallel",)),
        name="rmsnorm",
    )(x, g.reshape(1, d))


def _regroup_pitch(dilation):
    return dilation + 4 if dilation % 8 == 0 else dilation


def _matmul_body(*refs, has_bias, use_gelu, has_residual, dilation, n_side, has_zero, col_scale, w_col_of):
    a_ref, w_ref = refs[0], refs[1]
    pos = 2
    if w_col_of is None:
        w = w_ref[...]
    else:
        ring, sem = refs[-2], refs[-1]
        nj = pl.num_programs(1)
        step = pl.program_id(0) * nj + pl.program_id(1)
        n_steps = pl.num_programs(0) * nj
        tn = ring.shape[2]

        def w_copy(of_step):
            col = pl.multiple_of(w_col_of(lax.rem(of_step, nj)) * tn, tn)
            slot = lax.rem(of_step, W_RING_SLOTS)
            return pltpu.make_async_copy(w_ref.at[:, pl.ds(col, tn)], ring.at[slot], sem.at[slot])

        @pl.when(step == 0)
        def _():
            for ahead in range(W_RING_SLOTS - 1):
                @pl.when(ahead < n_steps)
                def _():
                    w_copy(ahead).start()

        @pl.when(step + W_RING_SLOTS - 1 < n_steps)
        def _():
            w_copy(step + W_RING_SLOTS - 1).start()

        w_copy(step).wait()
        w = ring[lax.rem(step, W_RING_SLOTS)]
    acc = jnp.dot(a_ref[...], w.astype(a_ref.dtype), preferred_element_type=jnp.float32)
    if col_scale is not None:
        blocks, scale = col_scale
        acc = acc * jnp.where(pl.program_id(1) < blocks, jnp.float32(scale), jnp.float32(1.0))
    if has_bias:
        acc = acc + refs[pos][...]
        pos += 1
    if use_gelu:
        acc = jax.nn.gelu(acc)
    if has_residual:
        acc = refs[pos][...] + acc
        pos += 1
    side_in = refs[pos:pos + n_side]
    o_ref = refs[pos + n_side]
    side_out = refs[pos + n_side + 1:pos + 2 * n_side + 1]
    pos += 2 * n_side + 1
    zero_ref = None
    if has_zero:
        zero_ref = refs[pos]
        pos += 1
    if dilation == 1:
        o_ref[...] = acc.astype(o_ref.dtype).reshape(o_ref.shape)
    else:
        acc_ref = refs[pos]
        sub = acc.shape[0] // dilation
        pitch = _regroup_pitch(dilation)
        for panel in range(acc_ref.shape[0]):
            cols = slice(panel * LANES, (panel + 1) * LANES)
            if pitch == dilation:
                acc_ref[panel] = acc[:, cols]
            else:
                for l in range(sub):
                    acc_ref[panel, l * pitch:l * pitch + dilation, :] = acc[l * dilation:(l + 1) * dilation, cols]
            for r in range(dilation):
                o_ref[0, r, :, cols] = acc_ref[panel, pl.ds(r, sub, stride=pitch), :].astype(o_ref.dtype)
    for s_in, s_out in zip(side_in, side_out):
        s_out[...] = s_in[...].astype(s_out.dtype)
    if has_zero:
        zero_ref[...] = jnp.zeros_like(zero_ref)


def _matmul(a, w, *, bias=None, use_gelu=False, residual=None, out_dtype, col_block_map=None, n_out=None,
            regroup=None, side=(), zero_fill=None, col_scale=None, tm=1024, tn=512):
    m, k = a.shape
    n = w.shape[1] if n_out is None else n_out
    tm, tn = min(tm, m), min(tn, n)
    grid = (m // tm, n // tn)
    w_col = (lambda j: j) if col_block_map is None else col_block_map
    use_ring = W_RING_SLOTS * k * tn * w.dtype.itemsize <= W_RING_BYTES
    w_spec = pl.BlockSpec(memory_space=pl.ANY) if use_ring else pl.BlockSpec((k, tn), lambda i, j: (0, w_col(j)))
    in_specs = [pl.BlockSpec((tm, k), lambda i, j: (i, 0)), w_spec]
    args = [a, w]
    if bias is not None:
        in_specs.append(pl.BlockSpec((1, tn), lambda i, j: (0, j)))
        args.append(bias.reshape(1, n))
    if residual is not None:
        in_specs.append(pl.BlockSpec((tm, tn), lambda i, j: (i, j)))
        args.append(residual)
    scratch = []
    dilation = 1
    if regroup is None:
        out_specs = [pl.BlockSpec((tm, tn), lambda i, j: (i, j))]
        out_shape = [jax.ShapeDtypeStruct((m, n), out_dtype)]
    else:
        batch, seq, dilation = regroup
        tiles = seq // tm
        out_specs = [pl.BlockSpec((1, dilation, tm // dilation, tn), lambda i, j: (i // tiles, 0, i % tiles, j))]
        out_shape = [jax.ShapeDtypeStruct((batch, dilation, seq // dilation, n), out_dtype)]
        if dilation > 1:
            scratch = [pltpu.VMEM((tn // LANES, tm // dilation * _regroup_pitch(dilation), LANES), jnp.float32)]
    n_steps = grid[0] * grid[1]

    def step_of(i, j):
        return i * grid[1] + j
    s_in, s_args, s_out, s_shape = _side_cast_specs(side, n_steps, step_of, a.dtype)
    in_specs += s_in
    args += s_args
    out_specs += s_out
    out_shape += s_shape
    if zero_fill is not None:
        z_rows, z_cols, z_dtype = zero_fill
        zero_slabs = 1
        while zero_slabs * 2 <= n_steps and z_rows % (zero_slabs * 2) == 0 and (z_rows // (zero_slabs * 2)) % 8 == 0:
            zero_slabs *= 2
        out_specs.append(pl.BlockSpec((z_rows // zero_slabs, z_cols),
                                      lambda i, j: (jnp.minimum(step_of(i, j), zero_slabs - 1), 0)))
        out_shape.append(jax.ShapeDtypeStruct((z_rows, z_cols), z_dtype))
    if use_ring:
        scratch += [pltpu.VMEM((W_RING_SLOTS, k, tn), w.dtype), pltpu.SemaphoreType.DMA((W_RING_SLOTS,))]
    body = functools.partial(_matmul_body, has_bias=bias is not None, use_gelu=use_gelu,
                             has_residual=residual is not None, dilation=dilation, n_side=len(side),
                             has_zero=zero_fill is not None, col_scale=col_scale,
                             w_col_of=w_col if use_ring else None)
    riders = bool(side) or zero_fill is not None
    sequential = riders or use_ring
    outs = pl.pallas_call(
        body,
        grid=grid,
        in_specs=in_specs,
        out_specs=out_specs,
        out_shape=out_shape,
        scratch_shapes=scratch,
        compiler_params=_params(("arbitrary", "arbitrary") if sequential else ("parallel", "arbitrary")),
        name="matmul",
    )(*args)
    if not riders:
        return outs[0]
    cast = [o.reshape(stack.shape[1:]) for o, (stack, _) in zip(outs[1:], side)]
    return (outs[0], cast, outs[-1]) if zero_fill is not None else (outs[0], cast)


def _sgu_gate_body(u_ref, v_ref, gain_ref, vbias_ref, ws_ref, bs_ref, o_ref, *, chunks):
    v = v_ref[...].astype(jnp.float32)
    mu = jnp.mean(v, axis=-1, keepdims=True)
    vc = v - mu
    var = jnp.mean(vc * vc, axis=-1, keepdims=True)
    vn = (vc * lax.rsqrt(var + EPS) * gain_ref[...] + vbias_ref[...]).astype(jnp.bfloat16)
    n_groups = ws_ref.shape[0]
    t_idx = lax.broadcasted_iota(jnp.int32, (SGU_CHUNK, SGU_CHUNK), 0)
    s_idx = lax.broadcasted_iota(jnp.int32, (SGU_CHUNK, SGU_CHUNK), 1)
    causal = s_idx <= t_idx
    for g in range(n_groups):
        cols = slice(g * SGU_GROUP_DIM, (g + 1) * SGU_GROUP_DIM)
        w = jnp.where(causal, ws_ref[g], jnp.zeros((), ws_ref.dtype))
        for c in range(chunks):
            rows = slice(c * SGU_CHUNK, (c + 1) * SGU_CHUNK)
            vm = jnp.dot(w, vn[rows, cols], preferred_element_type=jnp.float32) + bs_ref[:, cols]
            o_ref[rows, cols] = (u_ref[rows, cols].astype(jnp.float32) * vm).astype(o_ref.dtype)


def _sgu_gate(z, v_gain, v_bias, w_s, b_s, chunks=2):
    n, two_w = z.shape
    width = two_w // 2
    rows = chunks * SGU_CHUNK
    n_groups = w_s.shape[0]
    b_full = jnp.repeat(b_s.T, SGU_GROUP_DIM, axis=1)
    body = functools.partial(_sgu_gate_body, chunks=chunks)
    return pl.pallas_call(
        body,
        grid=(n // rows,),
        in_specs=[pl.BlockSpec((rows, width), lambda i: (i, 0)),
                  pl.BlockSpec((rows, width), lambda i: (i, 1)),
                  pl.BlockSpec((1, width), lambda i: (0, 0)),
                  pl.BlockSpec((1, width), lambda i: (0, 0)),
                  pl.BlockSpec((n_groups, SGU_CHUNK, SGU_CHUNK), lambda i: (0, 0, 0)),
                  pl.BlockSpec((SGU_CHUNK, width), lambda i: (0, 0))],
        out_specs=pl.BlockSpec((rows, width), lambda i: (i, 0)),
        out_shape=jax.ShapeDtypeStruct((n, width), jnp.bfloat16),
        compiler_params=_params(("parallel",)),
        name="sgu_gate",
    )(z, z, v_gain.reshape(1, width), v_bias.reshape(1, width), w_s.astype(jnp.bfloat16), b_full)


def _t5_causal_bucket(dist):
    max_exact = NUM_BUCKETS // 2
    d = jnp.maximum(dist, 0)
    df = jnp.maximum(d, 1).astype(jnp.float32)
    large = max_exact + (jnp.log(df / max_exact) / math.log(MAX_DISTANCE / max_exact)
                         * (NUM_BUCKETS - max_exact)).astype(jnp.int32)
    large = jnp.minimum(large, NUM_BUCKETS - 1)
    return jnp.where(d < max_exact, d, large)


def _band_bias(bias_table, window, dilation):
    span = window // dilation
    qi = jnp.arange(DIL_BLOCK)[:, None]
    kj = jnp.arange(2 * DIL_BLOCK)[None, :]
    sub_dist = qi + DIL_BLOCK - kj
    band = (sub_dist >= 0) & (sub_dist <= span)
    bucket = _t5_causal_bucket(sub_dist * dilation)
    onehot = (bucket[:, :, None] == jnp.arange(NUM_BUCKETS)).astype(jnp.float32)
    bias = jnp.einsum('qkb,bh->hqk', onehot, bias_table.astype(jnp.float32), precision=lax.Precision.HIGHEST)
    bias = jnp.where(band[None], bias, MASK_VALUE)
    return jnp.stack([bias, jnp.where(kj[None] < DIL_BLOCK, MASK_VALUE, bias)])


def _attn_body(q_ref, kp_ref, kc_ref, vp_ref, vc_ref, bias_ref, o_ref, lse_ref):
    table = jnp.where(pl.program_id(2) == 0, 1, 0)
    lane = lax.broadcasted_iota(jnp.int32, (DIL_BLOCK, LANES), 1)
    lse_all = jnp.zeros((DIL_BLOCK, LANES), jnp.float32)
    for h in range(DIL_HEADS):
        cols = slice(h * HEAD_DIM, (h + 1) * HEAD_DIM)
        q = q_ref[0, 0, :, cols]
        k = jnp.concatenate([kp_ref[0, 0, :, cols], kc_ref[0, 0, :, cols]], axis=0)
        v = jnp.concatenate([vp_ref[0, 0, :, cols], vc_ref[0, 0, :, cols]], axis=0)
        s = lax.dot_general(q, k, (((1,), (1,)), ((), ())), preferred_element_type=jnp.float32)
        s = s + bias_ref[table, h]
        m = jnp.max(s, axis=-1, keepdims=True)
        p = jnp.exp(s - m)
        den = jnp.sum(p, axis=-1, keepdims=True)
        o = jnp.dot(p.astype(jnp.bfloat16), v, preferred_element_type=jnp.float32) / den
        o_ref[0, 0, :, cols] = o.astype(o_ref.dtype)
        lse_all = jnp.where(lane == h, m + jnp.log(den), lse_all)
    lse_ref[0, 0] = lse_all


def _dilated_group(qkv, band_bias):
    batch, dilation, sub_len, _ = qkv.shape
    nb = sub_len // DIL_BLOCK

    def spec(part, prev):
        def index_map(b, r, n):
            return (b, r, jnp.maximum(n - 1, 0) if prev else n, part)
        return pl.BlockSpec((1, 1, DIL_BLOCK, DIL_INNER), index_map)

    return pl.pallas_call(
        _attn_body,
        grid=(batch, dilation, nb),
        in_specs=[spec(0, False), spec(1, True), spec(1, False), spec(2, True), spec(2, False),
                  pl.BlockSpec((2, DIL_HEADS, DIL_BLOCK, 2 * DIL_BLOCK), lambda b, r, n: (0, 0, 0, 0))],
        out_specs=[pl.BlockSpec((1, 1, DIL_BLOCK, DIL_INNER), lambda b, r, n: (b, r, n, 0)),
                   pl.BlockSpec((1, 1, DIL_BLOCK, LANES), lambda b, r, n: (b, r, n, 0))],
        out_shape=[jax.ShapeDtypeStruct((batch, dilation, sub_len, DIL_INNER), jnp.bfloat16),
                   jax.ShapeDtypeStruct((batch, dilation, sub_len, LANES), jnp.float32)],
        compiler_params=_params(("parallel", "parallel", "arbitrary")),
        name="dilated_attn",
    )(qkv, qkv, qkv, qkv, qkv, band_bias)


def _merge_body(*refs, dilations):
    n_g = len(dilations)
    o_refs, l_refs = refs[:n_g], refs[n_g:2 * n_g]
    out_ref, ltok, wtok, acc = refs[2 * n_g:]
    rows = out_ref.shape[0]

    def residue_rows(r, d):
        return pl.ds(r, rows // d, stride=d) if d > 1 else slice(None)

    for g, d in enumerate(dilations):
        for r in range(d):
            ltok[g, residue_rows(r, d), :] = l_refs[g][0, r]
    lse = [ltok[g] for g in range(n_g)]
    m = functools.reduce(jnp.maximum, lse)
    e = [jnp.exp(l - m) for l in lse]
    tot = functools.reduce(jnp.add, e)
    for g in range(n_g):
        wtok[g] = e[g] / tot
    for g, d in enumerate(dilations):
        for r in range(d):
            rr = residue_rows(r, d)
            w = wtok[g, rr, :]
            for h in range(DIL_HEADS):
                cols = slice(h * HEAD_DIM, (h + 1) * HEAD_DIM)
                term = o_refs[g][0, r, :, cols].astype(jnp.float32) * w[:, h:h + 1]
                if g == 0:
                    acc[h, rr, :] = term
                else:
                    acc[h, rr, :] = acc[h, rr, :] + term
    for h in range(DIL_HEADS):
        out_ref[:, h * HEAD_DIM:(h + 1) * HEAD_DIM] = acc[h].astype(out_ref.dtype)


def _merge_groups(outs, lses, rows=512):
    batch = outs[0].shape[0]
    dilations = tuple(o.shape[1] for o in outs)
    seq = outs[0].shape[1] * outs[0].shape[2]
    tiles = seq // rows

    def spec(d, width):
        return pl.BlockSpec((1, d, rows // d, width), lambda i: (i // tiles, 0, i % tiles, 0))

    n_g = len(outs)
    return pl.pallas_call(
        functools.partial(_merge_body, dilations=dilations),
        grid=(batch * tiles,),
        in_specs=[spec(d, DIL_INNER) for d in dilations] + [spec(d, LANES) for d in dilations],
        out_specs=pl.BlockSpec((rows, DIL_INNER), lambda i: (i, 0)),
        out_shape=jax.ShapeDtypeStruct((batch * seq, DIL_INNER), jnp.bfloat16),
        scratch_shapes=[pltpu.VMEM((n_g, rows, LANES), jnp.float32),
                        pltpu.VMEM((n_g, rows, LANES), jnp.float32),
                        pltpu.VMEM((DIL_HEADS, rows, HEAD_DIM), jnp.float32)],
        compiler_params=_params(("parallel",)),
        name="merge_groups",
    )(*outs, *lses)


def _pack_halves(v):
    c = v.shape[1] // 2
    r = v.astype(jnp.bfloat16).astype(jnp.float32)
    lo = lax.bitcast_convert_type(r[:, :c], jnp.uint32)
    hi = lax.bitcast_convert_type(r[:, c:], jnp.uint32)
    return (lo >> 16) | (hi & jnp.uint32(0xFFFF0000))


def _unpack_halves(w):
    lo = lax.bitcast_convert_type(w << 16, jnp.float32)
    hi = lax.bitcast_convert_type(w & jnp.uint32(0xFFFF0000), jnp.float32)
    return jnp.concatenate([lo, hi], axis=1)


def _router_body(x_ref, g_ref, w_ref, b_ref, o_ref, hp_ref, cnt_ref, carry):
    i = pl.program_id(0)

    @pl.when(i == 0)
    def _():
        carry[...] = jnp.zeros_like(carry)

    x = x_ref[...]
    ms = jnp.mean(x * x, axis=-1, keepdims=True)
    hf = x * lax.rsqrt(ms + EPS) * g_ref[...]
    hp_ref[...] = _pack_halves(hf)
    logits = jnp.dot(hf.astype(jnp.bfloat16), w_ref[...], preferred_element_type=jnp.float32) + b_ref[...]
    rows = logits.shape[0]
    lane = lax.broadcasted_iota(jnp.int32, logits.shape, 1)
    neg = -jnp.inf
    gl = jnp.where(lane < N_EXPERT_GROUPS, logits, neg)
    gmax = jnp.max(gl, axis=-1, keepdims=True)
    g_idx = jnp.min(jnp.where(gl == gmax, lane, LANES), axis=-1, keepdims=True)
    g_w = 1.0 / jnp.sum(jnp.exp(gl - gmax), axis=-1, keepdims=True)
    lo = N_EXPERT_GROUPS + EXPERTS_PER_GROUP * g_idx
    el = jnp.where(jnp.logical_and(lane >= lo, lane < lo + EXPERTS_PER_GROUP), logits, neg)
    v1 = jnp.max(el, axis=-1, keepdims=True)
    i1 = jnp.min(jnp.where(el == v1, lane, LANES), axis=-1, keepdims=True)
    el2 = jnp.where(lane == i1, neg, el)
    v2 = jnp.max(el2, axis=-1, keepdims=True)
    i2 = jnp.min(jnp.where(el2 == v2, lane, LANES), axis=-1, keepdims=True)
    e2 = jnp.exp(v2 - v1)
    p1 = 1.0 / (1.0 + e2)
    p2 = e2 / (1.0 + e2)
    oh1 = lane == i1 - N_EXPERT_GROUPS
    oh2 = lane == i2 - N_EXPERT_GROUPS
    oh1f, oh2f = oh1.astype(jnp.float32), oh2.astype(jnp.float32)
    before = (lax.broadcasted_iota(jnp.int32, (rows, rows), 1)
              < lax.broadcasted_iota(jnp.int32, (rows, rows), 0)).astype(jnp.bfloat16)
    pre1 = jnp.dot(before, oh1f.astype(jnp.bfloat16), preferred_element_type=jnp.float32)
    pre2 = jnp.dot(before, oh2f.astype(jnp.bfloat16), preferred_element_type=jnp.float32)
    cnt1 = jnp.sum(oh1f, axis=0, keepdims=True)
    cnt2 = jnp.sum(oh2f, axis=0, keepdims=True)
    base = carry[...]
    rank1 = jnp.sum(jnp.where(oh1, pre1 + base, 0.0), axis=-1, keepdims=True)
    rank2 = jnp.sum(jnp.where(oh2, pre2 + (base + cnt1), 0.0), axis=-1, keepdims=True)
    total = base + cnt1 + cnt2
    carry[...] = total
    cnt_ref[...] = total
    out = jnp.where(lane == 0, (i1 - N_EXPERT_GROUPS).astype(jnp.float32),
          jnp.where(lane == 1, (i2 - N_EXPERT_GROUPS).astype(jnp.float32),
          jnp.where(lane == 2, g_w * p1,
          jnp.where(lane == 3, g_w * p2,
          jnp.where(lane == 4, rank1,
          jnp.where(lane == 5, rank2, 0.0))))))
    o_ref[...] = out


def _router(x, g, w_rg, b_rg, w_re, b_re, rows=512):
    n, d = x.shape
    pad = LANES - N_EXPERT_GROUPS - N_EXPERTS
    w = jnp.concatenate([w_rg, w_re, jnp.zeros((d, pad), w_rg.dtype)], axis=1).astype(jnp.bfloat16)
    b = jnp.concatenate([b_rg, b_re, jnp.zeros((pad,), b_rg.dtype)]).reshape(1, LANES)
    return pl.pallas_call(
        _router_body,
        grid=(n // rows,),
        in_specs=[pl.BlockSpec((rows, d), lambda i: (i, 0)),
                  pl.BlockSpec((1, d), lambda i: (0, 0)),
                  pl.BlockSpec((d, LANES), lambda i: (0, 0)),
                  pl.BlockSpec((1, LANES), lambda i: (0, 0))],
        out_specs=[pl.BlockSpec((rows, LANES), lambda i: (i, 0)),
                   pl.BlockSpec((rows, d // 2), lambda i: (i, 0)),
                   pl.BlockSpec((1, LANES), lambda i: (0, 0))],
        out_shape=[jax.ShapeDtypeStruct((n, LANES), jnp.float32),
                   jax.ShapeDtypeStruct((n, d // 2), jnp.uint32),
                   jax.ShapeDtypeStruct((1, LANES), jnp.float32)],
        scratch_shapes=[pltpu.VMEM((1, LANES), jnp.float32)],
        compiler_params=_params(("arbitrary",)),
        name="router",
    )(x, g.reshape(1, d), w, b)


def _dispatch_body(s1_ref, s2_ref, hp_ref, xb_in_hbm, xb_hbm, stage, sem):
    del xb_in_hbm
    i = pl.program_id(0)
    n_steps = pl.num_programs(0)
    rows = s1_ref.shape[2]
    slot = i % 2

    def wait_slot(of_slot):
        for _ in range(TOP_K):
            pltpu.make_async_copy(stage.at[of_slot], stage.at[of_slot], sem.at[of_slot]).wait()

    @pl.when(i >= 2)
    def _():
        wait_slot(slot)

    stage[slot] = hp_ref[...]

    def issue(r, carry):
        src = stage.at[slot, pl.ds(r, 1)]
        pltpu.make_async_copy(src, xb_hbm.at[pl.ds(s1_ref[0, 0, r], 1)], sem.at[slot]).start()
        pltpu.make_async_copy(src, xb_hbm.at[pl.ds(s2_ref[0, 0, r], 1)], sem.at[slot]).start()
        return carry
    lax.fori_loop(0, rows, issue, 0, unroll=8)

    @pl.when(i == n_steps - 1)
    def _():
        @pl.when(i >= 1)
        def _():
            wait_slot(1 - slot)
        wait_slot(slot)


def _dispatch(hp, slot1, slot2, xb_zero, rows=512):
    n, c = hp.shape
    p_total = xb_zero.shape[0]
    steps = n // rows
    tab = pl.BlockSpec((1, 1, rows), lambda i: (i, 0, 0), memory_space=pltpu.SMEM)
    any_spec = pl.BlockSpec(memory_space=pl.ANY)
    return pl.pallas_call(
        _dispatch_body,
        grid=(steps,),
        in_specs=[tab, tab, pl.BlockSpec((rows, c), lambda i: (i, 0)), any_spec],
        out_specs=any_spec,
        out_shape=jax.ShapeDtypeStruct((p_total, c), hp.dtype),
        scratch_shapes=[pltpu.VMEM((2, rows, c), hp.dtype),
                        pltpu.SemaphoreType.DMA((2,))],
        input_output_aliases={3: 0},
        compiler_params=_params(("arbitrary",)),
        name="moe_dispatch",
    )(slot1.reshape(steps, 1, rows), slot2.reshape(steps, 1, rows), hp, xb_zero)


def _side_cast_specs(side, n_steps, index_of_step, dtype):
    in_specs, args, out_specs, out_shape = [], [], [], []
    for stack, layer in side:
        cols = stack.shape[-1]
        flat = stack.reshape(-1, cols)
        layer_rows = flat.shape[0] // stack.shape[0]
        n_slabs = 1
        while (n_slabs * 2 <= n_steps and layer_rows % (n_slabs * 2) == 0
               and (layer_rows // (n_slabs * 2)) % 16 == 0):
            n_slabs *= 2
        slab_rows = layer_rows // n_slabs

        def slab(*idx, last=n_slabs - 1):
            return jnp.minimum(index_of_step(*idx), last)
        in_specs.append(pl.BlockSpec((slab_rows, cols),
                                     lambda *idx, slab=slab, first=layer * n_slabs: (first + slab(*idx), 0)))
        args.append(flat)
        out_specs.append(pl.BlockSpec((slab_rows, cols), lambda *idx, slab=slab: (slab(*idx), 0)))
        out_shape.append(jax.ShapeDtypeStruct((layer_rows, cols), dtype))
    return in_specs, args, out_specs, out_shape


def _expert_body(blk_e_ref, nvalid_ref, xb_ref, wg_ref, wu_ref, wd_ref, *refs):
    del blk_e_ref
    n_side = (len(refs) - 1) // 2
    side_in, yb_ref, side_out = refs[:n_side], refs[n_side], refs[n_side + 1:]
    i = pl.program_id(0)
    for s_in, s_out in zip(side_in, side_out):
        s_out[...] = s_in[...].astype(s_out.dtype)

    @pl.when(nvalid_ref[i] > 0)
    def _():
        h = _unpack_halves(xb_ref[...]).astype(jnp.bfloat16)
        a = jax.nn.silu(jnp.dot(h, wg_ref[0], preferred_element_type=jnp.float32))
        a = a * jnp.dot(h, wu_ref[0], preferred_element_type=jnp.float32)
        y = jnp.dot(a.astype(jnp.bfloat16), wd_ref[0], preferred_element_type=jnp.float32)
        yb_ref[...] = _pack_halves(y)

    @pl.when(nvalid_ref[i] == 0)
    def _():
        yb_ref[...] = jnp.zeros_like(yb_ref)


def _moe_experts(xb, blk_e, nvalid, w_gate, w_up, w_down, side=()):
    p_total, c = xb.shape
    d = 2 * c
    d_e = w_gate.shape[-1]
    n_blk = p_total // MOE_ROWS
    s_in, s_args, s_out, s_shape = _side_cast_specs(side, n_blk, lambda i, *_: i, jnp.bfloat16)
    grid_spec = pltpu.PrefetchScalarGridSpec(
        num_scalar_prefetch=2,
        grid=(n_blk,),
        in_specs=[pl.BlockSpec((MOE_ROWS, c), lambda i, e, nv: (i, 0)),
                  pl.BlockSpec((1, d, d_e), lambda i, e, nv: (e[i], 0, 0)),
                  pl.BlockSpec((1, d, d_e), lambda i, e, nv: (e[i], 0, 0)),
                  pl.BlockSpec((1, d_e, d), lambda i, e, nv: (e[i], 0, 0))] + s_in,
        out_specs=[pl.BlockSpec((MOE_ROWS, c), lambda i, e, nv: (i, 0))] + s_out,
    )
    outs = pl.pallas_call(
        _expert_body,
        grid_spec=grid_spec,
        out_shape=[jax.ShapeDtypeStruct((p_total, c), xb.dtype)] + s_shape,
        compiler_params=_params(("arbitrary",)),
        name="moe_experts",
    )(blk_e, nvalid, xb, w_gate, w_up, w_down, *s_args)
    return outs[0], [o.reshape(stack.shape[1:]) for o, (stack, _) in zip(outs[1:], side)]


def _combine_body(s1_ref, s2_ref, s1n_ref, s2n_ref, x_ref, routed_ref, g_ref, yb_hbm, *refs, emit_sum):
    out_refs, ybuf, sem = refs[:-2], refs[-2], refs[-1]
    i = pl.program_id(0)
    n_steps = pl.num_programs(0)
    slot = i % 2
    rows = x_ref.shape[0]

    def gather(t1_ref, t2_ref, to_slot):
        def issue(r, carry):
            pltpu.make_async_copy(yb_hbm.at[pl.ds(t1_ref[0, 0, r], 1)], ybuf.at[to_slot, pl.ds(r, 1)],
                                  sem.at[to_slot]).start()
            pltpu.make_async_copy(yb_hbm.at[pl.ds(t2_ref[0, 0, r], 1)], ybuf.at[to_slot, pl.ds(rows + r, 1)],
                                  sem.at[to_slot]).start()
            return carry
        lax.fori_loop(0, rows, issue, 0, unroll=8)

    @pl.when(i == 0)
    def _():
        gather(s1_ref, s2_ref, 0)

    @pl.when(i + 1 < n_steps)
    def _():
        gather(s1n_ref, s2n_ref, 1 - slot)

    pltpu.make_async_copy(ybuf.at[slot], ybuf.at[slot], sem.at[slot]).wait()
    y1 = _unpack_halves(ybuf[slot, :rows])
    y2 = _unpack_halves(ybuf[slot, rows:])
    s = x_ref[...] + (y1 * routed_ref[:, 2:3] + y2 * routed_ref[:, 3:4])
    if emit_sum:
        out_refs[0][...] = s
    ms = jnp.mean(s * s, axis=-1, keepdims=True)
    out_refs[-1][...] = (s * lax.rsqrt(ms + EPS) * g_ref[...]).astype(out_refs[-1].dtype)


def _combine_norm(x, yb, routed, slot1, slot2, g, norm_dtype, emit_sum, rows=256):
    n, d = x.shape
    steps = n // rows
    row_spec = pl.BlockSpec((rows, d), lambda i: (i, 0))
    tab = pl.BlockSpec((1, 1, rows), lambda i: (i, 0, 0), memory_space=pltpu.SMEM)
    tab_next = pl.BlockSpec((1, 1, rows), lambda i: (jnp.minimum(i + 1, steps - 1), 0, 0),
                            memory_space=pltpu.SMEM)
    out_specs = [row_spec]
    out_shape = [jax.ShapeDtypeStruct((n, d), norm_dtype)]
    if emit_sum:
        out_specs = [row_spec] + out_specs
        out_shape = [jax.ShapeDtypeStruct((n, d), x.dtype)] + out_shape
    s1 = slot1.reshape(steps, 1, rows)
    s2 = slot2.reshape(steps, 1, rows)
    return pl.pallas_call(
        functools.partial(_combine_body, emit_sum=emit_sum),
        grid=(steps,),
        in_specs=[tab, tab, tab_next, tab_next, row_spec,
                  pl.BlockSpec((rows, LANES), lambda i: (i, 0)),
                  pl.BlockSpec((1, d), lambda i: (0, 0)),
                  pl.BlockSpec(memory_space=pl.ANY)],
        out_specs=out_specs,
        out_shape=out_shape,
        scratch_shapes=[pltpu.VMEM((2, TOP_K * rows, d // 2), yb.dtype),
                        pltpu.SemaphoreType.DMA((2,))],
        compiler_params=_params(("arbitrary",)),
        name="combine_norm",
    )(s1, s2, s1, s2, x, routed, g.reshape(1, d), yb)


def _moe_slot_rows(n):
    return n * TOP_K + N_EXPERTS * MOE_ROWS


def _moe_layer(x, ffn_g, next_g, norm_dtype, emit_sum, w_rg, b_rg, w_re, b_re, w_gate, w_up, w_down, xb_zero,
               side=()):
    n, d = x.shape
    routed, hp, counts = _router(x, ffn_g, w_rg, b_rg, w_re, b_re)

    n_blk = _moe_slot_rows(n) // MOE_ROWS
    experts = jnp.arange(N_EXPERTS, dtype=jnp.int32)
    counts = counts[0, :N_EXPERTS].astype(jnp.int32)
    padded = ((counts + MOE_ROWS - 1) // MOE_ROWS) * MOE_ROWS
    pends = jnp.cumsum(padded)
    pstarts = pends - padded
    expert_id = routed[:, 0:TOP_K].astype(jnp.int32)
    rank = routed[:, 2 * TOP_K:3 * TOP_K].astype(jnp.int32)
    slot = jnp.sum(jnp.where(expert_id[:, :, None] == experts, pstarts, 0), axis=-1) + rank
    blk_start = jnp.arange(n_blk, dtype=jnp.int32) * MOE_ROWS
    blk_e = jnp.minimum(jnp.sum(blk_start[:, None] >= pends[None, :], axis=1), N_EXPERTS - 1).astype(jnp.int32)
    seg_end = jnp.sum(jnp.where(blk_e[:, None] == experts, pstarts + counts, 0), axis=1)
    nvalid = jnp.clip(seg_end - blk_start, 0, MOE_ROWS).astype(jnp.int32)

    xb = _dispatch(hp, slot[:, 0], slot[:, 1], xb_zero)
    yb, cast = _moe_experts(xb, blk_e, nvalid, w_gate, w_up, w_down, side)
    return _combine_norm(x, yb, routed, slot[:, 0], slot[:, 1], next_g, norm_dtype, emit_sum), cast


def kernel(x, mix_norm, ffn_norm, sgu_w_in, sgu_b_in, sgu_v_gain, sgu_v_bias, sgu_w_spatial, sgu_b_spatial,
           sgu_w_out, dil_w_qkv, dil_w_out, rel_bias, router_w_group, router_b_group, router_w_expert,
           router_b_expert, moe_w_gate, moe_w_up, moe_w_down, final_norm):
    batch, seq, d = x.shape
    n = batch * seq
    bf16 = jnp.bfloat16
    xf = x.reshape(n, d)
    depth = mix_norm.shape[0]
    n_dil = len(DIL_CONFIGS)
    h = _rmsnorm(xf, mix_norm[0], bf16)
    out = None
    dense = {}
    for i in range(depth):
        j = i // 2
        xb_shape = (_moe_slot_rows(n), d // 2, jnp.uint32)
        if i % 2 == 0:
            w_in, w_out = dense.get(i, (sgu_w_in[j], sgu_w_out[j]))
            z, (w_gate, w_up), xb_zero = _matmul(h, w_in, bias=sgu_b_in[j], use_gelu=True, out_dtype=bf16,
                                                 side=((moe_w_gate, i), (moe_w_up, i)), zero_fill=xb_shape)
            y = _sgu_gate(z, sgu_v_gain[j], sgu_v_bias[j], sgu_w_spatial[j], sgu_b_spatial[j])
            xf, (w_down,) = _matmul(y, w_out, residual=xf, out_dtype=jnp.float32, side=((moe_w_down, i),))
        else:
            w_qkv, w_out = dense.get(i, (dil_w_qkv[j], dil_w_out[j]))
            outs, lses, cast = [], [], []
            tn = 512
            per_part = DIL_INNER // tn
            for g, (window, dilation) in enumerate(DIL_CONFIGS):
                res = _matmul(
                    h, w_qkv, out_dtype=bf16, n_out=3 * DIL_INNER, tn=tn,
                    col_block_map=lambda c, g=g: ((c // per_part) * n_dil + g) * per_part + c % per_part,
                    col_scale=(per_part, HEAD_DIM ** -0.5),
                    regroup=(batch, seq, dilation), side=(((moe_w_gate, moe_w_up, moe_w_down)[g], i),),
                    zero_fill=xb_shape if g == 0 else None)
                qkv, (w_cast,) = res[:2]
                if g == 0:
                    xb_zero = res[2]
                cast.append(w_cast)
                bias = _band_bias(rel_bias[:, g * DIL_HEADS:(g + 1) * DIL_HEADS], window, dilation)
                o, lse = _dilated_group(qkv, bias)
                outs.append(o)
                lses.append(lse)
            w_gate, w_up, w_down = cast
            merged = _merge_groups(outs, lses)
            xf = _matmul(merged, w_out, residual=xf, out_dtype=jnp.float32)
        last = i + 1 == depth
        side = ()
        if not last:
            j_next = (i + 1) // 2
            side = (((sgu_w_in, j_next), (sgu_w_out, j_next)) if (i + 1) % 2 == 0
                    else ((dil_w_qkv, j_next), (dil_w_out, j_next)))
        res, cast = _moe_layer(xf, ffn_norm[i], final_norm if last else mix_norm[i + 1],
                               x.dtype if last else bf16, not last,
                               router_w_group[i], router_b_group[i], router_w_expert[i], router_b_expert[i],
                               w_gate, w_up, w_down, xb_zero, side)
        if cast:
            dense[i + 1] = tuple(cast)
        if last:
            (out,) = res
        else:
            xf, h = res
    return out.reshape(batch, seq, d)
```

```python
import functools
import math

import jax
import jax.numpy as jnp
from jax import lax
from jax.experimental import pallas as pl
from jax.experimental.pallas import tpu as pltpu

EPS = 1e-6
SGU_CHUNK = 128
SGU_GROUP_DIM = 128
DIL_CONFIGS = ((128, 1), (512, 4), (2048, 16))
DIL_HEADS = 16
HEAD_DIM = 128
DIL_BLOCK = 128
DIL_INNER = DIL_HEADS * HEAD_DIM
NUM_BUCKETS = 32
MAX_DISTANCE = 2048
N_EXPERT_GROUPS = 4
EXPERTS_PER_GROUP = 8
N_EXPERTS = N_EXPERT_GROUPS * EXPERTS_PER_GROUP
TOP_K = 2
MASK_VALUE = -1e30

LANES = 128
MOE_ROWS = 256
VMEM_LIMIT = 52 * 1024 * 1024
W_RING_SLOTS = 3
W_RING_BYTES = 12 * 1024 * 1024


def _params(sem, vmem=VMEM_LIMIT):
    return pltpu.CompilerParams(dimension_semantics=sem, vmem_limit_bytes=vmem)


def _rmsnorm_body(x_ref, g_ref, o_ref):
    x = x_ref[...]
    ms = jnp.mean(x * x, axis=-1, keepdims=True)
    o_ref[...] = (x * lax.rsqrt(ms + EPS) * g_ref[...]).astype(o_ref.dtype)


def _rmsnorm(x, g, out_dtype, rows=512):
    n, d = x.shape
    return pl.pallas_call(
        _rmsnorm_body,
        grid=(n // rows,),
        in_specs=[pl.BlockSpec((rows, d), lambda i: (i, 0)),
                  pl.BlockSpec((1, d), lambda i: (0, 0))],
        out_specs=pl.BlockSpec((rows, d), lambda i: (i, 0)),
        out_shape=jax.ShapeDtypeStruct((n, d), out_dtype),
        compiler_params=_params(("parallel",)),
        name="rmsnorm",
    )(x, g.reshape(1, d))


def _regroup_pitch(dilation):
    return dilation + 4 if dilation % 8 == 0 else dilation


def _matmul_body(*refs, has_bias, use_gelu, has_residual, dilation, n_side, has_zero, col_scale, w_col_of):
    a_ref, w_ref = refs[0], refs[1]
    pos = 2
    if w_col_of is None:
        w = w_ref[...]
    else:
        ring, sem = refs[-2], refs[-1]
        nj = pl.num_programs(1)
        step = pl.program_id(0) * nj + pl.program_id(1)
        n_steps = pl.num_programs(0) * nj
        tn = ring.shape[2]

        def w_copy(of_step):
            col = pl.multiple_of(w_col_of(lax.rem(of_step, nj)) * tn, tn)
            slot = lax.rem(of_step, W_RING_SLOTS)
            return pltpu.make_async_copy(w_ref.at[:, pl.ds(col, tn)], ring.at[slot], sem.at[slot])

        @pl.when(step == 0)
        def _():
            for ahead in range(W_RING_SLOTS - 1):
                @pl.when(ahead < n_steps)
                def _():
                    w_copy(ahead).start()

        @pl.when(step + W_RING_SLOTS - 1 < n_steps)
        def _():
            w_copy(step + W_RING_SLOTS - 1).start()

        w_copy(step).wait()
        w = ring[lax.rem(step, W_RING_SLOTS)]
    acc = jnp.dot(a_ref[...], w.astype(a_ref.dtype), preferred_element_type=jnp.float32)
    if col_scale is not None:
        blocks, scale = col_scale
        acc = acc * jnp.where(pl.program_id(1) < blocks, jnp.float32(scale), jnp.float32(1.0))
    if has_bias:
        acc = acc + refs[pos][...]
        pos += 1
    if use_gelu:
        acc = jax.nn.gelu(acc)
    if has_residual:
        acc = refs[pos][...] + acc
        pos += 1
    side_in = refs[pos:pos + n_side]
    o_ref = refs[pos + n_side]
    side_out = refs[pos + n_side + 1:pos + 2 * n_side + 1]
    pos += 2 * n_side + 1
    zero_ref = None
    if has_zero:
        zero_ref = refs[pos]
        pos += 1
    if dilation == 1:
        o_ref[...] = acc.astype(o_ref.dtype).reshape(o_ref.shape)
    else:
        acc_ref = refs[pos]
        sub = acc.shape[0] // dilation
        pitch = _regroup_pitch(dilation)
        for panel in range(acc_ref.shape[0]):
            cols = slice(panel * LANES, (panel + 1) * LANES)
            if pitch == dilation:
                acc_ref[panel] = acc[:, cols]
            else:
                for l in range(sub):
                    acc_ref[panel, l * pitch:l * pitch + dilation, :] = acc[l * dilation:(l + 1) * dilation, cols]
            for r in range(dilation):
                o_ref[0, r, :, cols] = acc_ref[panel, pl.ds(r, sub, stride=pitch), :].astype(o_ref.dtype)
    for s_in, s_out in zip(side_in, side_out):
        s_out[...] = s_in[...].astype(s_out.dtype)
    if has_zero:
        zero_ref[...] = jnp.zeros_like(zero_ref)


def _matmul(a, w, *, bias=None, use_gelu=False, residual=None, out_dtype, col_block_map=None, n_out=None,
            regroup=None, side=(), zero_fill=None, col_scale=None, tm=1024, tn=512):
    m, k = a.shape
    n = w.shape[1] if n_out is None else n_out
    tm, tn = min(tm, m), min(tn, n)
    grid = (m // tm, n // tn)
    w_col = (lambda j: j) if col_block_map is None else col_block_map
    use_ring = W_RING_SLOTS * k * tn * w.dtype.itemsize <= W_RING_BYTES
    w_spec = pl.BlockSpec(memory_space=pl.ANY) if use_ring else pl.BlockSpec((k, tn), lambda i, j: (0, w_col(j)))
    in_specs = [pl.BlockSpec((tm, k), lambda i, j: (i, 0)), w_spec]
    args = [a, w]
    if bias is not None:
        in_specs.append(pl.BlockSpec((1, tn), lambda i, j: (0, j)))
        args.append(bias.reshape(1, n))
    if residual is not None:
        in_specs.append(pl.BlockSpec((tm, tn), lambda i, j: (i, j)))
        args.append(residual)
    scratch = []
    dilation = 1
    if regroup is None:
        out_specs = [pl.BlockSpec((tm, tn), lambda i, j: (i, j))]
        out_shape = [jax.ShapeDtypeStruct((m, n), out_dtype)]
    else:
        batch, seq, dilation = regroup
        tiles = seq // tm
        out_specs = [pl.BlockSpec((1, dilation, tm // dilation, tn), lambda i, j: (i // tiles, 0, i % tiles, j))]
        out_shape = [jax.ShapeDtypeStruct((batch, dilation, seq // dilation, n), out_dtype)]
        if dilation > 1:
            scratch = [pltpu.VMEM((tn // LANES, tm // dilation * _regroup_pitch(dilation), LANES), jnp.float32)]
    n_steps = grid[0] * grid[1]

    def step_of(i, j):
        return i * grid[1] + j
    s_in, s_args, s_out, s_shape = _side_cast_specs(side, n_steps, step_of, a.dtype)
    in_specs += s_in
    args += s_args
    out_specs += s_out
    out_shape += s_shape
    if zero_fill is not None:
        z_rows, z_cols, z_dtype = zero_fill
        zero_slabs = 1
        while zero_slabs * 2 <= n_steps and z_rows % (zero_slabs * 2) == 0 and (z_rows // (zero_slabs * 2)) % 8 == 0:
            zero_slabs *= 2
        out_specs.append(pl.BlockSpec((z_rows // zero_slabs, z_cols),
                                      lambda i, j: (jnp.minimum(step_of(i, j), zero_slabs - 1), 0)))
        out_shape.append(jax.ShapeDtypeStruct((z_rows, z_cols), z_dtype))
    if use_ring:
        scratch += [pltpu.VMEM((W_RING_SLOTS, k, tn), w.dtype), pltpu.SemaphoreType.DMA((W_RING_SLOTS,))]
    body = functools.partial(_matmul_body, has_bias=bias is not None, use_gelu=use_gelu,
                             has_residual=residual is not None, dilation=dilation, n_side=len(side),
                             has_zero=zero_fill is not None, col_scale=col_scale,
                             w_col_of=w_col if use_ring else None)
    riders = bool(side) or zero_fill is not None
    sequential = riders or use_ring
    outs = pl.pallas_call(
        body,
        grid=grid,
        in_specs=in_specs,
        out_specs=out_specs,
        out_shape=out_shape,
        scratch_shapes=scratch,
        compiler_params=_params(("arbitrary", "arbitrary") if sequential else ("parallel", "arbitrary")),
        name="matmul",
    )(*args)
    if not riders:
        return outs[0]
    cast = [o.reshape(stack.shape[1:]) for o, (stack, _) in zip(outs[1:], side)]
    return (outs[0], cast, outs[-1]) if zero_fill is not None else (outs[0], cast)


def _sgu_gate_body(u_ref, v_ref, gain_ref, vbias_ref, ws_ref, bs_ref, o_ref, *, chunks):
    v = v_ref[...].astype(jnp.float32)
    mu = jnp.mean(v, axis=-1, keepdims=True)
    vc = v - mu
    var = jnp.mean(vc * vc, axis=-1, keepdims=True)
    vn = (vc * lax.rsqrt(var + EPS) * gain_ref[...] + vbias_ref[...]).astype(jnp.bfloat16)
    n_groups = ws_ref.shape[0]
    t_idx = lax.broadcasted_iota(jnp.int32, (SGU_CHUNK, SGU_CHUNK), 0)
    s_idx = lax.broadcasted_iota(jnp.int32, (SGU_CHUNK, SGU_CHUNK), 1)
    causal = s_idx <= t_idx
    for g in range(n_groups):
        cols = slice(g * SGU_GROUP_DIM, (g + 1) * SGU_GROUP_DIM)
        w = jnp.where(causal, ws_ref[g], jnp.zeros((), ws_ref.dtype))
        for c in range(chunks):
            rows = slice(c * SGU_CHUNK, (c + 1) * SGU_CHUNK)
            vm = jnp.dot(w, vn[rows, cols], preferred_element_type=jnp.float32) + bs_ref[:, cols]
            o_ref[rows, cols] = (u_ref[rows, cols].astype(jnp.float32) * vm).astype(o_ref.dtype)


def _sgu_gate(z, v_gain, v_bias, w_s, b_s, chunks=2):
    n, two_w = z.shape
    width = two_w // 2
    rows = chunks * SGU_CHUNK
    n_groups = w_s.shape[0]
    b_full = jnp.repeat(b_s.T, SGU_GROUP_DIM, axis=1)
    body = functools.partial(_sgu_gate_body, chunks=chunks)
    return pl.pallas_call(
        body,
        grid=(n // rows,),
        in_specs=[pl.BlockSpec((rows, width), lambda i: (i, 0)),
                  pl.BlockSpec((rows, width), lambda i: (i, 1)),
                  pl.BlockSpec((1, width), lambda i: (0, 0)),
                  pl.BlockSpec((1, width), lambda i: (0, 0)),
                  pl.BlockSpec((n_groups, SGU_CHUNK, SGU_CHUNK), lambda i: (0, 0, 0)),
                  pl.BlockSpec((SGU_CHUNK, width), lambda i: (0, 0))],
        out_specs=pl.BlockSpec((rows, width), lambda i: (i, 0)),
        out_shape=jax.ShapeDtypeStruct((n, width), jnp.bfloat16),
        compiler_params=_params(("parallel",)),
        name="sgu_gate",
    )(z, z, v_gain.reshape(1, width), v_bias.reshape(1, width), w_s.astype(jnp.bfloat16), b_full)


def _t5_causal_bucket(dist):
    max_exact = NUM_BUCKETS // 2
    d = jnp.maximum(dist, 0)
    df = jnp.maximum(d, 1).astype(jnp.float32)
    large = max_exact + (jnp.log(df / max_exact) / math.log(MAX_DISTANCE / max_exact)
                         * (NUM_BUCKETS - max_exact)).astype(jnp.int32)
    large = jnp.minimum(large, NUM_BUCKETS - 1)
    return jnp.where(d < max_exact, d, large)


def _band_bias(bias_table, window, dilation):
    span = window // dilation
    qi = jnp.arange(DIL_BLOCK)[:, None]
    kj = jnp.arange(2 * DIL_BLOCK)[None, :]
    sub_dist = qi + DIL_BLOCK - kj
    band = (sub_dist >= 0) & (sub_dist <= span)
    bucket = _t5_causal_bucket(sub_dist * dilation)
    onehot = (bucket[:, :, None] == jnp.arange(NUM_BUCKETS)).astype(jnp.float32)
    bias = jnp.einsum('qkb,bh->hqk', onehot, bias_table.astype(jnp.float32), precision=lax.Precision.HIGHEST)
    bias = jnp.where(band[None], bias, MASK_VALUE)
    return jnp.stack([bias, jnp.where(kj[None] < DIL_BLOCK, MASK_VALUE, bias)])


def _attn_body(q_ref, kp_ref, kc_ref, vp_ref, vc_ref, bias_ref, o_ref, lse_ref):
    lane = lax.broadcasted_iota(jnp.int32, (DIL_BLOCK, LANES), 1)
    for qb in range(q_ref.shape[2] // DIL_BLOCK):
        rows = slice(qb * DIL_BLOCK, (qb + 1) * DIL_BLOCK)
        before = slice((qb - 1) * DIL_BLOCK, qb * DIL_BLOCK)
        table = jnp.where(pl.program_id(2) == 0, 1, 0) if qb == 0 else 0
        lse_all = jnp.zeros((DIL_BLOCK, LANES), jnp.float32)
        for h in range(DIL_HEADS):
            cols = slice(h * HEAD_DIM, (h + 1) * HEAD_DIM)
            q = q_ref[0, 0, rows, cols]
            k_prev = kp_ref[0, 0, :, cols] if qb == 0 else kc_ref[0, 0, before, cols]
            v_prev = vp_ref[0, 0, :, cols] if qb == 0 else vc_ref[0, 0, before, cols]
            k = jnp.concatenate([k_prev, kc_ref[0, 0, rows, cols]], axis=0)
            v = jnp.concatenate([v_prev, vc_ref[0, 0, rows, cols]], axis=0)
            s = lax.dot_general(q, k, (((1,), (1,)), ((), ())), preferred_element_type=jnp.float32)
            s = s + bias_ref[table, h]
            m = jnp.max(s, axis=-1, keepdims=True)
            p = jnp.exp(s - m)
            den = jnp.sum(p, axis=-1, keepdims=True)
            o = jnp.dot(p.astype(jnp.bfloat16), v, preferred_element_type=jnp.float32) / den
            o_ref[0, 0, rows, cols] = o.astype(o_ref.dtype)
            lse_all = jnp.where(lane == h, m + jnp.log(den), lse_all)
        lse_ref[0, 0, rows, :] = lse_all


def _dilated_group(qkv, band_bias):
    batch, dilation, sub_len, _ = qkv.shape
    nb = sub_len // DIL_BLOCK
    per_step = 2 if nb % 2 == 0 else 1
    rows = per_step * DIL_BLOCK

    def spec(part, prev):
        if prev:
            return pl.BlockSpec((1, 1, DIL_BLOCK, DIL_INNER),
                                lambda b, r, n: (b, r, jnp.maximum(per_step * n - 1, 0), part))
        return pl.BlockSpec((1, 1, rows, DIL_INNER), lambda b, r, n: (b, r, n, part))

    return pl.pallas_call(
        _attn_body,
        grid=(batch, dilation, nb // per_step),
        in_specs=[spec(0, False), spec(1, True), spec(1, False), spec(2, True), spec(2, False),
                  pl.BlockSpec((2, DIL_HEADS, DIL_BLOCK, 2 * DIL_BLOCK), lambda b, r, n: (0, 0, 0, 0))],
        out_specs=[pl.BlockSpec((1, 1, rows, DIL_INNER), lambda b, r, n: (b, r, n, 0)),
                   pl.BlockSpec((1, 1, rows, LANES), lambda b, r, n: (b, r, n, 0))],
        out_shape=[jax.ShapeDtypeStruct((batch, dilation, sub_len, DIL_INNER), jnp.bfloat16),
                   jax.ShapeDtypeStruct((batch, dilation, sub_len, LANES), jnp.float32)],
        compiler_params=_params(("parallel", "parallel", "arbitrary")),
        name="dilated_attn",
    )(qkv, qkv, qkv, qkv, qkv, band_bias)


def _merge_body(*refs, dilations):
    n_g = len(dilations)
    o_refs, l_refs = refs[:n_g], refs[n_g:2 * n_g]
    out_ref, ltok, wtok, acc = refs[2 * n_g:]
    rows = out_ref.shape[0]

    def residue_rows(r, d):
        return pl.ds(r, rows // d, stride=d) if d > 1 else slice(None)

    for g, d in enumerate(dilations):
        for r in range(d):
            ltok[g, residue_rows(r, d), :] = l_refs[g][0, r]
    lse = [ltok[g] for g in range(n_g)]
    m = functools.reduce(jnp.maximum, lse)
    e = [jnp.exp(l - m) for l in lse]
    tot = functools.reduce(jnp.add, e)
    for g in range(n_g):
        wtok[g] = e[g] / tot
    for g, d in enumerate(dilations):
        for r in range(d):
            rr = residue_rows(r, d)
            w = wtok[g, rr, :]
            for h in range(DIL_HEADS):
                cols = slice(h * HEAD_DIM, (h + 1) * HEAD_DIM)
                term = o_refs[g][0, r, :, cols].astype(jnp.float32) * w[:, h:h + 1]
                if g == 0:
                    acc[h, rr, :] = term
                else:
                    acc[h, rr, :] = acc[h, rr, :] + term
    for h in range(DIL_HEADS):
        out_ref[:, h * HEAD_DIM:(h + 1) * HEAD_DIM] = acc[h].astype(out_ref.dtype)


def _merge_groups(outs, lses, rows=512):
    batch = outs[0].shape[0]
    dilations = tuple(o.shape[1] for o in outs)
    seq = outs[0].shape[1] * outs[0].shape[2]
    tiles = seq // rows

    def spec(d, width):
        return pl.BlockSpec((1, d, rows // d, width), lambda i: (i // tiles, 0, i % tiles, 0))

    n_g = len(outs)
    return pl.pallas_call(
        functools.partial(_merge_body, dilations=dilations),
        grid=(batch * tiles,),
        in_specs=[spec(d, DIL_INNER) for d in dilations] + [spec(d, LANES) for d in dilations],
        out_specs=pl.BlockSpec((rows, DIL_INNER), lambda i: (i, 0)),
        out_shape=jax.ShapeDtypeStruct((batch * seq, DIL_INNER), jnp.bfloat16),
        scratch_shapes=[pltpu.VMEM((n_g, rows, LANES), jnp.float32),
                        pltpu.VMEM((n_g, rows, LANES), jnp.float32),
                        pltpu.VMEM((DIL_HEADS, rows, HEAD_DIM), jnp.float32)],
        compiler_params=_params(("parallel",)),
        name="merge_groups",
    )(*outs, *lses)


def _pack_halves(v):
    c = v.shape[1] // 2
    r = v.astype(jnp.bfloat16).astype(jnp.float32)
    lo = lax.bitcast_convert_type(r[:, :c], jnp.uint32)
    hi = lax.bitcast_convert_type(r[:, c:], jnp.uint32)
    return (lo >> 16) | (hi & jnp.uint32(0xFFFF0000))


def _unpack_halves(w):
    lo = lax.bitcast_convert_type(w << 16, jnp.float32)
    hi = lax.bitcast_convert_type(w & jnp.uint32(0xFFFF0000), jnp.float32)
    return jnp.concatenate([lo, hi], axis=1)


def _router_body(x_ref, g_ref, w_ref, b_ref, o_ref, hp_ref, cnt_ref, carry):
    i = pl.program_id(0)

    @pl.when(i == 0)
    def _():
        carry[...] = jnp.zeros_like(carry)

    x = x_ref[...]
    ms = jnp.mean(x * x, axis=-1, keepdims=True)
    hf = x * lax.rsqrt(ms + EPS) * g_ref[...]
    hp_ref[...] = _pack_halves(hf)
    logits = jnp.dot(hf.astype(jnp.bfloat16), w_ref[...], preferred_element_type=jnp.float32) + b_ref[...]
    rows = logits.shape[0]
    lane = lax.broadcasted_iota(jnp.int32, logits.shape, 1)
    neg = -jnp.inf
    gl = jnp.where(lane < N_EXPERT_GROUPS, logits, neg)
    gmax = jnp.max(gl, axis=-1, keepdims=True)
    g_idx = jnp.min(jnp.where(gl == gmax, lane, LANES), axis=-1, keepdims=True)
    g_w = 1.0 / jnp.sum(jnp.exp(gl - gmax), axis=-1, keepdims=True)
    lo = N_EXPERT_GROUPS + EXPERTS_PER_GROUP * g_idx
    el = jnp.where(jnp.logical_and(lane >= lo, lane < lo + EXPERTS_PER_GROUP), logits, neg)
    v1 = jnp.max(el, axis=-1, keepdims=True)
    i1 = jnp.min(jnp.where(el == v1, lane, LANES), axis=-1, keepdims=True)
    el2 = jnp.where(lane == i1, neg, el)
    v2 = jnp.max(el2, axis=-1, keepdims=True)
    i2 = jnp.min(jnp.where(el2 == v2, lane, LANES), axis=-1, keepdims=True)
    e2 = jnp.exp(v2 - v1)
    p1 = 1.0 / (1.0 + e2)
    p2 = e2 / (1.0 + e2)
    oh1 = lane == i1 - N_EXPERT_GROUPS
    oh2 = lane == i2 - N_EXPERT_GROUPS
    oh1f, oh2f = oh1.astype(jnp.float32), oh2.astype(jnp.float32)
    before = (lax.broadcasted_iota(jnp.int32, (rows, rows), 1)
              < lax.broadcasted_iota(jnp.int32, (rows, rows), 0)).astype(jnp.bfloat16)
    pre1 = jnp.dot(before, oh1f.astype(jnp.bfloat16), preferred_element_type=jnp.float32)
    pre2 = jnp.dot(before, oh2f.astype(jnp.bfloat16), preferred_element_type=jnp.float32)
    cnt1 = jnp.sum(oh1f, axis=0, keepdims=True)
    cnt2 = jnp.sum(oh2f, axis=0, keepdims=True)
    base = carry[...]
    rank1 = jnp.sum(jnp.where(oh1, pre1 + base, 0.0), axis=-1, keepdims=True)
    rank2 = jnp.sum(jnp.where(oh2, pre2 + (base + cnt1), 0.0), axis=-1, keepdims=True)
    total = base + cnt1 + cnt2
    carry[...] = total
    cnt_ref[...] = total
    out = jnp.where(lane == 0, (i1 - N_EXPERT_GROUPS).astype(jnp.float32),
          jnp.where(lane == 1, (i2 - N_EXPERT_GROUPS).astype(jnp.float32),
          jnp.where(lane == 2, g_w * p1,
          jnp.where(lane == 3, g_w * p2,
          jnp.where(lane == 4, rank1,
          jnp.where(lane == 5, rank2, 0.0))))))
    o_ref[...] = out


def _router(x, g, w_rg, b_rg, w_re, b_re, rows=512):
    n, d = x.shape
    pad = LANES - N_EXPERT_GROUPS - N_EXPERTS
    w = jnp.concatenate([w_rg, w_re, jnp.zeros((d, pad), w_rg.dtype)], axis=1).astype(jnp.bfloat16)
    b = jnp.concatenate([b_rg, b_re, jnp.zeros((pad,), b_rg.dtype)]).reshape(1, LANES)
    return pl.pallas_call(
        _router_body,
        grid=(n // rows,),
        in_specs=[pl.BlockSpec((rows, d), lambda i: (i, 0)),
                  pl.BlockSpec((1, d), lambda i: (0, 0)),
                  pl.BlockSpec((d, LANES), lambda i: (0, 0)),
                  pl.BlockSpec((1, LANES), lambda i: (0, 0))],
        out_specs=[pl.BlockSpec((rows, LANES), lambda i: (i, 0)),
                   pl.BlockSpec((rows, d // 2), lambda i: (i, 0)),
                   pl.BlockSpec((1, LANES), lambda i: (0, 0))],
        out_shape=[jax.ShapeDtypeStruct((n, LANES), jnp.float32),
                   jax.ShapeDtypeStruct((n, d // 2), jnp.uint32),
                   jax.ShapeDtypeStruct((1, LANES), jnp.float32)],
        scratch_shapes=[pltpu.VMEM((1, LANES), jnp.float32)],
        compiler_params=_params(("arbitrary",)),
        name="router",
    )(x, g.reshape(1, d), w, b)


def _dispatch_body(s1_ref, s2_ref, hp_ref, xb_in_hbm, xb_hbm, stage, sem):
    del xb_in_hbm
    i = pl.program_id(0)
    n_steps = pl.num_programs(0)
    rows = s1_ref.shape[2]
    slot = i % 2

    def wait_slot(of_slot):
        for _ in range(TOP_K):
            pltpu.make_async_copy(stage.at[of_slot], stage.at[of_slot], sem.at[of_slot]).wait()

    @pl.when(i >= 2)
    def _():
        wait_slot(slot)

    stage[slot] = hp_ref[...]

    def issue(r, carry):
        src = stage.at[slot, pl.ds(r, 1)]
        pltpu.make_async_copy(src, xb_hbm.at[pl.ds(s1_ref[0, 0, r], 1)], sem.at[slot]).start()
        pltpu.make_async_copy(src, xb_hbm.at[pl.ds(s2_ref[0, 0, r], 1)], sem.at[slot]).start()
        return carry
    lax.fori_loop(0, rows, issue, 0, unroll=8)

    @pl.when(i == n_steps - 1)
    def _():
        @pl.when(i >= 1)
        def _():
            wait_slot(1 - slot)
        wait_slot(slot)


def _dispatch(hp, slot1, slot2, xb_zero, rows=512):
    n, c = hp.shape
    p_total = xb_zero.shape[0]
    steps = n // rows
    tab = pl.BlockSpec((1, 1, rows), lambda i: (i, 0, 0), memory_space=pltpu.SMEM)
    any_spec = pl.BlockSpec(memory_space=pl.ANY)
    return pl.pallas_call(
        _dispatch_body,
        grid=(steps,),
        in_specs=[tab, tab, pl.BlockSpec((rows, c), lambda i: (i, 0)), any_spec],
        out_specs=any_spec,
        out_shape=jax.ShapeDtypeStruct((p_total, c), hp.dtype),
        scratch_shapes=[pltpu.VMEM((2, rows, c), hp.dtype),
                        pltpu.SemaphoreType.DMA((2,))],
        input_output_aliases={3: 0},
        compiler_params=_params(("arbitrary",)),
        name="moe_dispatch",
    )(slot1.reshape(steps, 1, rows), slot2.reshape(steps, 1, rows), hp, xb_zero)


def _side_cast_specs(side, n_steps, index_of_step, dtype):
    in_specs, args, out_specs, out_shape = [], [], [], []
    for stack, layer in side:
        cols = stack.shape[-1]
        flat = stack.reshape(-1, cols)
        layer_rows = flat.shape[0] // stack.shape[0]
        n_slabs = 1
        while (n_slabs * 2 <= n_steps and layer_rows % (n_slabs * 2) == 0
               and (layer_rows // (n_slabs * 2)) % 16 == 0):
            n_slabs *= 2
        slab_rows = layer_rows // n_slabs

        def slab(*idx, last=n_slabs - 1):
            return jnp.minimum(index_of_step(*idx), last)
        in_specs.append(pl.BlockSpec((slab_rows, cols),
                                     lambda *idx, slab=slab, first=layer * n_slabs: (first + slab(*idx), 0)))
        args.append(flat)
        out_specs.append(pl.BlockSpec((slab_rows, cols), lambda *idx, slab=slab: (slab(*idx), 0)))
        out_shape.append(jax.ShapeDtypeStruct((layer_rows, cols), dtype))
    return in_specs, args, out_specs, out_shape


def _expert_body(blk_e_ref, nvalid_ref, xb_ref, wg_ref, wu_ref, wd_ref, *refs):
    del blk_e_ref
    n_side = (len(refs) - 1) // 2
    side_in, yb_ref, side_out = refs[:n_side], refs[n_side], refs[n_side + 1:]
    i = pl.program_id(0)
    for s_in, s_out in zip(side_in, side_out):
        s_out[...] = s_in[...].astype(s_out.dtype)

    @pl.when(nvalid_ref[i] > 0)
    def _():
        h = _unpack_halves(xb_ref[...]).astype(jnp.bfloat16)
        a = jax.nn.silu(jnp.dot(h, wg_ref[0], preferred_element_type=jnp.float32))
        a = a * jnp.dot(h, wu_ref[0], preferred_element_type=jnp.float32)
        y = jnp.dot(a.astype(jnp.bfloat16), wd_ref[0], preferred_element_type=jnp.float32)
        yb_ref[...] = _pack_halves(y)

    @pl.when(nvalid_ref[i] == 0)
    def _():
        yb_ref[...] = jnp.zeros_like(yb_ref)


def _moe_experts(xb, blk_e, nvalid, w_gate, w_up, w_down, side=()):
    p_total, c = xb.shape
    d = 2 * c
    d_e = w_gate.shape[-1]
    n_blk = p_total // MOE_ROWS
    s_in, s_args, s_out, s_shape = _side_cast_specs(side, n_blk, lambda i, *_: i, jnp.bfloat16)
    grid_spec = pltpu.PrefetchScalarGridSpec(
        num_scalar_prefetch=2,
        grid=(n_blk,),
        in_specs=[pl.BlockSpec((MOE_ROWS, c), lambda i, e, nv: (i, 0)),
                  pl.BlockSpec((1, d, d_e), lambda i, e, nv: (e[i], 0, 0)),
                  pl.BlockSpec((1, d, d_e), lambda i, e, nv: (e[i], 0, 0)),
                  pl.BlockSpec((1, d_e, d), lambda i, e, nv: (e[i], 0, 0))] + s_in,
        out_specs=[pl.BlockSpec((MOE_ROWS, c), lambda i, e, nv: (i, 0))] + s_out,
    )
    outs = pl.pallas_call(
        _expert_body,
        grid_spec=grid_spec,
        out_shape=[jax.ShapeDtypeStruct((p_total, c), xb.dtype)] + s_shape,
        compiler_params=_params(("arbitrary",)),
        name="moe_experts",
    )(blk_e, nvalid, xb, w_gate, w_up, w_down, *s_args)
    return outs[0], [o.reshape(stack.shape[1:]) for o, (stack, _) in zip(outs[1:], side)]


def _combine_body(s1_ref, s2_ref, s1n_ref, s2n_ref, x_ref, routed_ref, g_ref, yb_hbm, *refs, emit_sum):
    out_refs, ybuf, sem = refs[:-2], refs[-2], refs[-1]
    i = pl.program_id(0)
    n_steps = pl.num_programs(0)
    slot = i % 2
    rows = x_ref.shape[0]

    def gather(t1_ref, t2_ref, to_slot):
        def issue(r, carry):
            pltpu.make_async_copy(yb_hbm.at[pl.ds(t1_ref[0, 0, r], 1)], ybuf.at[to_slot, pl.ds(r, 1)],
                                  sem.at[to_slot]).start()
            pltpu.make_async_copy(yb_hbm.at[pl.ds(t2_ref[0, 0, r], 1)], ybuf.at[to_slot, pl.ds(rows + r, 1)],
                                  sem.at[to_slot]).start()
            return carry
        lax.fori_loop(0, rows, issue, 0, unroll=8)

    @pl.when(i == 0)
    def _():
        gather(s1_ref, s2_ref, 0)

    @pl.when(i + 1 < n_steps)
    def _():
        gather(s1n_ref, s2n_ref, 1 - slot)

    pltpu.make_async_copy(ybuf.at[slot], ybuf.at[slot], sem.at[slot]).wait()
    y1 = _unpack_halves(ybuf[slot, :rows])
    y2 = _unpack_halves(ybuf[slot, rows:])
    s = x_ref[...] + (y1 * routed_ref[:, 2:3] + y2 * routed_ref[:, 3:4])
    if emit_sum:
        out_refs[0][...] = s
    ms = jnp.mean(s * s, axis=-1, keepdims=True)
    out_refs[-1][...] = (s * lax.rsqrt(ms + EPS) * g_ref[...]).astype(out_refs[-1].dtype)


def _combine_norm(x, yb, routed, slot1, slot2, g, norm_dtype, emit_sum, rows=256):
    n, d = x.shape
    steps = n // rows
    row_spec = pl.BlockSpec((rows, d), lambda i: (i, 0))
    tab = pl.BlockSpec((1, 1, rows), lambda i: (i, 0, 0), memory_space=pltpu.SMEM)
    tab_next = pl.BlockSpec((1, 1, rows), lambda i: (jnp.minimum(i + 1, steps - 1), 0, 0),
                            memory_space=pltpu.SMEM)
    out_specs = [row_spec]
    out_shape = [jax.ShapeDtypeStruct((n, d), norm_dtype)]
    if emit_sum:
        out_specs = [row_spec] + out_specs
        out_shape = [jax.ShapeDtypeStruct((n, d), x.dtype)] + out_shape
    s1 = slot1.reshape(steps, 1, rows)
    s2 = slot2.reshape(steps, 1, rows)
    return pl.pallas_call(
        functools.partial(_combine_body, emit_sum=emit_sum),
        grid=(steps,),
        in_specs=[tab, tab, tab_next, tab_next, row_spec,
                  pl.BlockSpec((rows, LANES), lambda i: (i, 0)),
                  pl.BlockSpec((1, d), lambda i: (0, 0)),
                  pl.BlockSpec(memory_space=pl.ANY)],
        out_specs=out_specs,
        out_shape=out_shape,
        scratch_shapes=[pltpu.VMEM((2, TOP_K * rows, d // 2), yb.dtype),
                        pltpu.SemaphoreType.DMA((2,))],
        compiler_params=_params(("arbitrary",)),
        name="combine_norm",
    )(s1, s2, s1, s2, x, routed, g.reshape(1, d), yb)


def _moe_slot_rows(n):
    return n * TOP_K + N_EXPERTS * MOE_ROWS


def _moe_layer(x, ffn_g, next_g, norm_dtype, emit_sum, w_rg, b_rg, w_re, b_re, w_gate, w_up, w_down, xb_zero,
               side=()):
    n, d = x.shape
    routed, hp, counts = _router(x, ffn_g, w_rg, b_rg, w_re, b_re)

    n_blk = _moe_slot_rows(n) // MOE_ROWS
    experts = jnp.arange(N_EXPERTS, dtype=jnp.int32)
    counts = counts[0, :N_EXPERTS].astype(jnp.int32)
    padded = ((counts + MOE_ROWS - 1) // MOE_ROWS) * MOE_ROWS
    pends = jnp.cumsum(padded)
    pstarts = pends - padded
    expert_id = routed[:, 0:TOP_K].astype(jnp.int32)
    rank = routed[:, 2 * TOP_K:3 * TOP_K].astype(jnp.int32)
    slot = jnp.sum(jnp.where(expert_id[:, :, None] == experts, pstarts, 0), axis=-1) + rank
    blk_start = jnp.arange(n_blk, dtype=jnp.int32) * MOE_ROWS
    blk_e = jnp.minimum(jnp.sum(blk_start[:, None] >= pends[None, :], axis=1), N_EXPERTS - 1).astype(jnp.int32)
    seg_end = jnp.sum(jnp.where(blk_e[:, None] == experts, pstarts + counts, 0), axis=1)
    nvalid = jnp.clip(seg_end - blk_start, 0, MOE_ROWS).astype(jnp.int32)

    xb = _dispatch(hp, slot[:, 0], slot[:, 1], xb_zero)
    yb, cast = _moe_experts(xb, blk_e, nvalid, w_gate, w_up, w_down, side)
    return _combine_norm(x, yb, routed, slot[:, 0], slot[:, 1], next_g, norm_dtype, emit_sum), cast


def kernel(x, mix_norm, ffn_norm, sgu_w_in, sgu_b_in, sgu_v_gain, sgu_v_bias, sgu_w_spatial, sgu_b_spatial,
           sgu_w_out, dil_w_qkv, dil_w_out, rel_bias, router_w_group, router_b_group, router_w_expert,
           router_b_expert, moe_w_gate, moe_w_up, moe_w_down, final_norm):
    batch, seq, d = x.shape
    n = batch * seq
    bf16 = jnp.bfloat16
    xf = x.reshape(n, d)
    depth = mix_norm.shape[0]
    n_dil = len(DIL_CONFIGS)
    h = _rmsnorm(xf, mix_norm[0], bf16)
    out = None
    dense = {}
    for i in range(depth):
        j = i // 2
        xb_shape = (_moe_slot_rows(n), d // 2, jnp.uint32)
        if i % 2 == 0:
            w_in, w_out = dense.get(i, (sgu_w_in[j], sgu_w_out[j]))
            z, (w_gate, w_up), xb_zero = _matmul(h, w_in, bias=sgu_b_in[j], use_gelu=True, out_dtype=bf16,
                                                 side=((moe_w_gate, i), (moe_w_up, i)), zero_fill=xb_shape)
            y = _sgu_gate(z, sgu_v_gain[j], sgu_v_bias[j], sgu_w_spatial[j], sgu_b_spatial[j])
            xf, (w_down,) = _matmul(y, w_out, residual=xf, out_dtype=jnp.float32, side=((moe_w_down, i),))
        else:
            w_qkv, w_out = dense.get(i, (dil_w_qkv[j], dil_w_out[j]))
            outs, lses, cast = [], [], []
            tn = 512
            per_part = DIL_INNER // tn
            for g, (window, dilation) in enumerate(DIL_CONFIGS):
                res = _matmul(
                    h, w_qkv, out_dtype=bf16, n_out=3 * DIL_INNER, tn=tn,
                    col_block_map=lambda c, g=g: ((c // per_part) * n_dil + g) * per_part + c % per_part,
                    col_scale=(per_part, HEAD_DIM ** -0.5),
                    regroup=(batch, seq, dilation), side=(((moe_w_gate, moe_w_up, moe_w_down)[g], i),),
                    zero_fill=xb_shape if g == 0 else None)
                qkv, (w_cast,) = res[:2]
                if g == 0:
                    xb_zero = res[2]
                cast.append(w_cast)
                bias = _band_bias(rel_bias[:, g * DIL_HEADS:(g + 1) * DIL_HEADS], window, dilation)
                o, lse = _dilated_group(qkv, bias)
                outs.append(o)
                lses.append(lse)
            w_gate, w_up, w_down = cast
            merged = _merge_groups(outs, lses)
            xf = _matmul(merged, w_out, residual=xf, out_dtype=jnp.float32)
        last = i + 1 == depth
        side = ()
        if not last:
            j_next = (i + 1) // 2
            side = (((sgu_w_in, j_next), (sgu_w_out, j_next)) if (i + 1) % 2 == 0
                    else ((dil_w_qkv, j_next), (dil_w_out, j_next)))
        res, cast = _moe_layer(xf, ffn_norm[i], final_norm if last else mix_norm[i + 1],
                               x.dtype if last else bf16, not last,
                               router_w_group[i], router_b_group[i], router_w_expert[i], router_b_expert[i],
                               w_gate, w_up, w_down, xb_zero, side)
        if cast:
            dense[i + 1] = tuple(cast)
        if last:
            (out,) = res
        else:
            xf, h = res
    return out.reshape(batch, seq, d)
```

```python
import functools
import math

import jax
import jax.numpy as jnp
from jax import lax
from jax.experimental import pallas as pl
from jax.experimental.pallas import tpu as pltpu

EPS = 1e-6
SGU_CHUNK = 128
SGU_GROUP_DIM = 128
DIL_CONFIGS = ((128, 1), (512, 4), (2048, 16))
DIL_HEADS = 16
HEAD_DIM = 128
DIL_BLOCK = 128
DIL_INNER = DIL_HEADS * HEAD_DIM
NUM_BUCKETS = 32
MAX_DISTANCE = 2048
N_EXPERT_GROUPS = 4
EXPERTS_PER_GROUP = 8
N_EXPERTS = N_EXPERT_GROUPS * EXPERTS_PER_GROUP
TOP_K = 2
MASK_VALUE = -1e30

LANES = 128
MOE_ROWS = 256
VMEM_LIMIT = 52 * 1024 * 1024
W_RING_SLOTS = 3
W_RING_BYTES = 12 * 1024 * 1024


def _params(sem, vmem=VMEM_LIMIT):
    return pltpu.CompilerParams(dimension_semantics=sem, vmem_limit_bytes=vmem)


def _rmsnorm_body(x_ref, g_ref, o_ref):
    x = x_ref[...]
    ms = jnp.mean(x * x, axis=-1, keepdims=True)
    o_ref[...] = (x * lax.rsqrt(ms + EPS) * g_ref[...]).astype(o_ref.dtype)


def _rmsnorm(x, g, out_dtype, rows=512):
    n, d = x.shape
    return pl.pallas_call(
        _rmsnorm_body,
        grid=(n // rows,),
        in_specs=[pl.BlockSpec((rows, d), lambda i: (i, 0)),
                  pl.BlockSpec((1, d), lambda i: (0, 0))],
        out_specs=pl.BlockSpec((rows, d), lambda i: (i, 0)),
        out_shape=jax.ShapeDtypeStruct((n, d), out_dtype),
        compiler_params=_params(("parallel",)),
        name="rmsnorm",
    )(x, g.reshape(1, d))


def _regroup_pitch(dilation):
    return dilation + 4 if dilation % 8 == 0 else dilation


def _matmul_body(*refs, has_bias, use_gelu, has_residual, dilation, n_side, has_zero, col_scale, w_col_of):
    a_ref, w_ref = refs[0], refs[1]
    pos = 2
    if w_col_of is None:
        w = w_ref[...]
    else:
        ring, sem = refs[-2], refs[-1]
        nj = pl.num_programs(1)
        step = pl.program_id(0) * nj + pl.program_id(1)
        n_steps = pl.num_programs(0) * nj
        tn = ring.shape[2]

        def w_copy(of_step):
            col = pl.multiple_of(w_col_of(lax.rem(of_step, nj)) * tn, tn)
            slot = lax.rem(of_step, W_RING_SLOTS)
            return pltpu.make_async_copy(w_ref.at[:, pl.ds(col, tn)], ring.at[slot], sem.at[slot])

        @pl.when(step == 0)
        def _():
            for ahead in range(W_RING_SLOTS - 1):
                @pl.when(ahead < n_steps)
                def _():
                    w_copy(ahead).start()

        @pl.when(step + W_RING_SLOTS - 1 < n_steps)
        def _():
            w_copy(step + W_RING_SLOTS - 1).start()

        w_copy(step).wait()
        w = ring[lax.rem(step, W_RING_SLOTS)]
    acc = jnp.dot(a_ref[...], w.astype(a_ref.dtype), preferred_element_type=jnp.float32)
    if col_scale is not None:
        blocks, scale = col_scale
        acc = acc * jnp.where(pl.program_id(1) < blocks, jnp.float32(scale), jnp.float32(1.0))
    if has_bias:
        acc = acc + refs[pos][...]
        pos += 1
    if use_gelu:
        acc = jax.nn.gelu(acc)
    if has_residual:
        acc = refs[pos][...] + acc
        pos += 1
    side_in = refs[pos:pos + n_side]
    o_ref = refs[pos + n_side]
    side_out = refs[pos + n_side + 1:pos + 2 * n_side + 1]
    pos += 2 * n_side + 1
    zero_ref = None
    if has_zero:
        zero_ref = refs[pos]
        pos += 1
    if dilation == 1:
        o_ref[...] = acc.astype(o_ref.dtype).reshape(o_ref.shape)
    else:
        acc_ref = refs[pos]
        sub = acc.shape[0] // dilation
        pitch = _regroup_pitch(dilation)
        for panel in range(acc_ref.shape[0]):
            cols = slice(panel * LANES, (panel + 1) * LANES)
            if pitch == dilation:
                acc_ref[panel] = acc[:, cols]
            else:
                for l in range(sub):
                    acc_ref[panel, l * pitch:l * pitch + dilation, :] = acc[l * dilation:(l + 1) * dilation, cols]
            for r in range(dilation):
                o_ref[0, r, :, cols] = acc_ref[panel, pl.ds(r, sub, stride=pitch), :].astype(o_ref.dtype)
    for s_in, s_out in zip(side_in, side_out):
        s_out[...] = s_in[...].astype(s_out.dtype)
    if has_zero:
        zero_ref[...] = jnp.zeros_like(zero_ref)


def _matmul(a, w, *, bias=None, use_gelu=False, residual=None, out_dtype, col_block_map=None, n_out=None,
            regroup=None, side=(), zero_fill=None, col_scale=None, tm=1024, tn=512):
    m, k = a.shape
    n = w.shape[1] if n_out is None else n_out
    tm, tn = min(tm, m), min(tn, n)
    grid = (m // tm, n // tn)
    w_col = (lambda j: j) if col_block_map is None else col_block_map
    use_ring = W_RING_SLOTS * k * tn * w.dtype.itemsize <= W_RING_BYTES
    w_spec = pl.BlockSpec(memory_space=pl.ANY) if use_ring else pl.BlockSpec((k, tn), lambda i, j: (0, w_col(j)))
    in_specs = [pl.BlockSpec((tm, k), lambda i, j: (i, 0)), w_spec]
    args = [a, w]
    if bias is not None:
        in_specs.append(pl.BlockSpec((1, tn), lambda i, j: (0, j)))
        args.append(bias.reshape(1, n))
    if residual is not None:
        in_specs.append(pl.BlockSpec((tm, tn), lambda i, j: (i, j)))
        args.append(residual)
    scratch = []
    dilation = 1
    if regroup is None:
        out_specs = [pl.BlockSpec((tm, tn), lambda i, j: (i, j))]
        out_shape = [jax.ShapeDtypeStruct((m, n), out_dtype)]
    else:
        batch, seq, dilation = regroup
        tiles = seq // tm
        out_specs = [pl.BlockSpec((1, dilation, tm // dilation, tn), lambda i, j: (i // tiles, 0, i % tiles, j))]
        out_shape = [jax.ShapeDtypeStruct((batch, dilation, seq // dilation, n), out_dtype)]
        if dilation > 1:
            scratch = [pltpu.VMEM((tn // LANES, tm // dilation * _regroup_pitch(dilation), LANES), jnp.float32)]
    n_steps = grid[0] * grid[1]

    def step_of(i, j):
        return i * grid[1] + j
    s_in, s_args, s_out, s_shape = _side_cast_specs(side, n_steps, step_of, a.dtype)
    in_specs += s_in
    args += s_args
    out_specs += s_out
    out_shape += s_shape
    if zero_fill is not None:
        z_rows, z_cols, z_dtype = zero_fill
        zero_slabs = 1
        while zero_slabs * 2 <= n_steps and z_rows % (zero_slabs * 2) == 0 and (z_rows // (zero_slabs * 2)) % 8 == 0:
            zero_slabs *= 2
        out_specs.append(pl.BlockSpec((z_rows // zero_slabs, z_cols),
                                      lambda i, j: (jnp.minimum(step_of(i, j), zero_slabs - 1), 0)))
        out_shape.append(jax.ShapeDtypeStruct((z_rows, z_cols), z_dtype))
    if use_ring:
        scratch += [pltpu.VMEM((W_RING_SLOTS, k, tn), w.dtype), pltpu.SemaphoreType.DMA((W_RING_SLOTS,))]
    body = functools.partial(_matmul_body, has_bias=bias is not None, use_gelu=use_gelu,
                             has_residual=residual is not None, dilation=dilation, n_side=len(side),
                             has_zero=zero_fill is not None, col_scale=col_scale,
                             w_col_of=w_col if use_ring else None)
    riders = bool(side) or zero_fill is not None
    sequential = riders or use_ring
    outs = pl.pallas_call(
        body,
        grid=grid,
        in_specs=in_specs,
        out_specs=out_specs,
        out_shape=out_shape,
        scratch_shapes=scratch,
        compiler_params=_params(("arbitrary", "arbitrary") if sequential else ("parallel", "arbitrary")),
        name="matmul",
    )(*args)
    if not riders:
        return outs[0]
    cast = [o.reshape(stack.shape[1:]) for o, (stack, _) in zip(outs[1:], side)]
    return (outs[0], cast, outs[-1]) if zero_fill is not None else (outs[0], cast)


def _sgu_gate_body(u_ref, v_ref, gain_ref, vbias_ref, ws_ref, bs_ref, o_ref, *, chunks):
    v = v_ref[...].astype(jnp.float32)
    mu = jnp.mean(v, axis=-1, keepdims=True)
    vc = v - mu
    var = jnp.mean(vc * vc, axis=-1, keepdims=True)
    vn = (vc * lax.rsqrt(var + EPS) * gain_ref[...] + vbias_ref[...]).astype(jnp.bfloat16)
    n_groups = ws_ref.shape[0]
    t_idx = lax.broadcasted_iota(jnp.int32, (SGU_CHUNK, SGU_CHUNK), 0)
    s_idx = lax.broadcasted_iota(jnp.int32, (SGU_CHUNK, SGU_CHUNK), 1)
    causal = s_idx <= t_idx
    for g in range(n_groups):
        cols = slice(g * SGU_GROUP_DIM, (g + 1) * SGU_GROUP_DIM)
        w = jnp.where(causal, ws_ref[g], jnp.zeros((), ws_ref.dtype))
        for c in range(chunks):
            rows = slice(c * SGU_CHUNK, (c + 1) * SGU_CHUNK)
            vm = jnp.dot(w, vn[rows, cols], preferred_element_type=jnp.float32) + bs_ref[:, cols]
            o_ref[rows, cols] = (u_ref[rows, cols].astype(jnp.float32) * vm).astype(o_ref.dtype)


def _sgu_gate(z, v_gain, v_bias, w_s, b_s, chunks=2):
    n, two_w = z.shape
    width = two_w // 2
    rows = chunks * SGU_CHUNK
    n_groups = w_s.shape[0]
    b_full = jnp.repeat(b_s.T, SGU_GROUP_DIM, axis=1)
    body = functools.partial(_sgu_gate_body, chunks=chunks)
    return pl.pallas_call(
        body,
        grid=(n // rows,),
        in_specs=[pl.BlockSpec((rows, width), lambda i: (i, 0)),
                  pl.BlockSpec((rows, width), lambda i: (i, 1)),
                  pl.BlockSpec((1, width), lambda i: (0, 0)),
                  pl.BlockSpec((1, width), lambda i: (0, 0)),
                  pl.BlockSpec((n_groups, SGU_CHUNK, SGU_CHUNK), lambda i: (0, 0, 0)),
                  pl.BlockSpec((SGU_CHUNK, width), lambda i: (0, 0))],
        out_specs=pl.BlockSpec((rows, width), lambda i: (i, 0)),
        out_shape=jax.ShapeDtypeStruct((n, width), jnp.bfloat16),
        compiler_params=_params(("parallel",)),
        name="sgu_gate",
    )(z, z, v_gain.reshape(1, width), v_bias.reshape(1, width), w_s.astype(jnp.bfloat16), b_full)


def _t5_causal_bucket(dist):
    max_exact = NUM_BUCKETS // 2
    d = jnp.maximum(dist, 0)
    df = jnp.maximum(d, 1).astype(jnp.float32)
    large = max_exact + (jnp.log(df / max_exact) / math.log(MAX_DISTANCE / max_exact)
                         * (NUM_BUCKETS - max_exact)).astype(jnp.int32)
    large = jnp.minimum(large, NUM_BUCKETS - 1)
    return jnp.where(d < max_exact, d, large)


def _band_bias(bias_table, window, dilation):
    span = window // dilation
    qi = jnp.arange(DIL_BLOCK)[:, None]
    kj = jnp.arange(2 * DIL_BLOCK)[None, :]
    sub_dist = qi + DIL_BLOCK - kj
    band = (sub_dist >= 0) & (sub_dist <= span)
    bucket = _t5_causal_bucket(sub_dist * dilation)
    onehot = (bucket[:, :, None] == jnp.arange(NUM_BUCKETS)).astype(jnp.float32)
    bias = jnp.einsum('qkb,bh->hqk', onehot, bias_table.astype(jnp.float32), precision=lax.Precision.HIGHEST)
    bias = jnp.where(band[None], bias, MASK_VALUE)
    return jnp.stack([bias, jnp.where(kj[None] < DIL_BLOCK, MASK_VALUE, bias)])


def _attn_body(q_ref, kp_ref, kc_ref, vp_ref, vc_ref, bias_ref, o_ref, lse_ref):
    lane = lax.broadcasted_iota(jnp.int32, (DIL_BLOCK, LANES), 1)
    for qb in range(q_ref.shape[2] // DIL_BLOCK):
        rows = slice(qb * DIL_BLOCK, (qb + 1) * DIL_BLOCK)
        before = slice((qb - 1) * DIL_BLOCK, qb * DIL_BLOCK)
        table = jnp.where(pl.program_id(2) == 0, 1, 0) if qb == 0 else 0
        lse_all = jnp.zeros((DIL_BLOCK, LANES), jnp.float32)
        for h in range(DIL_HEADS):
            cols = slice(h * HEAD_DIM, (h + 1) * HEAD_DIM)
            q = q_ref[0, 0, rows, cols]
            k_prev = kp_ref[0, 0, :, cols] if qb == 0 else kc_ref[0, 0, before, cols]
            v_prev = vp_ref[0, 0, :, cols] if qb == 0 else vc_ref[0, 0, before, cols]
            k = jnp.concatenate([k_prev, kc_ref[0, 0, rows, cols]], axis=0)
            v = jnp.concatenate([v_prev, vc_ref[0, 0, rows, cols]], axis=0)
            s = lax.dot_general(q, k, (((1,), (1,)), ((), ())), preferred_element_type=jnp.float32)
            s = s + bias_ref[table, h]
            m = jnp.max(s, axis=-1, keepdims=True)
            p = jnp.exp(s - m)
            den = jnp.sum(p, axis=-1, keepdims=True)
            o = jnp.dot(p.astype(jnp.bfloat16), v, preferred_element_type=jnp.float32) / den
            o_ref[0, 0, rows, cols] = o.astype(o_ref.dtype)
            lse_all = jnp.where(lane == h, m + jnp.log(den), lse_all)
        lse_ref[0, 0, rows, :] = lse_all


def _dilated_group(qkv, band_bias):
    batch, dilation, sub_len, _ = qkv.shape
    nb = sub_len // DIL_BLOCK
    per_step = next(c for c in (4, 2, 1) if nb % c == 0)
    rows = per_step * DIL_BLOCK

    def spec(part, prev):
        if prev:
            return pl.BlockSpec((1, 1, DIL_BLOCK, DIL_INNER),
                                lambda b, r, n: (b, r, jnp.maximum(per_step * n - 1, 0), part))
        return pl.BlockSpec((1, 1, rows, DIL_INNER), lambda b, r, n: (b, r, n, part))

    return pl.pallas_call(
        _attn_body,
        grid=(batch, dilation, nb // per_step),
        in_specs=[spec(0, False), spec(1, True), spec(1, False), spec(2, True), spec(2, False),
                  pl.BlockSpec((2, DIL_HEADS, DIL_BLOCK, 2 * DIL_BLOCK), lambda b, r, n: (0, 0, 0, 0))],
        out_specs=[pl.BlockSpec((1, 1, rows, DIL_INNER), lambda b, r, n: (b, r, n, 0)),
                   pl.BlockSpec((1, 1, rows, LANES), lambda b, r, n: (b, r, n, 0))],
        out_shape=[jax.ShapeDtypeStruct((batch, dilation, sub_len, DIL_INNER), jnp.bfloat16),
                   jax.ShapeDtypeStruct((batch, dilation, sub_len, LANES), jnp.float32)],
        compiler_params=_params(("parallel", "parallel", "arbitrary")),
        name="dilated_attn",
    )(qkv, qkv, qkv, qkv, qkv, band_bias)


def _merge_body(*refs, dilations):
    n_g = len(dilations)
    o_refs, l_refs = refs[:n_g], refs[n_g:2 * n_g]
    out_ref, ltok, wtok, acc = refs[2 * n_g:]
    rows = out_ref.shape[0]

    def residue_rows(r, d):
        return pl.ds(r, rows // d, stride=d) if d > 1 else slice(None)

    for g, d in enumerate(dilations):
        for r in range(d):
            ltok[g, residue_rows(r, d), :] = l_refs[g][0, r]
    lse = [ltok[g] for g in range(n_g)]
    m = functools.reduce(jnp.maximum, lse)
    e = [jnp.exp(l - m) for l in lse]
    tot = functools.reduce(jnp.add, e)
    for g in range(n_g):
        wtok[g] = e[g] / tot
    for g, d in enumerate(dilations):
        for r in range(d):
            rr = residue_rows(r, d)
            w = wtok[g, rr, :]
            for h in range(DIL_HEADS):
                cols = slice(h * HEAD_DIM, (h + 1) * HEAD_DIM)
                term = o_refs[g][0, r, :, cols].astype(jnp.float32) * w[:, h:h + 1]
                if g == 0:
                    acc[h, rr, :] = term
                else:
                    acc[h, rr, :] = acc[h, rr, :] + term
    for h in range(DIL_HEADS):
        out_ref[:, h * HEAD_DIM:(h + 1) * HEAD_DIM] = acc[h].astype(out_ref.dtype)


def _merge_groups(outs, lses, rows=512):
    batch = outs[0].shape[0]
    dilations = tuple(o.shape[1] for o in outs)
    seq = outs[0].shape[1] * outs[0].shape[2]
    tiles = seq // rows

    def spec(d, width):
        return pl.BlockSpec((1, d, rows // d, width), lambda i: (i // tiles, 0, i % tiles, 0))

    n_g = len(outs)
    return pl.pallas_call(
        functools.partial(_merge_body, dilations=dilations),
        grid=(batch * tiles,),
        in_specs=[spec(d, DIL_INNER) for d in dilations] + [spec(d, LANES) for d in dilations],
        out_specs=pl.BlockSpec((rows, DIL_INNER), lambda i: (i, 0)),
        out_shape=jax.ShapeDtypeStruct((batch * seq, DIL_INNER), jnp.bfloat16),
        scratch_shapes=[pltpu.VMEM((n_g, rows, LANES), jnp.float32),
                        pltpu.VMEM((n_g, rows, LANES), jnp.float32),
                        pltpu.VMEM((DIL_HEADS, rows, HEAD_DIM), jnp.float32)],
        compiler_params=_params(("parallel",)),
        name="merge_groups",
    )(*outs, *lses)


def _pack_halves(v):
    c = v.shape[1] // 2
    r = v.astype(jnp.bfloat16).astype(jnp.float32)
    lo = lax.bitcast_convert_type(r[:, :c], jnp.uint32)
    hi = lax.bitcast_convert_type(r[:, c:], jnp.uint32)
    return (lo >> 16) | (hi & jnp.uint32(0xFFFF0000))


def _unpack_halves(w):
    lo = lax.bitcast_convert_type(w << 16, jnp.float32)
    hi = lax.bitcast_convert_type(w & jnp.uint32(0xFFFF0000), jnp.float32)
    return jnp.concatenate([lo, hi], axis=1)


def _router_body(x_ref, g_ref, w_ref, b_ref, o_ref, hp_ref, cnt_ref, carry):
    i = pl.program_id(0)

    @pl.when(i == 0)
    def _():
        carry[...] = jnp.zeros_like(carry)

    x = x_ref[...]
    ms = jnp.mean(x * x, axis=-1, keepdims=True)
    hf = x * lax.rsqrt(ms + EPS) * g_ref[...]
    hp_ref[...] = _pack_halves(hf)
    logits = jnp.dot(hf.astype(jnp.bfloat16), w_ref[...], preferred_element_type=jnp.float32) + b_ref[...]
    rows = logits.shape[0]
    lane = lax.broadcasted_iota(jnp.int32, logits.shape, 1)
    neg = -jnp.inf
    gl = jnp.where(lane < N_EXPERT_GROUPS, logits, neg)
    gmax = jnp.max(gl, axis=-1, keepdims=True)
    g_idx = jnp.min(jnp.where(gl == gmax, lane, LANES), axis=-1, keepdims=True)
    g_w = 1.0 / jnp.sum(jnp.exp(gl - gmax), axis=-1, keepdims=True)
    lo = N_EXPERT_GROUPS + EXPERTS_PER_GROUP * g_idx
    el = jnp.where(jnp.logical_and(lane >= lo, lane < lo + EXPERTS_PER_GROUP), logits, neg)
    v1 = jnp.max(el, axis=-1, keepdims=True)
    i1 = jnp.min(jnp.where(el == v1, lane, LANES), axis=-1, keepdims=True)
    el2 = jnp.where(lane == i1, neg, el)
    v2 = jnp.max(el2, axis=-1, keepdims=True)
    i2 = jnp.min(jnp.where(el2 == v2, lane, LANES), axis=-1, keepdims=True)
    e2 = jnp.exp(v2 - v1)
    p1 = 1.0 / (1.0 + e2)
    p2 = e2 / (1.0 + e2)
    oh1 = lane == i1 - N_EXPERT_GROUPS
    oh2 = lane == i2 - N_EXPERT_GROUPS
    oh1f, oh2f = oh1.astype(jnp.float32), oh2.astype(jnp.float32)
    before = (lax.broadcasted_iota(jnp.int32, (rows, rows), 1)
              < lax.broadcasted_iota(jnp.int32, (rows, rows), 0)).astype(jnp.bfloat16)
    pre1 = jnp.dot(before, oh1f.astype(jnp.bfloat16), preferred_element_type=jnp.float32)
    pre2 = jnp.dot(before, oh2f.astype(jnp.bfloat16), preferred_element_type=jnp.float32)
    cnt1 = jnp.sum(oh1f, axis=0, keepdims=True)
    cnt2 = jnp.sum(oh2f, axis=0, keepdims=True)
    base = carry[...]
    rank1 = jnp.sum(jnp.where(oh1, pre1 + base, 0.0), axis=-1, keepdims=True)
    rank2 = jnp.sum(jnp.where(oh2, pre2 + (base + cnt1), 0.0), axis=-1, keepdims=True)
    total = base + cnt1 + cnt2
    carry[...] = total
    cnt_ref[...] = total
    out = jnp.where(lane == 0, (i1 - N_EXPERT_GROUPS).astype(jnp.float32),
          jnp.where(lane == 1, (i2 - N_EXPERT_GROUPS).astype(jnp.float32),
          jnp.where(lane == 2, g_w * p1,
          jnp.where(lane == 3, g_w * p2,
          jnp.where(lane == 4, rank1,
          jnp.where(lane == 5, rank2, 0.0))))))
    o_ref[...] = out


def _router(x, g, w_rg, b_rg, w_re, b_re, rows=512):
    n, d = x.shape
    pad = LANES - N_EXPERT_GROUPS - N_EXPERTS
    w = jnp.concatenate([w_rg, w_re, jnp.zeros((d, pad), w_rg.dtype)], axis=1).astype(jnp.bfloat16)
    b = jnp.concatenate([b_rg, b_re, jnp.zeros((pad,), b_rg.dtype)]).reshape(1, LANES)
    return pl.pallas_call(
        _router_body,
        grid=(n // rows,),
        in_specs=[pl.BlockSpec((rows, d), lambda i: (i, 0)),
                  pl.BlockSpec((1, d), lambda i: (0, 0)),
                  pl.BlockSpec((d, LANES), lambda i: (0, 0)),
                  pl.BlockSpec((1, LANES), lambda i: (0, 0))],
        out_specs=[pl.BlockSpec((rows, LANES), lambda i: (i, 0)),
                   pl.BlockSpec((rows, d // 2), lambda i: (i, 0)),
                   pl.BlockSpec((1, LANES), lambda i: (0, 0))],
        out_shape=[jax.ShapeDtypeStruct((n, LANES), jnp.float32),
                   jax.ShapeDtypeStruct((n, d // 2), jnp.uint32),
                   jax.ShapeDtypeStruct((1, LANES), jnp.float32)],
        scratch_shapes=[pltpu.VMEM((1, LANES), jnp.float32)],
        compiler_params=_params(("arbitrary",)),
        name="router",
    )(x, g.reshape(1, d), w, b)


def _dispatch_body(s1_ref, s2_ref, hp_ref, xb_in_hbm, xb_hbm, stage, sem):
    del xb_in_hbm
    i = pl.program_id(0)
    n_steps = pl.num_programs(0)
    rows = s1_ref.shape[2]
    slot = i % 2

    def wait_slot(of_slot):
        for _ in range(TOP_K):
            pltpu.make_async_copy(stage.at[of_slot], stage.at[of_slot], sem.at[of_slot]).wait()

    @pl.when(i >= 2)
    def _():
        wait_slot(slot)

    stage[slot] = hp_ref[...]

    def issue(r, carry):
        src = stage.at[slot, pl.ds(r, 1)]
        pltpu.make_async_copy(src, xb_hbm.at[pl.ds(s1_ref[0, 0, r], 1)], sem.at[slot]).start()
        pltpu.make_async_copy(src, xb_hbm.at[pl.ds(s2_ref[0, 0, r], 1)], sem.at[slot]).start()
        return carry
    lax.fori_loop(0, rows, issue, 0, unroll=8)

    @pl.when(i == n_steps - 1)
    def _():
        @pl.when(i >= 1)
        def _():
            wait_slot(1 - slot)
        wait_slot(slot)


def _dispatch(hp, slot1, slot2, xb_init, rows=512):
    n, c = hp.shape
    p_total = xb_init.shape[0]
    steps = n // rows
    tab = pl.BlockSpec((1, 1, rows), lambda i: (i, 0, 0), memory_space=pltpu.SMEM)
    any_spec = pl.BlockSpec(memory_space=pl.ANY)
    return pl.pallas_call(
        _dispatch_body,
        grid=(steps,),
        in_specs=[tab, tab, pl.BlockSpec((rows, c), lambda i: (i, 0)), any_spec],
        out_specs=any_spec,
        out_shape=jax.ShapeDtypeStruct((p_total, c), hp.dtype),
        scratch_shapes=[pltpu.VMEM((2, rows, c), hp.dtype),
                        pltpu.SemaphoreType.DMA((2,))],
        input_output_aliases={3: 0},
        compiler_params=_params(("arbitrary",)),
        name="moe_dispatch",
    )(slot1.reshape(steps, 1, rows), slot2.reshape(steps, 1, rows), hp, xb_init)


def _side_cast_specs(side, n_steps, index_of_step, dtype):
    in_specs, args, out_specs, out_shape = [], [], [], []
    for stack, layer in side:
        cols = stack.shape[-1]
        flat = stack.reshape(-1, cols)
        layer_rows = flat.shape[0] // stack.shape[0]
        n_slabs = 1
        while (n_slabs * 2 <= n_steps and layer_rows % (n_slabs * 2) == 0
               and (layer_rows // (n_slabs * 2)) % 16 == 0):
            n_slabs *= 2
        slab_rows = layer_rows // n_slabs

        def slab(*idx, last=n_slabs - 1):
            return jnp.minimum(index_of_step(*idx), last)
        in_specs.append(pl.BlockSpec((slab_rows, cols),
                                     lambda *idx, slab=slab, first=layer * n_slabs: (first + slab(*idx), 0)))
        args.append(flat)
        out_specs.append(pl.BlockSpec((slab_rows, cols), lambda *idx, slab=slab: (slab(*idx), 0)))
        out_shape.append(jax.ShapeDtypeStruct((layer_rows, cols), dtype))
    return in_specs, args, out_specs, out_shape


def _expert_body(blk_e_ref, nvalid_ref, xb_ref, wg_ref, wu_ref, wd_ref, *refs):
    del blk_e_ref
    n_side = (len(refs) - 1) // 2
    side_in, yb_ref, side_out = refs[:n_side], refs[n_side], refs[n_side + 1:]
    i = pl.program_id(0)
    for s_in, s_out in zip(side_in, side_out):
        s_out[...] = s_in[...].astype(s_out.dtype)

    @pl.when(nvalid_ref[i] > 0)
    def _():
        h = _unpack_halves(xb_ref[...]).astype(jnp.bfloat16)
        a = jax.nn.silu(jnp.dot(h, wg_ref[0], preferred_element_type=jnp.float32))
        a = a * jnp.dot(h, wu_ref[0], preferred_element_type=jnp.float32)
        y = jnp.dot(a.astype(jnp.bfloat16), wd_ref[0], preferred_element_type=jnp.float32)
        yb_ref[...] = _pack_halves(y)

    @pl.when(nvalid_ref[i] == 0)
    def _():
        yb_ref[...] = jnp.zeros_like(yb_ref)


def _moe_experts(xb, blk_e, nvalid, w_gate, w_up, w_down, side=()):
    p_total, c = xb.shape
    d = 2 * c
    d_e = w_gate.shape[-1]
    n_blk = p_total // MOE_ROWS
    s_in, s_args, s_out, s_shape = _side_cast_specs(side, n_blk, lambda i, *_: i, jnp.bfloat16)
    grid_spec = pltpu.PrefetchScalarGridSpec(
        num_scalar_prefetch=2,
        grid=(n_blk,),
        in_specs=[pl.BlockSpec((MOE_ROWS, c), lambda i, e, nv: (i, 0)),
                  pl.BlockSpec((1, d, d_e), lambda i, e, nv: (e[i], 0, 0)),
                  pl.BlockSpec((1, d, d_e), lambda i, e, nv: (e[i], 0, 0)),
                  pl.BlockSpec((1, d_e, d), lambda i, e, nv: (e[i], 0, 0))] + s_in,
        out_specs=[pl.BlockSpec((MOE_ROWS, c), lambda i, e, nv: (i, 0))] + s_out,
    )
    outs = pl.pallas_call(
        _expert_body,
        grid_spec=grid_spec,
        out_shape=[jax.ShapeDtypeStruct((p_total, c), xb.dtype)] + s_shape,
        compiler_params=_params(("arbitrary",)),
        name="moe_experts",
    )(blk_e, nvalid, xb, w_gate, w_up, w_down, *s_args)
    return outs[0], [o.reshape(stack.shape[1:]) for o, (stack, _) in zip(outs[1:], side)]


def _combine_body(s1_ref, s2_ref, s1n_ref, s2n_ref, x_ref, routed_ref, g_ref, yb_hbm, *refs, emit_sum):
    out_refs, ybuf, sem = refs[:-2], refs[-2], refs[-1]
    i = pl.program_id(0)
    n_steps = pl.num_programs(0)
    slot = i % 2
    rows = x_ref.shape[0]

    def gather(t1_ref, t2_ref, to_slot):
        def issue(r, carry):
            pltpu.make_async_copy(yb_hbm.at[pl.ds(t1_ref[0, 0, r], 1)], ybuf.at[to_slot, pl.ds(r, 1)],
                                  sem.at[to_slot]).start()
            pltpu.make_async_copy(yb_hbm.at[pl.ds(t2_ref[0, 0, r], 1)], ybuf.at[to_slot, pl.ds(rows + r, 1)],
                                  sem.at[to_slot]).start()
            return carry
        lax.fori_loop(0, rows, issue, 0, unroll=8)

    @pl.when(i == 0)
    def _():
        gather(s1_ref, s2_ref, 0)

    @pl.when(i + 1 < n_steps)
    def _():
        gather(s1n_ref, s2n_ref, 1 - slot)

    pltpu.make_async_copy(ybuf.at[slot], ybuf.at[slot], sem.at[slot]).wait()
    y1 = _unpack_halves(ybuf[slot, :rows])
    y2 = _unpack_halves(ybuf[slot, rows:])
    s = x_ref[...] + (y1 * routed_ref[:, 2:3] + y2 * routed_ref[:, 3:4])
    if emit_sum:
        out_refs[0][...] = s
    ms = jnp.mean(s * s, axis=-1, keepdims=True)
    out_refs[-1][...] = (s * lax.rsqrt(ms + EPS) * g_ref[...]).astype(out_refs[-1].dtype)


def _combine_norm(x, yb, routed, slot1, slot2, g, norm_dtype, emit_sum, rows=256):
    n, d = x.shape
    steps = n // rows
    row_spec = pl.BlockSpec((rows, d), lambda i: (i, 0))
    tab = pl.BlockSpec((1, 1, rows), lambda i: (i, 0, 0), memory_space=pltpu.SMEM)
    tab_next = pl.BlockSpec((1, 1, rows), lambda i: (jnp.minimum(i + 1, steps - 1), 0, 0),
                            memory_space=pltpu.SMEM)
    out_specs = [row_spec]
    out_shape = [jax.ShapeDtypeStruct((n, d), norm_dtype)]
    if emit_sum:
        out_specs = [row_spec] + out_specs
        out_shape = [jax.ShapeDtypeStruct((n, d), x.dtype)] + out_shape
    s1 = slot1.reshape(steps, 1, rows)
    s2 = slot2.reshape(steps, 1, rows)
    return pl.pallas_call(
        functools.partial(_combine_body, emit_sum=emit_sum),
        grid=(steps,),
        in_specs=[tab, tab, tab_next, tab_next, row_spec,
                  pl.BlockSpec((rows, LANES), lambda i: (i, 0)),
                  pl.BlockSpec((1, d), lambda i: (0, 0)),
                  pl.BlockSpec(memory_space=pl.ANY)],
        out_specs=out_specs,
        out_shape=out_shape,
        scratch_shapes=[pltpu.VMEM((2, TOP_K * rows, d // 2), yb.dtype),
                        pltpu.SemaphoreType.DMA((2,))],
        compiler_params=_params(("arbitrary",)),
        name="combine_norm",
    )(s1, s2, s1, s2, x, routed, g.reshape(1, d), yb)


def _moe_slot_rows(n):
    return n * TOP_K + N_EXPERTS * MOE_ROWS


def _moe_layer(x, ffn_g, next_g, norm_dtype, emit_sum, w_rg, b_rg, w_re, b_re, w_gate, w_up, w_down, xb_init,
               side=()):
    n, d = x.shape
    routed, hp, counts = _router(x, ffn_g, w_rg, b_rg, w_re, b_re)

    n_blk = _moe_slot_rows(n) // MOE_ROWS
    experts = jnp.arange(N_EXPERTS, dtype=jnp.int32)
    counts = counts[0, :N_EXPERTS].astype(jnp.int32)
    padded = ((counts + MOE_ROWS - 1) // MOE_ROWS) * MOE_ROWS
    pends = jnp.cumsum(padded)
    pstarts = pends - padded
    expert_id = routed[:, 0:TOP_K].astype(jnp.int32)
    rank = routed[:, 2 * TOP_K:3 * TOP_K].astype(jnp.int32)
    slot = jnp.sum(jnp.where(expert_id[:, :, None] == experts, pstarts, 0), axis=-1) + rank
    blk_start = jnp.arange(n_blk, dtype=jnp.int32) * MOE_ROWS
    blk_e = jnp.minimum(jnp.sum(blk_start[:, None] >= pends[None, :], axis=1), N_EXPERTS - 1).astype(jnp.int32)
    seg_end = jnp.sum(jnp.where(blk_e[:, None] == experts, pstarts + counts, 0), axis=1)
    nvalid = jnp.clip(seg_end - blk_start, 0, MOE_ROWS).astype(jnp.int32)

    xb = _dispatch(hp, slot[:, 0], slot[:, 1], xb_init)
    yb, cast = _moe_experts(xb, blk_e, nvalid, w_gate, w_up, w_down, side)
    return _combine_norm(x, yb, routed, slot[:, 0], slot[:, 1], next_g, norm_dtype, emit_sum), cast, xb


def kernel(x, mix_norm, ffn_norm, sgu_w_in, sgu_b_in, sgu_v_gain, sgu_v_bias, sgu_w_spatial, sgu_b_spatial,
           sgu_w_out, dil_w_qkv, dil_w_out, rel_bias, router_w_group, router_b_group, router_w_expert,
           router_b_expert, moe_w_gate, moe_w_up, moe_w_down, final_norm):
    batch, seq, d = x.shape
    n = batch * seq
    bf16 = jnp.bfloat16
    xf = x.reshape(n, d)
    depth = mix_norm.shape[0]
    n_dil = len(DIL_CONFIGS)
    h = _rmsnorm(xf, mix_norm[0], bf16)
    out = None
    dense = {}
    xb_buf = None
    for i in range(depth):
        j = i // 2
        xb_shape = (_moe_slot_rows(n), d // 2, jnp.uint32) if xb_buf is None else None
        if i % 2 == 0:
            w_in, w_out = dense.get(i, (sgu_w_in[j], sgu_w_out[j]))
            res = _matmul(h, w_in, bias=sgu_b_in[j], use_gelu=True, out_dtype=bf16,
                          side=((moe_w_gate, i), (moe_w_up, i)), zero_fill=xb_shape)
            z, (w_gate, w_up) = res[:2]
            if xb_shape is not None:
                xb_buf = res[2]
            y = _sgu_gate(z, sgu_v_gain[j], sgu_v_bias[j], sgu_w_spatial[j], sgu_b_spatial[j])
            xf, (w_down,) = _matmul(y, w_out, residual=xf, out_dtype=jnp.float32, side=((moe_w_down, i),))
        else:
            w_qkv, w_out = dense.get(i, (dil_w_qkv[j], dil_w_out[j]))
            outs, lses, cast = [], [], []
            tn = 512
            per_part = DIL_INNER // tn
            for g, (window, dilation) in enumerate(DIL_CONFIGS):
                res = _matmul(
                    h, w_qkv, out_dtype=bf16, n_out=3 * DIL_INNER, tn=tn,
                    col_block_map=lambda c, g=g: ((c // per_part) * n_dil + g) * per_part + c % per_part,
                    col_scale=(per_part, HEAD_DIM ** -0.5),
                    regroup=(batch, seq, dilation), side=(((moe_w_gate, moe_w_up, moe_w_down)[g], i),),
                    zero_fill=xb_shape if g == 0 else None)
                qkv, (w_cast,) = res[:2]
                if g == 0 and xb_shape is not None:
                    xb_buf = res[2]
                cast.append(w_cast)
                bias = _band_bias(rel_bias[:, g * DIL_HEADS:(g + 1) * DIL_HEADS], window, dilation)
                o, lse = _dilated_group(qkv, bias)
                outs.append(o)
                lses.append(lse)
            w_gate, w_up, w_down = cast
            merged = _merge_groups(outs, lses)
            xf = _matmul(merged, w_out, residual=xf, out_dtype=jnp.float32)
        last = i + 1 == depth
        side = ()
        if not last:
            j_next = (i + 1) // 2
            side = (((sgu_w_in, j_next), (sgu_w_out, j_next)) if (i + 1) % 2 == 0
                    else ((dil_w_qkv, j_next), (dil_w_out, j_next)))
        res, cast, xb_buf = _moe_layer(xf, ffn_norm[i], final_norm if last else mix_norm[i + 1],
                                       x.dtype if last else bf16, not last,
                                       router_w_group[i], router_b_group[i], router_w_expert[i], router_b_expert[i],
                                       w_gate, w_up, w_down, xb_buf, side)
        if cast:
            dense[i + 1] = tuple(cast)
        if last:
            (out,) = res
        else:
            xf, h = res
    return out.reshape(batch, seq, d)
```

```python
import functools
import math

import jax
import jax.numpy as jnp
from jax import lax
from jax.experimental import pallas as pl
from jax.experimental.pallas import tpu as pltpu

EPS = 1e-6
SGU_CHUNK = 128
SGU_GROUP_DIM = 128
DIL_CONFIGS = ((128, 1), (512, 4), (2048, 16))
DIL_HEADS = 16
HEAD_DIM = 128
DIL_BLOCK = 128
DIL_INNER = DIL_HEADS * HEAD_DIM
NUM_BUCKETS = 32
MAX_DISTANCE = 2048
N_EXPERT_GROUPS = 4
EXPERTS_PER_GROUP = 8
N_EXPERTS = N_EXPERT_GROUPS * EXPERTS_PER_GROUP
TOP_K = 2
MASK_VALUE = -1e30

LANES = 128
MOE_ROWS = 256
VMEM_LIMIT = 52 * 1024 * 1024
W_RING_SLOTS = 3
W_RING_BYTES = 12 * 1024 * 1024


def _params(sem, vmem=VMEM_LIMIT):
    return pltpu.CompilerParams(dimension_semantics=sem, vmem_limit_bytes=vmem)


def _rmsnorm_body(x_ref, g_ref, o_ref):
    x = x_ref[...]
    ms = jnp.mean(x * x, axis=-1, keepdims=True)
    o_ref[...] = (x * lax.rsqrt(ms + EPS) * g_ref[...]).astype(o_ref.dtype)


def _rmsnorm(x, g, out_dtype, rows=512):
    n, d = x.shape
    return pl.pallas_call(
        _rmsnorm_body,
        grid=(n // rows,),
        in_specs=[pl.BlockSpec((rows, d), lambda i: (i, 0)),
                  pl.BlockSpec((1, d), lambda i: (0, 0))],
        out_specs=pl.BlockSpec((rows, d), lambda i: (i, 0)),
        out_shape=jax.ShapeDtypeStruct((n, d), out_dtype),
        compiler_params=_params(("parallel",)),
        name="rmsnorm",
    )(x, g.reshape(1, d))


def _regroup_pitch(dilation):
    return dilation + 4 if dilation % 8 == 0 else dilation


def _matmul_body(*refs, has_bias, use_gelu, has_residual, dilation, n_side, has_zero, col_scale, w_col_of):
    a_ref, w_ref = refs[0], refs[1]
    pos = 2
    if w_col_of is None:
        w = w_ref[...]
    else:
        ring, sem = refs[-2], refs[-1]
        nj = pl.num_programs(1)
        step = pl.program_id(0) * nj + pl.program_id(1)
        n_steps = pl.num_programs(0) * nj
        tn = ring.shape[2]

        def w_copy(of_step):
            col = pl.multiple_of(w_col_of(lax.rem(of_step, nj)) * tn, tn)
            slot = lax.rem(of_step, W_RING_SLOTS)
            return pltpu.make_async_copy(w_ref.at[:, pl.ds(col, tn)], ring.at[slot], sem.at[slot])

        @pl.when(step == 0)
        def _():
            for ahead in range(W_RING_SLOTS - 1):
                @pl.when(ahead < n_steps)
                def _():
                    w_copy(ahead).start()

        @pl.when(step + W_RING_SLOTS - 1 < n_steps)
        def _():
            w_copy(step + W_RING_SLOTS - 1).start()

        w_copy(step).wait()
        w = ring[lax.rem(step, W_RING_SLOTS)]
    acc = jnp.dot(a_ref[...], w.astype(a_ref.dtype), preferred_element_type=jnp.float32)
    if col_scale is not None:
        blocks, scale = col_scale
        acc = acc * jnp.where(pl.program_id(1) < blocks, jnp.float32(scale), jnp.float32(1.0))
    if has_bias:
        acc = acc + refs[pos][...]
        pos += 1
    if use_gelu:
        acc = jax.nn.gelu(acc)
    if has_residual:
        acc = refs[pos][...] + acc
        pos += 1
    side_in = refs[pos:pos + n_side]
    o_ref = refs[pos + n_side]
    side_out = refs[pos + n_side + 1:pos + 2 * n_side + 1]
    pos += 2 * n_side + 1
    zero_ref = None
    if has_zero:
        zero_ref = refs[pos]
        pos += 1
    if dilation == 1:
        o_ref[...] = acc.astype(o_ref.dtype).reshape(o_ref.shape)
    else:
        acc_ref = refs[pos]
        sub = acc.shape[0] // dilation
        pitch = _regroup_pitch(dilation)
        for panel in range(acc_ref.shape[0]):
            cols = slice(panel * LANES, (panel + 1) * LANES)
            if pitch == dilation:
                acc_ref[panel] = acc[:, cols]
            else:
                for l in range(sub):
                    acc_ref[panel, l * pitch:l * pitch + dilation, :] = acc[l * dilation:(l + 1) * dilation, cols]
            for r in range(dilation):
                o_ref[0, r, :, cols] = acc_ref[panel, pl.ds(r, sub, stride=pitch), :].astype(o_ref.dtype)
    for s_in, s_out in zip(side_in, side_out):
        s_out[...] = s_in[...].astype(s_out.dtype)
    if has_zero:
        zero_ref[...] = jnp.zeros_like(zero_ref)


def _matmul(a, w, *, bias=None, use_gelu=False, residual=None, out_dtype, col_block_map=None, n_out=None,
            regroup=None, side=(), zero_fill=None, col_scale=None, tm=1024, tn=512):
    m, k = a.shape
    n = w.shape[1] if n_out is None else n_out
    tm, tn = min(tm, m), min(tn, n)
    grid = (m // tm, n // tn)
    w_col = (lambda j: j) if col_block_map is None else col_block_map
    use_ring = W_RING_SLOTS * k * tn * w.dtype.itemsize <= W_RING_BYTES
    w_spec = pl.BlockSpec(memory_space=pl.ANY) if use_ring else pl.BlockSpec((k, tn), lambda i, j: (0, w_col(j)))
    in_specs = [pl.BlockSpec((tm, k), lambda i, j: (i, 0)), w_spec]
    args = [a, w]
    if bias is not None:
        in_specs.append(pl.BlockSpec((1, tn), lambda i, j: (0, j)))
        args.append(bias.reshape(1, n))
    if residual is not None:
        in_specs.append(pl.BlockSpec((tm, tn), lambda i, j: (i, j)))
        args.append(residual)
    scratch = []
    dilation = 1
    if regroup is None:
        out_specs = [pl.BlockSpec((tm, tn), lambda i, j: (i, j))]
        out_shape = [jax.ShapeDtypeStruct((m, n), out_dtype)]
    else:
        batch, seq, dilation = regroup
        tiles = seq // tm
        out_specs = [pl.BlockSpec((1, dilation, tm // dilation, tn), lambda i, j: (i // tiles, 0, i % tiles, j))]
        out_shape = [jax.ShapeDtypeStruct((batch, dilation, seq // dilation, n), out_dtype)]
        if dilation > 1:
            scratch = [pltpu.VMEM((tn // LANES, tm // dilation * _regroup_pitch(dilation), LANES), jnp.float32)]
    n_steps = grid[0] * grid[1]

    def step_of(i, j):
        return i * grid[1] + j
    s_in, s_args, s_out, s_shape = _side_cast_specs(side, n_steps, step_of, a.dtype)
    in_specs += s_in
    args += s_args
    out_specs += s_out
    out_shape += s_shape
    if zero_fill is not None:
        z_rows, z_cols, z_dtype = zero_fill
        zero_slabs = 1
        while zero_slabs * 2 <= n_steps and z_rows % (zero_slabs * 2) == 0 and (z_rows // (zero_slabs * 2)) % 8 == 0:
            zero_slabs *= 2
        out_specs.append(pl.BlockSpec((z_rows // zero_slabs, z_cols),
                                      lambda i, j: (jnp.minimum(step_of(i, j), zero_slabs - 1), 0)))
        out_shape.append(jax.ShapeDtypeStruct((z_rows, z_cols), z_dtype))
    if use_ring:
        scratch += [pltpu.VMEM((W_RING_SLOTS, k, tn), w.dtype), pltpu.SemaphoreType.DMA((W_RING_SLOTS,))]
    body = functools.partial(_matmul_body, has_bias=bias is not None, use_gelu=use_gelu,
                             has_residual=residual is not None, dilation=dilation, n_side=len(side),
                             has_zero=zero_fill is not None, col_scale=col_scale,
                             w_col_of=w_col if use_ring else None)
    riders = bool(side) or zero_fill is not None
    sequential = riders or use_ring
    outs = pl.pallas_call(
        body,
        grid=grid,
        in_specs=in_specs,
        out_specs=out_specs,
        out_shape=out_shape,
        scratch_shapes=scratch,
        compiler_params=_params(("arbitrary", "arbitrary") if sequential else ("parallel", "arbitrary")),
        name="matmul",
    )(*args)
    if not riders:
        return outs[0]
    cast = [o.reshape(stack.shape[1:]) for o, (stack, _) in zip(outs[1:], side)]
    return (outs[0], cast, outs[-1]) if zero_fill is not None else (outs[0], cast)


def _sgu_gate_body(u_ref, v_ref, gain_ref, vbias_ref, ws_ref, bs_ref, o_ref, *, chunks):
    v = v_ref[...].astype(jnp.float32)
    mu = jnp.mean(v, axis=-1, keepdims=True)
    vc = v - mu
    var = jnp.mean(vc * vc, axis=-1, keepdims=True)
    vn = (vc * lax.rsqrt(var + EPS) * gain_ref[...] + vbias_ref[...]).astype(jnp.bfloat16)
    n_groups = ws_ref.shape[0]
    t_idx = lax.broadcasted_iota(jnp.int32, (SGU_CHUNK, SGU_CHUNK), 0)
    s_idx = lax.broadcasted_iota(jnp.int32, (SGU_CHUNK, SGU_CHUNK), 1)
    causal = s_idx <= t_idx
    for g in range(n_groups):
        cols = slice(g * SGU_GROUP_DIM, (g + 1) * SGU_GROUP_DIM)
        w = jnp.where(causal, ws_ref[g], jnp.zeros((), ws_ref.dtype))
        for c in range(chunks):
            rows = slice(c * SGU_CHUNK, (c + 1) * SGU_CHUNK)
            vm = jnp.dot(w, vn[rows, cols], preferred_element_type=jnp.float32) + bs_ref[:, cols]
            o_ref[rows, cols] = (u_ref[rows, cols].astype(jnp.float32) * vm).astype(o_ref.dtype)


def _sgu_gate(z, v_gain, v_bias, w_s, b_s, chunks=2):
    n, two_w = z.shape
    width = two_w // 2
    rows = chunks * SGU_CHUNK
    n_groups = w_s.shape[0]
    b_full = jnp.repeat(b_s.T, SGU_GROUP_DIM, axis=1)
    body = functools.partial(_sgu_gate_body, chunks=chunks)
    return pl.pallas_call(
        body,
        grid=(n // rows,),
        in_specs=[pl.BlockSpec((rows, width), lambda i: (i, 0)),
                  pl.BlockSpec((rows, width), lambda i: (i, 1)),
                  pl.BlockSpec((1, width), lambda i: (0, 0)),
                  pl.BlockSpec((1, width), lambda i: (0, 0)),
                  pl.BlockSpec((n_groups, SGU_CHUNK, SGU_CHUNK), lambda i: (0, 0, 0)),
                  pl.BlockSpec((SGU_CHUNK, width), lambda i: (0, 0))],
        out_specs=pl.BlockSpec((rows, width), lambda i: (i, 0)),
        out_shape=jax.ShapeDtypeStruct((n, width), jnp.bfloat16),
        compiler_params=_params(("parallel",)),
        name="sgu_gate",
    )(z, z, v_gain.reshape(1, width), v_bias.reshape(1, width), w_s.astype(jnp.bfloat16), b_full)


def _t5_causal_bucket(dist):
    max_exact = NUM_BUCKETS // 2
    d = jnp.maximum(dist, 0)
    df = jnp.maximum(d, 1).astype(jnp.float32)
    large = max_exact + (jnp.log(df / max_exact) / math.log(MAX_DISTANCE / max_exact)
                         * (NUM_BUCKETS - max_exact)).astype(jnp.int32)
    large = jnp.minimum(large, NUM_BUCKETS - 1)
    return jnp.where(d < max_exact, d, large)


def _band_bias(bias_table, window, dilation):
    span = window // dilation
    qi = jnp.arange(DIL_BLOCK)[:, None]
    kj = jnp.arange(2 * DIL_BLOCK)[None, :]
    sub_dist = qi + DIL_BLOCK - kj
    band = (sub_dist >= 0) & (sub_dist <= span)
    bucket = _t5_causal_bucket(sub_dist * dilation)
    onehot = (bucket[:, :, None] == jnp.arange(NUM_BUCKETS)).astype(jnp.float32)
    bias = jnp.einsum('qkb,bh->hqk', onehot, bias_table.astype(jnp.float32), precision=lax.Precision.HIGHEST)
    bias = jnp.where(band[None], bias, MASK_VALUE)
    return jnp.stack([bias, jnp.where(kj[None] < DIL_BLOCK, MASK_VALUE, bias)])


def _attn_body(q_ref, kp_ref, kc_ref, vp_ref, vc_ref, bias_ref, o_ref, lse_ref):
    lane = lax.broadcasted_iota(jnp.int32, (DIL_BLOCK, LANES), 1)
    for qb in range(q_ref.shape[2] // DIL_BLOCK):
        rows = slice(qb * DIL_BLOCK, (qb + 1) * DIL_BLOCK)
        before = slice((qb - 1) * DIL_BLOCK, qb * DIL_BLOCK)
        table = jnp.where(pl.program_id(2) == 0, 1, 0) if qb == 0 else 0
        lse_all = jnp.zeros((DIL_BLOCK, LANES), jnp.float32)
        for h in range(DIL_HEADS):
            cols = slice(h * HEAD_DIM, (h + 1) * HEAD_DIM)
            q = q_ref[0, 0, rows, cols]
            k_prev = kp_ref[0, 0, :, cols] if qb == 0 else kc_ref[0, 0, before, cols]
            v_prev = vp_ref[0, 0, :, cols] if qb == 0 else vc_ref[0, 0, before, cols]
            k = jnp.concatenate([k_prev, kc_ref[0, 0, rows, cols]], axis=0)
            v = jnp.concatenate([v_prev, vc_ref[0, 0, rows, cols]], axis=0)
            s = lax.dot_general(q, k, (((1,), (1,)), ((), ())), preferred_element_type=jnp.float32)
            s = s + bias_ref[table, h]
            m = jnp.max(s, axis=-1, keepdims=True)
            p = jnp.exp(s - m)
            den = jnp.sum(p, axis=-1, keepdims=True)
            o = jnp.dot(p.astype(jnp.bfloat16), v, preferred_element_type=jnp.float32) / den
            o_ref[0, 0, rows, cols] = o.astype(o_ref.dtype)
            lse_all = jnp.where(lane == h, m + jnp.log(den), lse_all)
        lse_ref[0, 0, rows, :] = lse_all


def _dilated_group(qkv, band_bias):
    batch, dilation, sub_len, _ = qkv.shape
    nb = sub_len // DIL_BLOCK
    per_step = next(c for c in (4, 2, 1) if nb % c == 0)
    rows = per_step * DIL_BLOCK

    def spec(part, prev):
        if prev:
            return pl.BlockSpec((1, 1, DIL_BLOCK, DIL_INNER),
                                lambda b, r, n: (b, r, jnp.maximum(per_step * n - 1, 0), part))
        return pl.BlockSpec((1, 1, rows, DIL_INNER), lambda b, r, n: (b, r, n, part))

    return pl.pallas_call(
        _attn_body,
        grid=(batch, dilation, nb // per_step),
        in_specs=[spec(0, False), spec(1, True), spec(1, False), spec(2, True), spec(2, False),
                  pl.BlockSpec((2, DIL_HEADS, DIL_BLOCK, 2 * DIL_BLOCK), lambda b, r, n: (0, 0, 0, 0))],
        out_specs=[pl.BlockSpec((1, 1, rows, DIL_INNER), lambda b, r, n: (b, r, n, 0)),
                   pl.BlockSpec((1, 1, rows, LANES), lambda b, r, n: (b, r, n, 0))],
        out_shape=[jax.ShapeDtypeStruct((batch, dilation, sub_len, DIL_INNER), jnp.bfloat16),
                   jax.ShapeDtypeStruct((batch, dilation, sub_len, LANES), jnp.float32)],
        compiler_params=_params(("parallel", "parallel", "arbitrary")),
        name="dilated_attn",
    )(qkv, qkv, qkv, qkv, qkv, band_bias)


def _merge_body(*refs, dilations):
    n_g = len(dilations)
    o_refs, l_refs = refs[:n_g], refs[n_g:2 * n_g]
    out_ref, ltok, wtok, acc = refs[2 * n_g:]
    rows = out_ref.shape[0]

    def residue_rows(r, d):
        return pl.ds(r, rows // d, stride=d) if d > 1 else slice(None)

    for g, d in enumerate(dilations):
        for r in range(d):
            ltok[g, residue_rows(r, d), :] = l_refs[g][0, r]
    lse = [ltok[g] for g in range(n_g)]
    m = functools.reduce(jnp.maximum, lse)
    e = [jnp.exp(l - m) for l in lse]
    tot = functools.reduce(jnp.add, e)
    for g in range(n_g):
        wtok[g] = e[g] / tot
    for g, d in enumerate(dilations):
        for r in range(d):
            rr = residue_rows(r, d)
            w = wtok[g, rr, :]
            for h in range(DIL_HEADS):
                cols = slice(h * HEAD_DIM, (h + 1) * HEAD_DIM)
                term = o_refs[g][0, r, :, cols].astype(jnp.float32) * w[:, h:h + 1]
                if g == 0:
                    acc[h, rr, :] = term
                else:
                    acc[h, rr, :] = acc[h, rr, :] + term
    for h in range(DIL_HEADS):
        out_ref[:, h * HEAD_DIM:(h + 1) * HEAD_DIM] = acc[h].astype(out_ref.dtype)


def _merge_groups(outs, lses, rows=512):
    batch = outs[0].shape[0]
    dilations = tuple(o.shape[1] for o in outs)
    seq = outs[0].shape[1] * outs[0].shape[2]
    tiles = seq // rows

    def spec(d, width):
        return pl.BlockSpec((1, d, rows // d, width), lambda i: (i // tiles, 0, i % tiles, 0))

    n_g = len(outs)
    return pl.pallas_call(
        functools.partial(_merge_body, dilations=dilations),
        grid=(batch * tiles,),
        in_specs=[spec(d, DIL_INNER) for d in dilations] + [spec(d, LANES) for d in dilations],
        out_specs=pl.BlockSpec((rows, DIL_INNER), lambda i: (i, 0)),
        out_shape=jax.ShapeDtypeStruct((batch * seq, DIL_INNER), jnp.bfloat16),
        scratch_shapes=[pltpu.VMEM((n_g, rows, LANES), jnp.float32),
                        pltpu.VMEM((n_g, rows, LANES), jnp.float32),
                        pltpu.VMEM((DIL_HEADS, rows, HEAD_DIM), jnp.float32)],
        compiler_params=_params(("parallel",)),
        name="merge_groups",
    )(*outs, *lses)


def _pack_halves(v):
    c = v.shape[1] // 2
    r = v.astype(jnp.bfloat16).astype(jnp.float32)
    lo = lax.bitcast_convert_type(r[:, :c], jnp.uint32)
    hi = lax.bitcast_convert_type(r[:, c:], jnp.uint32)
    return (lo >> 16) | (hi & jnp.uint32(0xFFFF0000))


def _unpack_halves(w):
    lo = lax.bitcast_convert_type(w << 16, jnp.float32)
    hi = lax.bitcast_convert_type(w & jnp.uint32(0xFFFF0000), jnp.float32)
    return jnp.concatenate([lo, hi], axis=1)


def _router_body(x_ref, g_ref, w_ref, b_ref, o_ref, hp_ref, cnt_ref, carry):
    i = pl.program_id(0)

    @pl.when(i == 0)
    def _():
        carry[...] = jnp.zeros_like(carry)

    x = x_ref[...]
    ms = jnp.mean(x * x, axis=-1, keepdims=True)
    hf = x * lax.rsqrt(ms + EPS) * g_ref[...]
    hp_ref[...] = _pack_halves(hf)
    logits = jnp.dot(hf.astype(jnp.bfloat16), w_ref[...], preferred_element_type=jnp.float32) + b_ref[...]
    rows = logits.shape[0]
    lane = lax.broadcasted_iota(jnp.int32, logits.shape, 1)
    neg = -jnp.inf
    gl = jnp.where(lane < N_EXPERT_GROUPS, logits, neg)
    gmax = jnp.max(gl, axis=-1, keepdims=True)
    g_idx = jnp.min(jnp.where(gl == gmax, lane, LANES), axis=-1, keepdims=True)
    g_w = 1.0 / jnp.sum(jnp.exp(gl - gmax), axis=-1, keepdims=True)
    lo = N_EXPERT_GROUPS + EXPERTS_PER_GROUP * g_idx
    el = jnp.where(jnp.logical_and(lane >= lo, lane < lo + EXPERTS_PER_GROUP), logits, neg)
    v1 = jnp.max(el, axis=-1, keepdims=True)
    i1 = jnp.min(jnp.where(el == v1, lane, LANES), axis=-1, keepdims=True)
    el2 = jnp.where(lane == i1, neg, el)
    v2 = jnp.max(el2, axis=-1, keepdims=True)
    i2 = jnp.min(jnp.where(el2 == v2, lane, LANES), axis=-1, keepdims=True)
    e2 = jnp.exp(v2 - v1)
    p1 = 1.0 / (1.0 + e2)
    p2 = e2 / (1.0 + e2)
    oh1 = lane == i1 - N_EXPERT_GROUPS
    oh2 = lane == i2 - N_EXPERT_GROUPS
    oh1f, oh2f = oh1.astype(jnp.float32), oh2.astype(jnp.float32)
    before = (lax.broadcasted_iota(jnp.int32, (rows, rows), 1)
              < lax.broadcasted_iota(jnp.int32, (rows, rows), 0)).astype(jnp.bfloat16)
    pre1 = jnp.dot(before, oh1f.astype(jnp.bfloat16), preferred_element_type=jnp.float32)
    pre2 = jnp.dot(before, oh2f.astype(jnp.bfloat16), preferred_element_type=jnp.float32)
    cnt1 = jnp.sum(oh1f, axis=0, keepdims=True)
    cnt2 = jnp.sum(oh2f, axis=0, keepdims=True)
    base = carry[...]
    rank1 = jnp.sum(jnp.where(oh1, pre1 + base, 0.0), axis=-1, keepdims=True)
    rank2 = jnp.sum(jnp.where(oh2, pre2 + (base + cnt1), 0.0), axis=-1, keepdims=True)
    total = base + cnt1 + cnt2
    carry[...] = total
    cnt_ref[...] = total
    out = jnp.where(lane == 0, (i1 - N_EXPERT_GROUPS).astype(jnp.float32),
          jnp.where(lane == 1, (i2 - N_EXPERT_GROUPS).astype(jnp.float32),
          jnp.where(lane == 2, g_w * p1,
          jnp.where(lane == 3, g_w * p2,
          jnp.where(lane == 4, rank1,
          jnp.where(lane == 5, rank2, 0.0))))))
    o_ref[...] = out


def _router(x, g, w_rg, b_rg, w_re, b_re, rows=512):
    n, d = x.shape
    pad = LANES - N_EXPERT_GROUPS - N_EXPERTS
    w = jnp.concatenate([w_rg, w_re, jnp.zeros((d, pad), w_rg.dtype)], axis=1).astype(jnp.bfloat16)
    b = jnp.concatenate([b_rg, b_re, jnp.zeros((pad,), b_rg.dtype)]).reshape(1, LANES)
    return pl.pallas_call(
        _router_body,
        grid=(n // rows,),
        in_specs=[pl.BlockSpec((rows, d), lambda i: (i, 0)),
                  pl.BlockSpec((1, d), lambda i: (0, 0)),
                  pl.BlockSpec((d, LANES), lambda i: (0, 0)),
                  pl.BlockSpec((1, LANES), lambda i: (0, 0))],
        out_specs=[pl.BlockSpec((rows, LANES), lambda i: (i, 0)),
                   pl.BlockSpec((rows, d // 2), lambda i: (i, 0)),
                   pl.BlockSpec((1, LANES), lambda i: (0, 0))],
        out_shape=[jax.ShapeDtypeStruct((n, LANES), jnp.float32),
                   jax.ShapeDtypeStruct((n, d // 2), jnp.uint32),
                   jax.ShapeDtypeStruct((1, LANES), jnp.float32)],
        scratch_shapes=[pltpu.VMEM((1, LANES), jnp.float32)],
        compiler_params=_params(("arbitrary",)),
        name="router",
    )(x, g.reshape(1, d), w, b)


def _dispatch_body(s1_ref, s2_ref, hp_ref, xb_in_hbm, xb_hbm, stage, sem):
    del xb_in_hbm
    i = pl.program_id(0)
    n_steps = pl.num_programs(0)
    rows = s1_ref.shape[2]
    slot = i % 2

    def wait_slot(of_slot):
        for _ in range(TOP_K):
            pltpu.make_async_copy(stage.at[of_slot], stage.at[of_slot], sem.at[of_slot]).wait()

    @pl.when(i >= 2)
    def _():
        wait_slot(slot)

    stage[slot] = hp_ref[...]

    def issue(r, carry):
        src = stage.at[slot, pl.ds(r, 1)]
        pltpu.make_async_copy(src, xb_hbm.at[pl.ds(s1_ref[0, 0, r], 1)], sem.at[slot]).start()
        pltpu.make_async_copy(src, xb_hbm.at[pl.ds(s2_ref[0, 0, r], 1)], sem.at[slot]).start()
        return carry
    lax.fori_loop(0, rows, issue, 0, unroll=8)

    @pl.when(i == n_steps - 1)
    def _():
        @pl.when(i >= 1)
        def _():
            wait_slot(1 - slot)
        wait_slot(slot)


def _dispatch(hp, slot1, slot2, xb_init, rows=512):
    n, c = hp.shape
    p_total = xb_init.shape[0]
    steps = n // rows
    tab = pl.BlockSpec((1, 1, rows), lambda i: (i, 0, 0), memory_space=pltpu.SMEM)
    any_spec = pl.BlockSpec(memory_space=pl.ANY)
    return pl.pallas_call(
        _dispatch_body,
        grid=(steps,),
        in_specs=[tab, tab, pl.BlockSpec((rows, c), lambda i: (i, 0)), any_spec],
        out_specs=any_spec,
        out_shape=jax.ShapeDtypeStruct((p_total, c), hp.dtype),
        scratch_shapes=[pltpu.VMEM((2, rows, c), hp.dtype),
                        pltpu.SemaphoreType.DMA((2,))],
        input_output_aliases={3: 0},
        compiler_params=_params(("arbitrary",)),
        name="moe_dispatch",
    )(slot1.reshape(steps, 1, rows), slot2.reshape(steps, 1, rows), hp, xb_init)


def _side_cast_specs(side, n_steps, index_of_step, dtype):
    in_specs, args, out_specs, out_shape = [], [], [], []
    for stack, layer in side:
        cols = stack.shape[-1]
        flat = stack.reshape(-1, cols)
        layer_rows = flat.shape[0] // stack.shape[0]
        n_slabs = 1
        while (n_slabs * 2 <= n_steps and layer_rows % (n_slabs * 2) == 0
               and (layer_rows // (n_slabs * 2)) % 16 == 0):
            n_slabs *= 2
        slab_rows = layer_rows // n_slabs

        def slab(*idx, last=n_slabs - 1):
            return jnp.minimum(index_of_step(*idx), last)
        in_specs.append(pl.BlockSpec((slab_rows, cols),
                                     lambda *idx, slab=slab, first=layer * n_slabs: (first + slab(*idx), 0)))
        args.append(flat)
        out_specs.append(pl.BlockSpec((slab_rows, cols), lambda *idx, slab=slab: (slab(*idx), 0)))
        out_shape.append(jax.ShapeDtypeStruct((layer_rows, cols), dtype))
    return in_specs, args, out_specs, out_shape


def _cast_body(*refs):
    n_side = len(refs) // 2
    for s_in, s_out in zip(refs[:n_side], refs[n_side:]):
        s_out[...] = s_in[...].astype(s_out.dtype)


def _cast_layers(side, dtype, n_steps=32):
    s_in, s_args, s_out, s_shape = _side_cast_specs(side, n_steps, lambda i: i, dtype)
    outs = pl.pallas_call(
        _cast_body,
        grid=(n_steps,),
        in_specs=s_in,
        out_specs=s_out,
        out_shape=s_shape,
        compiler_params=_params(("arbitrary",)),
        name="cast",
    )(*s_args)
    return [o.reshape(stack.shape[1:]) for o, (stack, _) in zip(outs, side)]


def _expert_body(blk_e_ref, nvalid_ref, xb_ref, wg_ref, wu_ref, wd_ref, *refs):
    del blk_e_ref
    n_side = (len(refs) - 1) // 2
    side_in, yb_ref, side_out = refs[:n_side], refs[n_side], refs[n_side + 1:]
    i = pl.program_id(0)
    for s_in, s_out in zip(side_in, side_out):
        s_out[...] = s_in[...].astype(s_out.dtype)

    @pl.when(nvalid_ref[i] > 0)
    def _():
        h = _unpack_halves(xb_ref[...]).astype(jnp.bfloat16)
        a = jax.nn.silu(jnp.dot(h, wg_ref[0], preferred_element_type=jnp.float32))
        a = a * jnp.dot(h, wu_ref[0], preferred_element_type=jnp.float32)
        y = jnp.dot(a.astype(jnp.bfloat16), wd_ref[0], preferred_element_type=jnp.float32)
        yb_ref[...] = _pack_halves(y)

    @pl.when(nvalid_ref[i] == 0)
    def _():
        yb_ref[...] = jnp.zeros_like(yb_ref)


def _moe_experts(xb, blk_e, nvalid, w_gate, w_up, w_down, side=()):
    p_total, c = xb.shape
    d = 2 * c
    d_e = w_gate.shape[-1]
    n_blk = p_total // MOE_ROWS
    s_in, s_args, s_out, s_shape = _side_cast_specs(side, n_blk, lambda i, *_: i, jnp.bfloat16)
    grid_spec = pltpu.PrefetchScalarGridSpec(
        num_scalar_prefetch=2,
        grid=(n_blk,),
        in_specs=[pl.BlockSpec((MOE_ROWS, c), lambda i, e, nv: (i, 0)),
                  pl.BlockSpec((1, d, d_e), lambda i, e, nv: (e[i], 0, 0)),
                  pl.BlockSpec((1, d, d_e), lambda i, e, nv: (e[i], 0, 0)),
                  pl.BlockSpec((1, d_e, d), lambda i, e, nv: (e[i], 0, 0))] + s_in,
        out_specs=[pl.BlockSpec((MOE_ROWS, c), lambda i, e, nv: (i, 0))] + s_out,
    )
    outs = pl.pallas_call(
        _expert_body,
        grid_spec=grid_spec,
        out_shape=[jax.ShapeDtypeStruct((p_total, c), xb.dtype)] + s_shape,
        compiler_params=_params(("arbitrary",)),
        name="moe_experts",
    )(blk_e, nvalid, xb, w_gate, w_up, w_down, *s_args)
    return outs[0], [o.reshape(stack.shape[1:]) for o, (stack, _) in zip(outs[1:], side)]


def _combine_body(s1_ref, s2_ref, s1n_ref, s2n_ref, x_ref, routed_ref, g_ref, yb_hbm, *refs, emit_sum):
    out_refs, ybuf, sem = refs[:-2], refs[-2], refs[-1]
    i = pl.program_id(0)
    n_steps = pl.num_programs(0)
    slot = i % 2
    rows = x_ref.shape[0]

    def gather(t1_ref, t2_ref, to_slot):
        def issue(r, carry):
            pltpu.make_async_copy(yb_hbm.at[pl.ds(t1_ref[0, 0, r], 1)], ybuf.at[to_slot, pl.ds(r, 1)],
                                  sem.at[to_slot]).start()
            pltpu.make_async_copy(yb_hbm.at[pl.ds(t2_ref[0, 0, r], 1)], ybuf.at[to_slot, pl.ds(rows + r, 1)],
                                  sem.at[to_slot]).start()
            return carry
        lax.fori_loop(0, rows, issue, 0, unroll=8)

    @pl.when(i == 0)
    def _():
        gather(s1_ref, s2_ref, 0)

    @pl.when(i + 1 < n_steps)
    def _():
        gather(s1n_ref, s2n_ref, 1 - slot)

    pltpu.make_async_copy(ybuf.at[slot], ybuf.at[slot], sem.at[slot]).wait()
    y1 = _unpack_halves(ybuf[slot, :rows])
    y2 = _unpack_halves(ybuf[slot, rows:])
    s = x_ref[...] + (y1 * routed_ref[:, 2:3] + y2 * routed_ref[:, 3:4])
    if emit_sum:
        out_refs[0][...] = s
    ms = jnp.mean(s * s, axis=-1, keepdims=True)
    out_refs[-1][...] = (s * lax.rsqrt(ms + EPS) * g_ref[...]).astype(out_refs[-1].dtype)


def _combine_norm(x, yb, routed, slot1, slot2, g, norm_dtype, emit_sum, rows=256):
    n, d = x.shape
    steps = n // rows
    row_spec = pl.BlockSpec((rows, d), lambda i: (i, 0))
    tab = pl.BlockSpec((1, 1, rows), lambda i: (i, 0, 0), memory_space=pltpu.SMEM)
    tab_next = pl.BlockSpec((1, 1, rows), lambda i: (jnp.minimum(i + 1, steps - 1), 0, 0),
                            memory_space=pltpu.SMEM)
    out_specs = [row_spec]
    out_shape = [jax.ShapeDtypeStruct((n, d), norm_dtype)]
    if emit_sum:
        out_specs = [row_spec] + out_specs
        out_shape = [jax.ShapeDtypeStruct((n, d), x.dtype)] + out_shape
    s1 = slot1.reshape(steps, 1, rows)
    s2 = slot2.reshape(steps, 1, rows)
    return pl.pallas_call(
        functools.partial(_combine_body, emit_sum=emit_sum),
        grid=(steps,),
        in_specs=[tab, tab, tab_next, tab_next, row_spec,
                  pl.BlockSpec((rows, LANES), lambda i: (i, 0)),
                  pl.BlockSpec((1, d), lambda i: (0, 0)),
                  pl.BlockSpec(memory_space=pl.ANY)],
        out_specs=out_specs,
        out_shape=out_shape,
        scratch_shapes=[pltpu.VMEM((2, TOP_K * rows, d // 2), yb.dtype),
                        pltpu.SemaphoreType.DMA((2,))],
        compiler_params=_params(("arbitrary",)),
        name="combine_norm",
    )(s1, s2, s1, s2, x, routed, g.reshape(1, d), yb)


def _moe_slot_rows(n):
    return n * TOP_K + N_EXPERTS * MOE_ROWS


def _moe_layer(x, ffn_g, next_g, norm_dtype, emit_sum, w_rg, b_rg, w_re, b_re, w_gate, w_up, w_down, xb_init,
               side=()):
    n, d = x.shape
    routed, hp, counts = _router(x, ffn_g, w_rg, b_rg, w_re, b_re)

    n_blk = _moe_slot_rows(n) // MOE_ROWS
    experts = jnp.arange(N_EXPERTS, dtype=jnp.int32)
    counts = counts[0, :N_EXPERTS].astype(jnp.int32)
    padded = ((counts + MOE_ROWS - 1) // MOE_ROWS) * MOE_ROWS
    pends = jnp.cumsum(padded)
    pstarts = pends - padded
    expert_id = routed[:, 0:TOP_K].astype(jnp.int32)
    rank = routed[:, 2 * TOP_K:3 * TOP_K].astype(jnp.int32)
    slot = jnp.sum(jnp.where(expert_id[:, :, None] == experts, pstarts, 0), axis=-1) + rank
    blk_start = jnp.arange(n_blk, dtype=jnp.int32) * MOE_ROWS
    blk_e = jnp.minimum(jnp.sum(blk_start[:, None] >= pends[None, :], axis=1), N_EXPERTS - 1).astype(jnp.int32)
    seg_end = jnp.sum(jnp.where(blk_e[:, None] == experts, pstarts + counts, 0), axis=1)
    nvalid = jnp.clip(seg_end - blk_start, 0, MOE_ROWS).astype(jnp.int32)

    xb = _dispatch(hp, slot[:, 0], slot[:, 1], xb_init)
    yb, cast = _moe_experts(xb, blk_e, nvalid, w_gate, w_up, w_down, side)
    return _combine_norm(x, yb, routed, slot[:, 0], slot[:, 1], next_g, norm_dtype, emit_sum), cast, xb


def kernel(x, mix_norm, ffn_norm, sgu_w_in, sgu_b_in, sgu_v_gain, sgu_v_bias, sgu_w_spatial, sgu_b_spatial,
           sgu_w_out, dil_w_qkv, dil_w_out, rel_bias, router_w_group, router_b_group, router_w_expert,
           router_b_expert, moe_w_gate, moe_w_up, moe_w_down, final_norm):
    batch, seq, d = x.shape
    n = batch * seq
    bf16 = jnp.bfloat16
    xf = x.reshape(n, d)
    depth = mix_norm.shape[0]
    n_dil = len(DIL_CONFIGS)
    h = _rmsnorm(xf, mix_norm[0], bf16)
    out = None
    dense = {}
    xb_buf = None
    for i in range(depth):
        j = i // 2
        xb_shape = (_moe_slot_rows(n), d // 2, jnp.uint32) if xb_buf is None else None
        if i not in dense:
            stacks = (sgu_w_in, sgu_w_out) if i % 2 == 0 else (dil_w_qkv, dil_w_out)
            dense[i] = tuple(_cast_layers(tuple((s, j) for s in stacks), bf16))
        if i % 2 == 0:
            w_in, w_out = dense[i]
            res = _matmul(h, w_in, bias=sgu_b_in[j], use_gelu=True, out_dtype=bf16,
                          side=((moe_w_gate, i), (moe_w_up, i)), zero_fill=xb_shape)
            z, (w_gate, w_up) = res[:2]
            if xb_shape is not None:
                xb_buf = res[2]
            y = _sgu_gate(z, sgu_v_gain[j], sgu_v_bias[j], sgu_w_spatial[j], sgu_b_spatial[j])
            xf, (w_down,) = _matmul(y, w_out, residual=xf, out_dtype=jnp.float32, side=((moe_w_down, i),))
        else:
            w_qkv, w_out = dense[i]
            outs, lses, cast = [], [], []
            tn = 512
            per_part = DIL_INNER // tn
            for g, (window, dilation) in enumerate(DIL_CONFIGS):
                res = _matmul(
                    h, w_qkv, out_dtype=bf16, n_out=3 * DIL_INNER, tn=tn,
                    col_block_map=lambda c, g=g: ((c // per_part) * n_dil + g) * per_part + c % per_part,
                    col_scale=(per_part, HEAD_DIM ** -0.5),
                    regroup=(batch, seq, dilation), side=(((moe_w_gate, moe_w_up, moe_w_down)[g], i),),
                    zero_fill=xb_shape if g == 0 else None)
                qkv, (w_cast,) = res[:2]
                if g == 0 and xb_shape is not None:
                    xb_buf = res[2]
                cast.append(w_cast)
                bias = _band_bias(rel_bias[:, g * DIL_HEADS:(g + 1) * DIL_HEADS], window, dilation)
                o, lse = _dilated_group(qkv, bias)
                outs.append(o)
                lses.append(lse)
            w_gate, w_up, w_down = cast
            merged = _merge_groups(outs, lses)
            xf = _matmul(merged, w_out, residual=xf, out_dtype=jnp.float32)
        last = i + 1 == depth
        side = ()
        if not last:
            j_next = (i + 1) // 2
            side = (((sgu_w_in, j_next), (sgu_w_out, j_next)) if (i + 1) % 2 == 0
                    else ((dil_w_qkv, j_next), (dil_w_out, j_next)))
        res, cast, xb_buf = _moe_layer(xf, ffn_norm[i], final_norm if last else mix_norm[i + 1],
                                       x.dtype if last else bf16, not last,
                                       router_w_group[i], router_b_group[i], router_w_expert[i], router_b_expert[i],
                                       w_gate, w_up, w_down, xb_buf, side)
        if cast:
            dense[i + 1] = tuple(cast)
        if last:
            (out,) = res
        else:
            xf, h = res
    return out.reshape(batch, seq, d)
```

```python
import functools
import math

import jax
import jax.numpy as jnp
from jax import lax
from jax.experimental import pallas as pl
from jax.experimental.pallas import tpu as pltpu

EPS = 1e-6
SGU_CHUNK = 128
SGU_GROUP_DIM = 128
DIL_CONFIGS = ((128, 1), (512, 4), (2048, 16))
DIL_HEADS = 16
HEAD_DIM = 128
DIL_BLOCK = 128
DIL_INNER = DIL_HEADS * HEAD_DIM
NUM_BUCKETS = 32
MAX_DISTANCE = 2048
N_EXPERT_GROUPS = 4
EXPERTS_PER_GROUP = 8
N_EXPERTS = N_EXPERT_GROUPS * EXPERTS_PER_GROUP
TOP_K = 2
MASK_VALUE = -1e30

LANES = 128
MOE_ROWS = 256
VMEM_LIMIT = 52 * 1024 * 1024
W_RING_SLOTS = 3
W_RING_BYTES = 12 * 1024 * 1024


def _params(sem, vmem=VMEM_LIMIT):
    return pltpu.CompilerParams(dimension_semantics=sem, vmem_limit_bytes=vmem)


def _rmsnorm_body(x_ref, g_ref, o_ref):
    x = x_ref[...]
    ms = jnp.mean(x * x, axis=-1, keepdims=True)
    o_ref[...] = (x * lax.rsqrt(ms + EPS) * g_ref[...]).astype(o_ref.dtype)


def _rmsnorm(x, g, out_dtype, rows=512):
    n, d = x.shape
    return pl.pallas_call(
        _rmsnorm_body,
        grid=(n // rows,),
        in_specs=[pl.BlockSpec((rows, d), lambda i: (i, 0)),
                  pl.BlockSpec((1, d), lambda i: (0, 0))],
        out_specs=pl.BlockSpec((rows, d), lambda i: (i, 0)),
        out_shape=jax.ShapeDtypeStruct((n, d), out_dtype),
        compiler_params=_params(("parallel",)),
        name="rmsnorm",
    )(x, g.reshape(1, d))


def _regroup_pitch(dilation):
    return dilation + 4 if dilation % 8 == 0 else dilation


def _matmul_body(*refs, has_bias, use_gelu, has_residual, dilation, n_side, has_zero, col_scale, w_col_of):
    a_ref, w_ref = refs[0], refs[1]
    pos = 2
    if w_col_of is None:
        w = w_ref[...]
    else:
        ring, sem = refs[-2], refs[-1]
        nj = pl.num_programs(1)
        step = pl.program_id(0) * nj + pl.program_id(1)
        n_steps = pl.num_programs(0) * nj
        tn = ring.shape[2]

        def w_copy(of_step):
            col = pl.multiple_of(w_col_of(lax.rem(of_step, nj)) * tn, tn)
            slot = lax.rem(of_step, W_RING_SLOTS)
            return pltpu.make_async_copy(w_ref.at[:, pl.ds(col, tn)], ring.at[slot], sem.at[slot])

        @pl.when(step == 0)
        def _():
            for ahead in range(W_RING_SLOTS - 1):
                @pl.when(ahead < n_steps)
                def _():
                    w_copy(ahead).start()

        @pl.when(step + W_RING_SLOTS - 1 < n_steps)
        def _():
            w_copy(step + W_RING_SLOTS - 1).start()

        w_copy(step).wait()
        w = ring[lax.rem(step, W_RING_SLOTS)]
    acc = jnp.dot(a_ref[...], w.astype(a_ref.dtype), preferred_element_type=jnp.float32)
    if col_scale is not None:
        blocks, scale = col_scale
        acc = acc * jnp.where(pl.program_id(1) < blocks, jnp.float32(scale), jnp.float32(1.0))
    if has_bias:
        acc = acc + refs[pos][...]
        pos += 1
    if use_gelu:
        acc = jax.nn.gelu(acc)
    if has_residual:
        acc = refs[pos][...] + acc
        pos += 1
    side_in = refs[pos:pos + n_side]
    o_ref = refs[pos + n_side]
    side_out = refs[pos + n_side + 1:pos + 2 * n_side + 1]
    pos += 2 * n_side + 1
    zero_ref = None
    if has_zero:
        zero_ref = refs[pos]
        pos += 1
    if dilation == 1:
        o_ref[...] = acc.astype(o_ref.dtype).reshape(o_ref.shape)
    else:
        acc_ref = refs[pos]
        sub = acc.shape[0] // dilation
        pitch = _regroup_pitch(dilation)
        for panel in range(acc_ref.shape[0]):
            cols = slice(panel * LANES, (panel + 1) * LANES)
            if pitch == dilation:
                acc_ref[panel] = acc[:, cols]
            else:
                for l in range(sub):
                    acc_ref[panel, l * pitch:l * pitch + dilation, :] = acc[l * dilation:(l + 1) * dilation, cols]
            for r in range(dilation):
                o_ref[0, r, :, cols] = acc_ref[panel, pl.ds(r, sub, stride=pitch), :].astype(o_ref.dtype)
    for s_in, s_out in zip(side_in, side_out):
        s_out[...] = s_in[...].astype(s_out.dtype)
    if has_zero:
        zero_ref[...] = jnp.zeros_like(zero_ref)


def _matmul(a, w, *, bias=None, use_gelu=False, residual=None, out_dtype, col_block_map=None, n_out=None,
            regroup=None, side=(), zero_fill=None, col_scale=None, tm=1024, tn=512):
    m, k = a.shape
    n = w.shape[1] if n_out is None else n_out
    tm, tn = min(tm, m), min(tn, n)
    grid = (m // tm, n // tn)
    w_col = (lambda j: j) if col_block_map is None else col_block_map
    use_ring = W_RING_SLOTS * k * tn * w.dtype.itemsize <= W_RING_BYTES
    w_spec = pl.BlockSpec(memory_space=pl.ANY) if use_ring else pl.BlockSpec((k, tn), lambda i, j: (0, w_col(j)))
    in_specs = [pl.BlockSpec((tm, k), lambda i, j: (i, 0)), w_spec]
    args = [a, w]
    if bias is not None:
        in_specs.append(pl.BlockSpec((1, tn), lambda i, j: (0, j)))
        args.append(bias.reshape(1, n))
    if residual is not None:
        in_specs.append(pl.BlockSpec((tm, tn), lambda i, j: (i, j)))
        args.append(residual)
    scratch = []
    dilation = 1
    if regroup is None:
        out_specs = [pl.BlockSpec((tm, tn), lambda i, j: (i, j))]
        out_shape = [jax.ShapeDtypeStruct((m, n), out_dtype)]
    else:
        batch, seq, dilation = regroup
        tiles = seq // tm
        out_specs = [pl.BlockSpec((1, dilation, tm // dilation, tn), lambda i, j: (i // tiles, 0, i % tiles, j))]
        out_shape = [jax.ShapeDtypeStruct((batch, dilation, seq // dilation, n), out_dtype)]
        if dilation > 1:
            scratch = [pltpu.VMEM((tn // LANES, tm // dilation * _regroup_pitch(dilation), LANES), jnp.float32)]
    n_steps = grid[0] * grid[1]

    def step_of(i, j):
        return i * grid[1] + j
    s_in, s_args, s_out, s_shape = _side_cast_specs(side, n_steps, step_of, a.dtype)
    in_specs += s_in
    args += s_args
    out_specs += s_out
    out_shape += s_shape
    if zero_fill is not None:
        z_rows, z_cols, z_dtype = zero_fill
        zero_slabs = 1
        while zero_slabs * 2 <= n_steps and z_rows % (zero_slabs * 2) == 0 and (z_rows // (zero_slabs * 2)) % 8 == 0:
            zero_slabs *= 2
        out_specs.append(pl.BlockSpec((z_rows // zero_slabs, z_cols),
                                      lambda i, j: (jnp.minimum(step_of(i, j), zero_slabs - 1), 0)))
        out_shape.append(jax.ShapeDtypeStruct((z_rows, z_cols), z_dtype))
    if use_ring:
        scratch += [pltpu.VMEM((W_RING_SLOTS, k, tn), w.dtype), pltpu.SemaphoreType.DMA((W_RING_SLOTS,))]
    body = functools.partial(_matmul_body, has_bias=bias is not None, use_gelu=use_gelu,
                             has_residual=residual is not None, dilation=dilation, n_side=len(side),
                             has_zero=zero_fill is not None, col_scale=col_scale,
                             w_col_of=w_col if use_ring else None)
    riders = bool(side) or zero_fill is not None
    sequential = riders or use_ring
    outs = pl.pallas_call(
        body,
        grid=grid,
        in_specs=in_specs,
        out_specs=out_specs,
        out_shape=out_shape,
        scratch_shapes=scratch,
        compiler_params=_params(("arbitrary", "arbitrary") if sequential else ("parallel", "arbitrary")),
        name="matmul",
    )(*args)
    if not riders:
        return outs[0]
    cast = [o.reshape(stack.shape[1:]) for o, (stack, _) in zip(outs[1:], side)]
    return (outs[0], cast, outs[-1]) if zero_fill is not None else (outs[0], cast)


def _sgu_gate_body(u_ref, v_ref, gain_ref, vbias_ref, ws_ref, bs_ref, o_ref, *, chunks):
    v = v_ref[...].astype(jnp.float32)
    mu = jnp.mean(v, axis=-1, keepdims=True)
    vc = v - mu
    var = jnp.mean(vc * vc, axis=-1, keepdims=True)
    vn = (vc * lax.rsqrt(var + EPS) * gain_ref[...] + vbias_ref[...]).astype(jnp.bfloat16)
    n_groups = ws_ref.shape[0]
    t_idx = lax.broadcasted_iota(jnp.int32, (SGU_CHUNK, SGU_CHUNK), 0)
    s_idx = lax.broadcasted_iota(jnp.int32, (SGU_CHUNK, SGU_CHUNK), 1)
    causal = s_idx <= t_idx
    for g in range(n_groups):
        cols = slice(g * SGU_GROUP_DIM, (g + 1) * SGU_GROUP_DIM)
        w = jnp.where(causal, ws_ref[g], jnp.zeros((), ws_ref.dtype))
        for c in range(chunks):
            rows = slice(c * SGU_CHUNK, (c + 1) * SGU_CHUNK)
            vm = jnp.dot(w, vn[rows, cols], preferred_element_type=jnp.float32) + bs_ref[:, cols]
            o_ref[rows, cols] = (u_ref[rows, cols].astype(jnp.float32) * vm).astype(o_ref.dtype)


def _sgu_gate(z, v_gain, v_bias, w_s, b_s, chunks=2):
    n, two_w = z.shape
    width = two_w // 2
    rows = chunks * SGU_CHUNK
    n_groups = w_s.shape[0]
    b_full = jnp.repeat(b_s.T, SGU_GROUP_DIM, axis=1)
    body = functools.partial(_sgu_gate_body, chunks=chunks)
    return pl.pallas_call(
        body,
        grid=(n // rows,),
        in_specs=[pl.BlockSpec((rows, width), lambda i: (i, 0)),
                  pl.BlockSpec((rows, width), lambda i: (i, 1)),
                  pl.BlockSpec((1, width), lambda i: (0, 0)),
                  pl.BlockSpec((1, width), lambda i: (0, 0)),
                  pl.BlockSpec((n_groups, SGU_CHUNK, SGU_CHUNK), lambda i: (0, 0, 0)),
                  pl.BlockSpec((SGU_CHUNK, width), lambda i: (0, 0))],
        out_specs=pl.BlockSpec((rows, width), lambda i: (i, 0)),
        out_shape=jax.ShapeDtypeStruct((n, width), jnp.bfloat16),
        compiler_params=_params(("parallel",)),
        name="sgu_gate",
    )(z, z, v_gain.reshape(1, width), v_bias.reshape(1, width), w_s.astype(jnp.bfloat16), b_full)


def _t5_causal_bucket(dist):
    max_exact = NUM_BUCKETS // 2
    d = jnp.maximum(dist, 0)
    df = jnp.maximum(d, 1).astype(jnp.float32)
    large = max_exact + (jnp.log(df / max_exact) / math.log(MAX_DISTANCE / max_exact)
                         * (NUM_BUCKETS - max_exact)).astype(jnp.int32)
    large = jnp.minimum(large, NUM_BUCKETS - 1)
    return jnp.where(d < max_exact, d, large)


def _band_bias(bias_table, window, dilation):
    span = window // dilation
    qi = jnp.arange(DIL_BLOCK)[:, None]
    kj = jnp.arange(2 * DIL_BLOCK)[None, :]
    sub_dist = qi + DIL_BLOCK - kj
    band = (sub_dist >= 0) & (sub_dist <= span)
    bucket = _t5_causal_bucket(sub_dist * dilation)
    onehot = (bucket[:, :, None] == jnp.arange(NUM_BUCKETS)).astype(jnp.float32)
    bias = jnp.einsum('qkb,bh->hqk', onehot, bias_table.astype(jnp.float32), precision=lax.Precision.HIGHEST)
    bias = jnp.where(band[None], bias, MASK_VALUE)
    return jnp.stack([bias, jnp.where(kj[None] < DIL_BLOCK, MASK_VALUE, bias)])


def _attn_body(q_ref, kp_ref, kc_ref, vp_ref, vc_ref, bias_ref, o_ref, lse_ref):
    lane = lax.broadcasted_iota(jnp.int32, (DIL_BLOCK, LANES), 1)
    for qb in range(q_ref.shape[2] // DIL_BLOCK):
        rows = slice(qb * DIL_BLOCK, (qb + 1) * DIL_BLOCK)
        before = slice((qb - 1) * DIL_BLOCK, qb * DIL_BLOCK)
        table = jnp.where(pl.program_id(2) == 0, 1, 0) if qb == 0 else 0
        lse_all = jnp.zeros((DIL_BLOCK, LANES), jnp.float32)
        for h in range(DIL_HEADS):
            cols = slice(h * HEAD_DIM, (h + 1) * HEAD_DIM)
            q = q_ref[0, 0, rows, cols]
            k_prev = kp_ref[0, 0, :, cols] if qb == 0 else kc_ref[0, 0, before, cols]
            v_prev = vp_ref[0, 0, :, cols] if qb == 0 else vc_ref[0, 0, before, cols]
            k = jnp.concatenate([k_prev, kc_ref[0, 0, rows, cols]], axis=0)
            v = jnp.concatenate([v_prev, vc_ref[0, 0, rows, cols]], axis=0)
            s = lax.dot_general(q, k, (((1,), (1,)), ((), ())), preferred_element_type=jnp.float32)
            s = s + bias_ref[table, h]
            m = jnp.max(s, axis=-1, keepdims=True)
            p = jnp.exp(s - m)
            den = jnp.sum(p, axis=-1, keepdims=True)
            o = jnp.dot(p.astype(jnp.bfloat16), v, preferred_element_type=jnp.float32) / den
            o_ref[0, 0, rows, cols] = o.astype(o_ref.dtype)
            lse_all = jnp.where(lane == h, m + jnp.log(den), lse_all)
        lse_ref[0, 0, rows, :] = lse_all


def _dilated_group(qkv, band_bias):
    batch, dilation, sub_len, _ = qkv.shape
    nb = sub_len // DIL_BLOCK
    per_step = next(c for c in (4, 2, 1) if nb % c == 0)
    rows = per_step * DIL_BLOCK

    def spec(part, prev):
        if prev:
            return pl.BlockSpec((1, 1, DIL_BLOCK, DIL_INNER),
                                lambda b, r, n: (b, r, jnp.maximum(per_step * n - 1, 0), part))
        return pl.BlockSpec((1, 1, rows, DIL_INNER), lambda b, r, n: (b, r, n, part))

    return pl.pallas_call(
        _attn_body,
        grid=(batch, dilation, nb // per_step),
        in_specs=[spec(0, False), spec(1, True), spec(1, False), spec(2, True), spec(2, False),
                  pl.BlockSpec((2, DIL_HEADS, DIL_BLOCK, 2 * DIL_BLOCK), lambda b, r, n: (0, 0, 0, 0))],
        out_specs=[pl.BlockSpec((1, 1, rows, DIL_INNER), lambda b, r, n: (b, r, n, 0)),
                   pl.BlockSpec((1, 1, rows, LANES), lambda b, r, n: (b, r, n, 0))],
        out_shape=[jax.ShapeDtypeStruct((batch, dilation, sub_len, DIL_INNER), jnp.bfloat16),
                   jax.ShapeDtypeStruct((batch, dilation, sub_len, LANES), jnp.float32)],
        compiler_params=_params(("parallel", "parallel", "arbitrary")),
        name="dilated_attn",
    )(qkv, qkv, qkv, qkv, qkv, band_bias)


def _merge_body(*refs, dilations):
    n_g = len(dilations)
    o_refs, l_refs = refs[:n_g], refs[n_g:2 * n_g]
    out_ref, ltok, wtok, acc = refs[2 * n_g:]
    rows = out_ref.shape[0]

    def residue_rows(r, d):
        return pl.ds(r, rows // d, stride=d) if d > 1 else slice(None)

    for g, d in enumerate(dilations):
        for r in range(d):
            ltok[g, residue_rows(r, d), :] = l_refs[g][0, r]
    lse = [ltok[g] for g in range(n_g)]
    m = functools.reduce(jnp.maximum, lse)
    e = [jnp.exp(l - m) for l in lse]
    tot = functools.reduce(jnp.add, e)
    for g in range(n_g):
        wtok[g] = e[g] / tot
    for g, d in enumerate(dilations):
        for r in range(d):
            rr = residue_rows(r, d)
            w = wtok[g, rr, :]
            for h in range(DIL_HEADS):
                cols = slice(h * HEAD_DIM, (h + 1) * HEAD_DIM)
                term = o_refs[g][0, r, :, cols].astype(jnp.float32) * w[:, h:h + 1]
                if g == 0:
                    acc[h, rr, :] = term
                else:
                    acc[h, rr, :] = acc[h, rr, :] + term
    for h in range(DIL_HEADS):
        out_ref[:, h * HEAD_DIM:(h + 1) * HEAD_DIM] = acc[h].astype(out_ref.dtype)


def _merge_groups(outs, lses, rows=512):
    batch = outs[0].shape[0]
    dilations = tuple(o.shape[1] for o in outs)
    seq = outs[0].shape[1] * outs[0].shape[2]
    tiles = seq // rows

    def spec(d, width):
        return pl.BlockSpec((1, d, rows // d, width), lambda i: (i // tiles, 0, i % tiles, 0))

    n_g = len(outs)
    return pl.pallas_call(
        functools.partial(_merge_body, dilations=dilations),
        grid=(batch * tiles,),
        in_specs=[spec(d, DIL_INNER) for d in dilations] + [spec(d, LANES) for d in dilations],
        out_specs=pl.BlockSpec((rows, DIL_INNER), lambda i: (i, 0)),
        out_shape=jax.ShapeDtypeStruct((batch * seq, DIL_INNER), jnp.bfloat16),
        scratch_shapes=[pltpu.VMEM((n_g, rows, LANES), jnp.float32),
                        pltpu.VMEM((n_g, rows, LANES), jnp.float32),
                        pltpu.VMEM((DIL_HEADS, rows, HEAD_DIM), jnp.float32)],
        compiler_params=_params(("parallel",)),
        name="merge_groups",
    )(*outs, *lses)


def _pack_halves(v):
    c = v.shape[1] // 2
    r = v.astype(jnp.bfloat16).astype(jnp.float32)
    lo = lax.bitcast_convert_type(r[:, :c], jnp.uint32)
    hi = lax.bitcast_convert_type(r[:, c:], jnp.uint32)
    return (lo >> 16) | (hi & jnp.uint32(0xFFFF0000))


def _unpack_halves(w):
    lo = lax.bitcast_convert_type(w << 16, jnp.float32)
    hi = lax.bitcast_convert_type(w & jnp.uint32(0xFFFF0000), jnp.float32)
    return jnp.concatenate([lo, hi], axis=1)


def _router_body(x_ref, g_ref, w_ref, b_ref, o_ref, hp_ref, cnt_ref, carry):
    i = pl.program_id(0)

    @pl.when(i == 0)
    def _():
        carry[...] = jnp.zeros_like(carry)

    x = x_ref[...]
    ms = jnp.mean(x * x, axis=-1, keepdims=True)
    hf = x * lax.rsqrt(ms + EPS) * g_ref[...]
    hp_ref[...] = _pack_halves(hf)
    logits = jnp.dot(hf.astype(jnp.bfloat16), w_ref[...], preferred_element_type=jnp.float32) + b_ref[...]
    rows = logits.shape[0]
    lane = lax.broadcasted_iota(jnp.int32, logits.shape, 1)
    neg = -jnp.inf
    gl = jnp.where(lane < N_EXPERT_GROUPS, logits, neg)
    gmax = jnp.max(gl, axis=-1, keepdims=True)
    g_idx = jnp.min(jnp.where(gl == gmax, lane, LANES), axis=-1, keepdims=True)
    g_w = 1.0 / jnp.sum(jnp.exp(gl - gmax), axis=-1, keepdims=True)
    lo = N_EXPERT_GROUPS + EXPERTS_PER_GROUP * g_idx
    el = jnp.where(jnp.logical_and(lane >= lo, lane < lo + EXPERTS_PER_GROUP), logits, neg)
    v1 = jnp.max(el, axis=-1, keepdims=True)
    i1 = jnp.min(jnp.where(el == v1, lane, LANES), axis=-1, keepdims=True)
    el2 = jnp.where(lane == i1, neg, el)
    v2 = jnp.max(el2, axis=-1, keepdims=True)
    i2 = jnp.min(jnp.where(el2 == v2, lane, LANES), axis=-1, keepdims=True)
    e2 = jnp.exp(v2 - v1)
    p1 = 1.0 / (1.0 + e2)
    p2 = e2 / (1.0 + e2)
    oh1 = lane == i1 - N_EXPERT_GROUPS
    oh2 = lane == i2 - N_EXPERT_GROUPS
    oh1f, oh2f = oh1.astype(jnp.float32), oh2.astype(jnp.float32)
    before = (lax.broadcasted_iota(jnp.int32, (rows, rows), 1)
              < lax.broadcasted_iota(jnp.int32, (rows, rows), 0)).astype(jnp.bfloat16)
    pre1 = jnp.dot(before, oh1f.astype(jnp.bfloat16), preferred_element_type=jnp.float32)
    pre2 = jnp.dot(before, oh2f.astype(jnp.bfloat16), preferred_element_type=jnp.float32)
    cnt1 = jnp.sum(oh1f, axis=0, keepdims=True)
    cnt2 = jnp.sum(oh2f, axis=0, keepdims=True)
    base = carry[...]
    rank1 = jnp.sum(jnp.where(oh1, pre1 + base, 0.0), axis=-1, keepdims=True)
    rank2 = jnp.sum(jnp.where(oh2, pre2 + (base + cnt1), 0.0), axis=-1, keepdims=True)
    total = base + cnt1 + cnt2
    carry[...] = total
    cnt_ref[...] = total
    out = jnp.where(lane == 0, (i1 - N_EXPERT_GROUPS).astype(jnp.float32),
          jnp.where(lane == 1, (i2 - N_EXPERT_GROUPS).astype(jnp.float32),
          jnp.where(lane == 2, g_w * p1,
          jnp.where(lane == 3, g_w * p2,
          jnp.where(lane == 4, rank1,
          jnp.where(lane == 5, rank2, 0.0))))))
    o_ref[...] = out


def _router(x, g, w_rg, b_rg, w_re, b_re, rows=512):
    n, d = x.shape
    pad = LANES - N_EXPERT_GROUPS - N_EXPERTS
    w = jnp.concatenate([w_rg, w_re, jnp.zeros((d, pad), w_rg.dtype)], axis=1).astype(jnp.bfloat16)
    b = jnp.concatenate([b_rg, b_re, jnp.zeros((pad,), b_rg.dtype)]).reshape(1, LANES)
    return pl.pallas_call(
        _router_body,
        grid=(n // rows,),
        in_specs=[pl.BlockSpec((rows, d), lambda i: (i, 0)),
                  pl.BlockSpec((1, d), lambda i: (0, 0)),
                  pl.BlockSpec((d, LANES), lambda i: (0, 0)),
                  pl.BlockSpec((1, LANES), lambda i: (0, 0))],
        out_specs=[pl.BlockSpec((rows, LANES), lambda i: (i, 0)),
                   pl.BlockSpec((rows, d // 2), lambda i: (i, 0)),
                   pl.BlockSpec((1, LANES), lambda i: (0, 0))],
        out_shape=[jax.ShapeDtypeStruct((n, LANES), jnp.float32),
                   jax.ShapeDtypeStruct((n, d // 2), jnp.uint32),
                   jax.ShapeDtypeStruct((1, LANES), jnp.float32)],
        scratch_shapes=[pltpu.VMEM((1, LANES), jnp.float32)],
        compiler_params=_params(("arbitrary",)),
        name="router",
    )(x, g.reshape(1, d), w, b)


def _dispatch_body(s1_ref, s2_ref, hp_ref, xb_in_hbm, xb_hbm, stage, sem):
    del xb_in_hbm
    i = pl.program_id(0)
    n_steps = pl.num_programs(0)
    rows = s1_ref.shape[2]
    slot = i % 2

    def wait_slot(of_slot):
        for _ in range(TOP_K):
            pltpu.make_async_copy(stage.at[of_slot], stage.at[of_slot], sem.at[of_slot]).wait()

    @pl.when(i >= 2)
    def _():
        wait_slot(slot)

    stage[slot] = hp_ref[...]

    def issue(r, carry):
        src = stage.at[slot, pl.ds(r, 1)]
        pltpu.make_async_copy(src, xb_hbm.at[pl.ds(s1_ref[0, 0, r], 1)], sem.at[slot]).start()
        pltpu.make_async_copy(src, xb_hbm.at[pl.ds(s2_ref[0, 0, r], 1)], sem.at[slot]).start()
        return carry
    lax.fori_loop(0, rows, issue, 0, unroll=8)

    @pl.when(i == n_steps - 1)
    def _():
        @pl.when(i >= 1)
        def _():
            wait_slot(1 - slot)
        wait_slot(slot)


def _dispatch(hp, slot1, slot2, xb_init, rows=512):
    n, c = hp.shape
    p_total = xb_init.shape[0]
    steps = n // rows
    tab = pl.BlockSpec((1, 1, rows), lambda i: (i, 0, 0), memory_space=pltpu.SMEM)
    any_spec = pl.BlockSpec(memory_space=pl.ANY)
    return pl.pallas_call(
        _dispatch_body,
        grid=(steps,),
        in_specs=[tab, tab, pl.BlockSpec((rows, c), lambda i: (i, 0)), any_spec],
        out_specs=any_spec,
        out_shape=jax.ShapeDtypeStruct((p_total, c), hp.dtype),
        scratch_shapes=[pltpu.VMEM((2, rows, c), hp.dtype),
                        pltpu.SemaphoreType.DMA((2,))],
        input_output_aliases={3: 0},
        compiler_params=_params(("arbitrary",)),
        name="moe_dispatch",
    )(slot1.reshape(steps, 1, rows), slot2.reshape(steps, 1, rows), hp, xb_init)


def _side_cast_specs(side, n_steps, index_of_step, dtype):
    in_specs, args, out_specs, out_shape = [], [], [], []
    for stack, layer in side:
        cols = stack.shape[-1]
        flat = stack.reshape(-1, cols)
        layer_rows = flat.shape[0] // stack.shape[0]
        n_slabs = 1
        while (n_slabs * 2 <= n_steps and layer_rows % (n_slabs * 2) == 0
               and (layer_rows // (n_slabs * 2)) % 16 == 0):
            n_slabs *= 2
        slab_rows = layer_rows // n_slabs

        def slab(*idx, last=n_slabs - 1):
            return jnp.minimum(index_of_step(*idx), last)
        in_specs.append(pl.BlockSpec((slab_rows, cols),
                                     lambda *idx, slab=slab, first=layer * n_slabs: (first + slab(*idx), 0)))
        args.append(flat)
        out_specs.append(pl.BlockSpec((slab_rows, cols), lambda *idx, slab=slab: (slab(*idx), 0)))
        out_shape.append(jax.ShapeDtypeStruct((layer_rows, cols), dtype))
    return in_specs, args, out_specs, out_shape


def _cast_body(*refs):
    n_side = len(refs) // 2
    for s_in, s_out in zip(refs[:n_side], refs[n_side:]):
        s_out[...] = s_in[...].astype(s_out.dtype)


def _cast_layers(side, dtype, n_steps=32):
    s_in, s_args, s_out, s_shape = _side_cast_specs(side, n_steps, lambda i: i, dtype)
    outs = pl.pallas_call(
        _cast_body,
        grid=(n_steps,),
        in_specs=s_in,
        out_specs=s_out,
        out_shape=s_shape,
        compiler_params=_params(("arbitrary",)),
        name="cast",
    )(*s_args)
    return [o.reshape(stack.shape[1:]) for o, (stack, _) in zip(outs, side)]


def _expert_body(blk_e_ref, nvalid_ref, xb_ref, wg_ref, wu_ref, wd_ref, *refs):
    del blk_e_ref
    n_side = (len(refs) - 1) // 2
    side_in, yb_ref, side_out = refs[:n_side], refs[n_side], refs[n_side + 1:]
    i = pl.program_id(0)
    for s_in, s_out in zip(side_in, side_out):
        s_out[...] = s_in[...].astype(s_out.dtype)

    @pl.when(nvalid_ref[i] > 0)
    def _():
        h = _unpack_halves(xb_ref[...]).astype(jnp.bfloat16)
        a = jax.nn.silu(jnp.dot(h, wg_ref[0], preferred_element_type=jnp.float32))
        a = a * jnp.dot(h, wu_ref[0], preferred_element_type=jnp.float32)
        y = jnp.dot(a.astype(jnp.bfloat16), wd_ref[0], preferred_element_type=jnp.float32)
        yb_ref[...] = _pack_halves(y)

    @pl.when(nvalid_ref[i] == 0)
    def _():
        yb_ref[...] = jnp.zeros_like(yb_ref)


def _moe_experts(xb, blk_e, nvalid, w_gate, w_up, w_down, side=()):
    p_total, c = xb.shape
    d = 2 * c
    d_e = w_gate.shape[-1]
    n_blk = p_total // MOE_ROWS
    s_in, s_args, s_out, s_shape = _side_cast_specs(side, n_blk, lambda i, *_: i, jnp.bfloat16)
    grid_spec = pltpu.PrefetchScalarGridSpec(
        num_scalar_prefetch=2,
        grid=(n_blk,),
        in_specs=[pl.BlockSpec((MOE_ROWS, c), lambda i, e, nv: (i, 0)),
                  pl.BlockSpec((1, d, d_e), lambda i, e, nv: (e[i], 0, 0)),
                  pl.BlockSpec((1, d, d_e), lambda i, e, nv: (e[i], 0, 0)),
                  pl.BlockSpec((1, d_e, d), lambda i, e, nv: (e[i], 0, 0))] + s_in,
        out_specs=[pl.BlockSpec((MOE_ROWS, c), lambda i, e, nv: (i, 0))] + s_out,
    )
    outs = pl.pallas_call(
        _expert_body,
        grid_spec=grid_spec,
        out_shape=[jax.ShapeDtypeStruct((p_total, c), xb.dtype)] + s_shape,
        compiler_params=_params(("arbitrary",)),
        name="moe_experts",
    )(blk_e, nvalid, xb, w_gate, w_up, w_down, *s_args)
    return outs[0], [o.reshape(stack.shape[1:]) for o, (stack, _) in zip(outs[1:], side)]


def _combine_body(s1_ref, s2_ref, s1n_ref, s2n_ref, x_ref, routed_ref, g_ref, yb_hbm, *refs, emit_sum):
    out_refs, ybuf, sem = refs[:-2], refs[-2], refs[-1]
    i = pl.program_id(0)
    n_steps = pl.num_programs(0)
    slot = i % 2
    rows = x_ref.shape[0]

    def gather(t1_ref, t2_ref, to_slot):
        def issue(r, carry):
            pltpu.make_async_copy(yb_hbm.at[pl.ds(t1_ref[0, 0, r], 1)], ybuf.at[to_slot, pl.ds(r, 1)],
                                  sem.at[to_slot]).start()
            pltpu.make_async_copy(yb_hbm.at[pl.ds(t2_ref[0, 0, r], 1)], ybuf.at[to_slot, pl.ds(rows + r, 1)],
                                  sem.at[to_slot]).start()
            return carry
        lax.fori_loop(0, rows, issue, 0, unroll=8)

    @pl.when(i == 0)
    def _():
        gather(s1_ref, s2_ref, 0)

    @pl.when(i + 1 < n_steps)
    def _():
        gather(s1n_ref, s2n_ref, 1 - slot)

    pltpu.make_async_copy(ybuf.at[slot], ybuf.at[slot], sem.at[slot]).wait()
    y1 = _unpack_halves(ybuf[slot, :rows])
    y2 = _unpack_halves(ybuf[slot, rows:])
    s = x_ref[...] + (y1 * routed_ref[:, 2:3] + y2 * routed_ref[:, 3:4])
    if emit_sum:
        out_refs[0][...] = s
    ms = jnp.mean(s * s, axis=-1, keepdims=True)
    out_refs[-1][...] = (s * lax.rsqrt(ms + EPS) * g_ref[...]).astype(out_refs[-1].dtype)


def _combine_norm(x, yb, routed, slot1, slot2, g, norm_dtype, emit_sum, rows=256):
    n, d = x.shape
    steps = n // rows
    row_spec = pl.BlockSpec((rows, d), lambda i: (i, 0))
    tab = pl.BlockSpec((1, 1, rows), lambda i: (i, 0, 0), memory_space=pltpu.SMEM)
    tab_next = pl.BlockSpec((1, 1, rows), lambda i: (jnp.minimum(i + 1, steps - 1), 0, 0),
                            memory_space=pltpu.SMEM)
    out_specs = [row_spec]
    out_shape = [jax.ShapeDtypeStruct((n, d), norm_dtype)]
    if emit_sum:
        out_specs = [row_spec] + out_specs
        out_shape = [jax.ShapeDtypeStruct((n, d), x.dtype)] + out_shape
    s1 = slot1.reshape(steps, 1, rows)
    s2 = slot2.reshape(steps, 1, rows)
    return pl.pallas_call(
        functools.partial(_combine_body, emit_sum=emit_sum),
        grid=(steps,),
        in_specs=[tab, tab, tab_next, tab_next, row_spec,
                  pl.BlockSpec((rows, LANES), lambda i: (i, 0)),
                  pl.BlockSpec((1, d), lambda i: (0, 0)),
                  pl.BlockSpec(memory_space=pl.ANY)],
        out_specs=out_specs,
        out_shape=out_shape,
        scratch_shapes=[pltpu.VMEM((2, TOP_K * rows, d // 2), yb.dtype),
                        pltpu.SemaphoreType.DMA((2,))],
        compiler_params=_params(("arbitrary",)),
        name="combine_norm",
    )(s1, s2, s1, s2, x, routed, g.reshape(1, d), yb)


def _moe_slot_rows(n):
    return n * TOP_K + N_EXPERTS * MOE_ROWS


def _moe_layer(x, ffn_g, next_g, norm_dtype, emit_sum, w_rg, b_rg, w_re, b_re, w_gate, w_up, w_down, xb_init,
               side=()):
    n, d = x.shape
    routed, hp, counts = _router(x, ffn_g, w_rg, b_rg, w_re, b_re)

    n_blk = _moe_slot_rows(n) // MOE_ROWS
    experts = jnp.arange(N_EXPERTS, dtype=jnp.int32)
    counts = counts[0, :N_EXPERTS].astype(jnp.int32)
    padded = ((counts + MOE_ROWS - 1) // MOE_ROWS) * MOE_ROWS
    pends = jnp.cumsum(padded)
    pstarts = pends - padded
    expert_id = routed[:, 0:TOP_K].astype(jnp.int32)
    rank = routed[:, 2 * TOP_K:3 * TOP_K].astype(jnp.int32)
    slot = jnp.sum(jnp.where(expert_id[:, :, None] == experts, pstarts, 0), axis=-1) + rank
    blk_start = jnp.arange(n_blk, dtype=jnp.int32) * MOE_ROWS
    blk_e = jnp.minimum(jnp.sum(blk_start[:, None] >= pends[None, :], axis=1), N_EXPERTS - 1).astype(jnp.int32)
    seg_end = jnp.sum(jnp.where(blk_e[:, None] == experts, pstarts + counts, 0), axis=1)
    nvalid = jnp.clip(seg_end - blk_start, 0, MOE_ROWS).astype(jnp.int32)

    xb = _dispatch(hp, slot[:, 0], slot[:, 1], xb_init)
    yb, cast = _moe_experts(xb, blk_e, nvalid, w_gate, w_up, w_down, side)
    return _combine_norm(x, yb, routed, slot[:, 0], slot[:, 1], next_g, norm_dtype, emit_sum), cast, xb


def kernel(x, mix_norm, ffn_norm, sgu_w_in, sgu_b_in, sgu_v_gain, sgu_v_bias, sgu_w_spatial, sgu_b_spatial,
           sgu_w_out, dil_w_qkv, dil_w_out, rel_bias, router_w_group, router_b_group, router_w_expert,
           router_b_expert, moe_w_gate, moe_w_up, moe_w_down, final_norm):
    batch, seq, d = x.shape
    n = batch * seq
    bf16 = jnp.bfloat16
    xf = x.reshape(n, d)
    depth = mix_norm.shape[0]
    n_dil = len(DIL_CONFIGS)
    h = _rmsnorm(xf, mix_norm[0], bf16)
    out = None
    dense = {}
    xb_buf = None
    for i in range(depth):
        j = i // 2
        xb_shape = (_moe_slot_rows(n), d // 2, jnp.uint32) if xb_buf is None else None
        if i % 2 == 0:
            riders = ((moe_w_gate, i), (moe_w_up, i))
            if i in dense:
                w_in, w_out = dense[i]
            else:
                (w_in,), w_out = _cast_layers(((sgu_w_in, j),), bf16), None
                riders += ((sgu_w_out, j),)
            res = _matmul(h, w_in, bias=sgu_b_in[j], use_gelu=True, out_dtype=bf16, side=riders, zero_fill=xb_shape)
            z, (w_gate, w_up, *rest) = res[:2]
            w_out = rest[0] if rest else w_out
            if xb_shape is not None:
                xb_buf = res[2]
            y = _sgu_gate(z, sgu_v_gain[j], sgu_v_bias[j], sgu_w_spatial[j], sgu_b_spatial[j])
            xf, (w_down,) = _matmul(y, w_out, residual=xf, out_dtype=jnp.float32, side=((moe_w_down, i),))
        else:
            if i not in dense:
                dense[i] = tuple(_cast_layers(((dil_w_qkv, j), (dil_w_out, j)), bf16))
            w_qkv, w_out = dense[i]
            outs, lses, cast = [], [], []
            tn = 512
            per_part = DIL_INNER // tn
            for g, (window, dilation) in enumerate(DIL_CONFIGS):
                res = _matmul(
                    h, w_qkv, out_dtype=bf16, n_out=3 * DIL_INNER, tn=tn,
                    col_block_map=lambda c, g=g: ((c // per_part) * n_dil + g) * per_part + c % per_part,
                    col_scale=(per_part, HEAD_DIM ** -0.5),
                    regroup=(batch, seq, dilation), side=(((moe_w_gate, moe_w_up, moe_w_down)[g], i),),
                    zero_fill=xb_shape if g == 0 else None)
                qkv, (w_cast,) = res[:2]
                if g == 0 and xb_shape is not None:
                    xb_buf = res[2]
                cast.append(w_cast)
                bias = _band_bias(rel_bias[:, g * DIL_HEADS:(g + 1) * DIL_HEADS], window, dilation)
                o, lse = _dilated_group(qkv, bias)
                outs.append(o)
                lses.append(lse)
            w_gate, w_up, w_down = cast
            merged = _merge_groups(outs, lses)
            xf = _matmul(merged, w_out, residual=xf, out_dtype=jnp.float32)
        last = i + 1 == depth
        side = ()
        if not last:
            j_next = (i + 1) // 2
            side = (((sgu_w_in, j_next), (sgu_w_out, j_next)) if (i + 1) % 2 == 0
                    else ((dil_w_qkv, j_next), (dil_w_out, j_next)))
        res, cast, xb_buf = _moe_layer(xf, ffn_norm[i], final_norm if last else mix_norm[i + 1],
                                       x.dtype if last else bf16, not last,
                                       router_w_group[i], router_b_group[i], router_w_expert[i], router_b_expert[i],
                                       w_gate, w_up, w_down, xb_buf, side)
        if cast:
            dense[i + 1] = tuple(cast)
        if last:
            (out,) = res
        else:
            xf, h = res
    return out.reshape(batch, seq, d)
```

```python
import functools
import math

import jax
import jax.numpy as jnp
from jax import lax
from jax.experimental import pallas as pl
from jax.experimental.pallas import tpu as pltpu

EPS = 1e-6
SGU_CHUNK = 128
SGU_GROUP_DIM = 128
DIL_CONFIGS = ((128, 1), (512, 4), (2048, 16))
DIL_HEADS = 16
HEAD_DIM = 128
DIL_BLOCK = 128
DIL_INNER = DIL_HEADS * HEAD_DIM
NUM_BUCKETS = 32
MAX_DISTANCE = 2048
N_EXPERT_GROUPS = 4
EXPERTS_PER_GROUP = 8
N_EXPERTS = N_EXPERT_GROUPS * EXPERTS_PER_GROUP
TOP_K = 2
MASK_VALUE = -1e30

LANES = 128
MOE_ROWS = 256
VMEM_LIMIT = 52 * 1024 * 1024
W_RING_SLOTS = 3
W_RING_BYTES = 12 * 1024 * 1024


def _params(sem, vmem=VMEM_LIMIT):
    return pltpu.CompilerParams(dimension_semantics=sem, vmem_limit_bytes=vmem)


def _rmsnorm_body(x_ref, g_ref, o_ref):
    x = x_ref[...]
    ms = jnp.mean(x * x, axis=-1, keepdims=True)
    o_ref[...] = (x * lax.rsqrt(ms + EPS) * g_ref[...]).astype(o_ref.dtype)


def _rmsnorm(x, g, out_dtype, rows=512):
    n, d = x.shape
    return pl.pallas_call(
        _rmsnorm_body,
        grid=(n // rows,),
        in_specs=[pl.BlockSpec((rows, d), lambda i: (i, 0)),
                  pl.BlockSpec((1, d), lambda i: (0, 0))],
        out_specs=pl.BlockSpec((rows, d), lambda i: (i, 0)),
        out_shape=jax.ShapeDtypeStruct((n, d), out_dtype),
        compiler_params=_params(("parallel",)),
        name="rmsnorm",
    )(x, g.reshape(1, d))


def _regroup_pitch(dilation):
    return dilation + 4 if dilation % 8 == 0 else dilation


def _matmul_body(*refs, has_bias, use_gelu, has_residual, dilation, n_side, has_zero, col_scale, w_col_of):
    a_ref, w_ref = refs[0], refs[1]
    pos = 2
    if w_col_of is None:
        w = w_ref[...]
    else:
        ring, sem = refs[-2], refs[-1]
        nj = pl.num_programs(1)
        step = pl.program_id(0) * nj + pl.program_id(1)
        n_steps = pl.num_programs(0) * nj
        tn = ring.shape[2]

        def w_copy(of_step):
            col = pl.multiple_of(w_col_of(lax.rem(of_step, nj)) * tn, tn)
            slot = lax.rem(of_step, W_RING_SLOTS)
            return pltpu.make_async_copy(w_ref.at[:, pl.ds(col, tn)], ring.at[slot], sem.at[slot])

        @pl.when(step == 0)
        def _():
            for ahead in range(W_RING_SLOTS - 1):
                @pl.when(ahead < n_steps)
                def _():
                    w_copy(ahead).start()

        @pl.when(step + W_RING_SLOTS - 1 < n_steps)
        def _():
            w_copy(step + W_RING_SLOTS - 1).start()

        w_copy(step).wait()
        w = ring[lax.rem(step, W_RING_SLOTS)]
    first_side = 2 + int(has_bias) + int(has_residual)
    for s_in, s_out in zip(refs[first_side:first_side + n_side],
                           refs[first_side + n_side + 1:first_side + 2 * n_side + 1]):
        s_out[...] = s_in[...].astype(s_out.dtype)
    if has_zero:
        zero_out = refs[first_side + 2 * n_side + 1]
        zero_out[...] = jnp.zeros_like(zero_out)
    acc = jnp.dot(a_ref[...], w.astype(a_ref.dtype), preferred_element_type=jnp.float32)
    if col_scale is not None:
        blocks, scale = col_scale
        acc = acc * jnp.where(pl.program_id(1) < blocks, jnp.float32(scale), jnp.float32(1.0))
    if has_bias:
        acc = acc + refs[pos][...]
        pos += 1
    if use_gelu:
        acc = jax.nn.gelu(acc)
    if has_residual:
        acc = refs[pos][...] + acc
        pos += 1
    o_ref = refs[pos + n_side]
    pos += 2 * n_side + 1 + int(has_zero)
    if dilation == 1:
        o_ref[...] = acc.astype(o_ref.dtype).reshape(o_ref.shape)
    else:
        acc_ref = refs[pos]
        sub = acc.shape[0] // dilation
        pitch = _regroup_pitch(dilation)
        for panel in range(acc_ref.shape[0]):
            cols = slice(panel * LANES, (panel + 1) * LANES)
            if pitch == dilation:
                acc_ref[panel] = acc[:, cols]
            else:
                for l in range(sub):
                    acc_ref[panel, l * pitch:l * pitch + dilation, :] = acc[l * dilation:(l + 1) * dilation, cols]
            for r in range(dilation):
                o_ref[0, r, :, cols] = acc_ref[panel, pl.ds(r, sub, stride=pitch), :].astype(o_ref.dtype)


def _matmul(a, w, *, bias=None, use_gelu=False, residual=None, out_dtype, col_block_map=None, n_out=None,
            regroup=None, side=(), zero_fill=None, col_scale=None, tm=1024, tn=512):
    m, k = a.shape
    n = w.shape[1] if n_out is None else n_out
    tm, tn = min(tm, m), min(tn, n)
    grid = (m // tm, n // tn)
    w_col = (lambda j: j) if col_block_map is None else col_block_map
    use_ring = W_RING_SLOTS * k * tn * w.dtype.itemsize <= W_RING_BYTES
    w_spec = pl.BlockSpec(memory_space=pl.ANY) if use_ring else pl.BlockSpec((k, tn), lambda i, j: (0, w_col(j)))
    in_specs = [pl.BlockSpec((tm, k), lambda i, j: (i, 0)), w_spec]
    args = [a, w]
    if bias is not None:
        in_specs.append(pl.BlockSpec((1, tn), lambda i, j: (0, j)))
        args.append(bias.reshape(1, n))
    if residual is not None:
        in_specs.append(pl.BlockSpec((tm, tn), lambda i, j: (i, j)))
        args.append(residual)
    scratch = []
    dilation = 1
    if regroup is None:
        out_specs = [pl.BlockSpec((tm, tn), lambda i, j: (i, j))]
        out_shape = [jax.ShapeDtypeStruct((m, n), out_dtype)]
    else:
        batch, seq, dilation = regroup
        tiles = seq // tm
        out_specs = [pl.BlockSpec((1, dilation, tm // dilation, tn), lambda i, j: (i // tiles, 0, i % tiles, j))]
        out_shape = [jax.ShapeDtypeStruct((batch, dilation, seq // dilation, n), out_dtype)]
        if dilation > 1:
            scratch = [pltpu.VMEM((tn // LANES, tm // dilation * _regroup_pitch(dilation), LANES), jnp.float32)]
    n_steps = grid[0] * grid[1]

    def step_of(i, j):
        return i * grid[1] + j
    s_in, s_args, s_out, s_shape = _side_cast_specs(side, n_steps, step_of, a.dtype)
    in_specs += s_in
    args += s_args
    out_specs += s_out
    out_shape += s_shape
    if zero_fill is not None:
        z_rows, z_cols, z_dtype = zero_fill
        zero_slabs = 1
        while zero_slabs * 2 <= n_steps and z_rows % (zero_slabs * 2) == 0 and (z_rows // (zero_slabs * 2)) % 8 == 0:
            zero_slabs *= 2
        out_specs.append(pl.BlockSpec((z_rows // zero_slabs, z_cols),
                                      lambda i, j: (jnp.minimum(step_of(i, j), zero_slabs - 1), 0)))
        out_shape.append(jax.ShapeDtypeStruct((z_rows, z_cols), z_dtype))
    if use_ring:
        scratch += [pltpu.VMEM((W_RING_SLOTS, k, tn), w.dtype), pltpu.SemaphoreType.DMA((W_RING_SLOTS,))]
    body = functools.partial(_matmul_body, has_bias=bias is not None, use_gelu=use_gelu,
                             has_residual=residual is not None, dilation=dilation, n_side=len(side),
                             has_zero=zero_fill is not None, col_scale=col_scale,
                             w_col_of=w_col if use_ring else None)
    riders = bool(side) or zero_fill is not None
    sequential = riders or use_ring
    outs = pl.pallas_call(
        body,
        grid=grid,
        in_specs=in_specs,
        out_specs=out_specs,
        out_shape=out_shape,
        scratch_shapes=scratch,
        compiler_params=_params(("arbitrary", "arbitrary") if sequential else ("parallel", "arbitrary")),
        name="matmul",
    )(*args)
    if not riders:
        return outs[0]
    cast = [o.reshape(stack.shape[1:]) for o, (stack, _) in zip(outs[1:], side)]
    return (outs[0], cast, outs[-1]) if zero_fill is not None else (outs[0], cast)


def _sgu_gate_body(u_ref, v_ref, gain_ref, vbias_ref, ws_ref, bs_ref, o_ref, *, chunks):
    v = v_ref[...].astype(jnp.float32)
    mu = jnp.mean(v, axis=-1, keepdims=True)
    vc = v - mu
    var = jnp.mean(vc * vc, axis=-1, keepdims=True)
    vn = (vc * lax.rsqrt(var + EPS) * gain_ref[...] + vbias_ref[...]).astype(jnp.bfloat16)
    n_groups = ws_ref.shape[0]
    t_idx = lax.broadcasted_iota(jnp.int32, (SGU_CHUNK, SGU_CHUNK), 0)
    s_idx = lax.broadcasted_iota(jnp.int32, (SGU_CHUNK, SGU_CHUNK), 1)
    causal = s_idx <= t_idx
    for g in range(n_groups):
        cols = slice(g * SGU_GROUP_DIM, (g + 1) * SGU_GROUP_DIM)
        w = jnp.where(causal, ws_ref[g], jnp.zeros((), ws_ref.dtype))
        for c in range(chunks):
            rows = slice(c * SGU_CHUNK, (c + 1) * SGU_CHUNK)
            vm = jnp.dot(w, vn[rows, cols], preferred_element_type=jnp.float32) + bs_ref[:, cols]
            o_ref[rows, cols] = (u_ref[rows, cols].astype(jnp.float32) * vm).astype(o_ref.dtype)


def _sgu_gate(z, v_gain, v_bias, w_s, b_s, chunks=2):
    n, two_w = z.shape
    width = two_w // 2
    rows = chunks * SGU_CHUNK
    n_groups = w_s.shape[0]
    b_full = jnp.repeat(b_s.T, SGU_GROUP_DIM, axis=1)
    body = functools.partial(_sgu_gate_body, chunks=chunks)
    return pl.pallas_call(
        body,
        grid=(n // rows,),
        in_specs=[pl.BlockSpec((rows, width), lambda i: (i, 0)),
                  pl.BlockSpec((rows, width), lambda i: (i, 1)),
                  pl.BlockSpec((1, width), lambda i: (0, 0)),
                  pl.BlockSpec((1, width), lambda i: (0, 0)),
                  pl.BlockSpec((n_groups, SGU_CHUNK, SGU_CHUNK), lambda i: (0, 0, 0)),
                  pl.BlockSpec((SGU_CHUNK, width), lambda i: (0, 0))],
        out_specs=pl.BlockSpec((rows, width), lambda i: (i, 0)),
        out_shape=jax.ShapeDtypeStruct((n, width), jnp.bfloat16),
        compiler_params=_params(("parallel",)),
        name="sgu_gate",
    )(z, z, v_gain.reshape(1, width), v_bias.reshape(1, width), w_s.astype(jnp.bfloat16), b_full)


def _t5_causal_bucket(dist):
    max_exact = NUM_BUCKETS // 2
    d = jnp.maximum(dist, 0)
    df = jnp.maximum(d, 1).astype(jnp.float32)
    large = max_exact + (jnp.log(df / max_exact) / math.log(MAX_DISTANCE / max_exact)
                         * (NUM_BUCKETS - max_exact)).astype(jnp.int32)
    large = jnp.minimum(large, NUM_BUCKETS - 1)
    return jnp.where(d < max_exact, d, large)


def _band_bias(bias_table, window, dilation):
    span = window // dilation
    qi = jnp.arange(DIL_BLOCK)[:, None]
    kj = jnp.arange(2 * DIL_BLOCK)[None, :]
    sub_dist = qi + DIL_BLOCK - kj
    band = (sub_dist >= 0) & (sub_dist <= span)
    bucket = _t5_causal_bucket(sub_dist * dilation)
    onehot = (bucket[:, :, None] == jnp.arange(NUM_BUCKETS)).astype(jnp.float32)
    bias = jnp.einsum('qkb,bh->hqk', onehot, bias_table.astype(jnp.float32), precision=lax.Precision.HIGHEST)
    bias = jnp.where(band[None], bias, MASK_VALUE)
    return jnp.stack([bias, jnp.where(kj[None] < DIL_BLOCK, MASK_VALUE, bias)])


def _attn_body(q_ref, kp_ref, kc_ref, vp_ref, vc_ref, bias_ref, o_ref, lse_ref):
    for qb in range(q_ref.shape[2] // DIL_BLOCK):
        rows = slice(qb * DIL_BLOCK, (qb + 1) * DIL_BLOCK)
        before = slice((qb - 1) * DIL_BLOCK, qb * DIL_BLOCK)
        table = jnp.where(pl.program_id(2) == 0, 1, 0) if qb == 0 else 0
        lse_ref[0, 0, rows, :] = jnp.zeros((DIL_BLOCK, LANES), jnp.float32)
        for h in range(DIL_HEADS):
            cols = slice(h * HEAD_DIM, (h + 1) * HEAD_DIM)
            q = q_ref[0, 0, rows, cols]
            k_prev = kp_ref[0, 0, :, cols] if qb == 0 else kc_ref[0, 0, before, cols]
            v_prev = vp_ref[0, 0, :, cols] if qb == 0 else vc_ref[0, 0, before, cols]
            k = jnp.concatenate([k_prev, kc_ref[0, 0, rows, cols]], axis=0)
            v = jnp.concatenate([v_prev, vc_ref[0, 0, rows, cols]], axis=0)
            s = lax.dot_general(q, k, (((1,), (1,)), ((), ())), preferred_element_type=jnp.float32)
            s = s + bias_ref[table, h]
            m = jnp.max(s, axis=-1, keepdims=True)
            p = jnp.exp(s - m)
            den = jnp.sum(p, axis=-1, keepdims=True)
            o = jnp.dot(p.astype(jnp.bfloat16), v, preferred_element_type=jnp.float32) / den
            o_ref[0, 0, rows, cols] = o.astype(o_ref.dtype)
            lse_ref[0, 0, rows, h:h + 1] = m + jnp.log(den)


def _dilated_group(qkv, band_bias):
    batch, dilation, sub_len, _ = qkv.shape
    nb = sub_len // DIL_BLOCK
    per_step = next(c for c in (4, 2, 1) if nb % c == 0)
    rows = per_step * DIL_BLOCK

    def spec(part, prev):
        if prev:
            return pl.BlockSpec((1, 1, DIL_BLOCK, DIL_INNER),
                                lambda b, r, n: (b, r, jnp.maximum(per_step * n - 1, 0), part))
        return pl.BlockSpec((1, 1, rows, DIL_INNER), lambda b, r, n: (b, r, n, part))

    return pl.pallas_call(
        _attn_body,
        grid=(batch, dilation, nb // per_step),
        in_specs=[spec(0, False), spec(1, True), spec(1, False), spec(2, True), spec(2, False),
                  pl.BlockSpec((2, DIL_HEADS, DIL_BLOCK, 2 * DIL_BLOCK), lambda b, r, n: (0, 0, 0, 0))],
        out_specs=[pl.BlockSpec((1, 1, rows, DIL_INNER), lambda b, r, n: (b, r, n, 0)),
                   pl.BlockSpec((1, 1, rows, LANES), lambda b, r, n: (b, r, n, 0))],
        out_shape=[jax.ShapeDtypeStruct((batch, dilation, sub_len, DIL_INNER), jnp.bfloat16),
                   jax.ShapeDtypeStruct((batch, dilation, sub_len, LANES), jnp.float32)],
        compiler_params=_params(("parallel", "parallel", "arbitrary")),
        name="dilated_attn",
    )(qkv, qkv, qkv, qkv, qkv, band_bias)


def _merge_body(*refs, dilations):
    n_g = len(dilations)
    o_refs, l_refs = refs[:n_g], refs[n_g:2 * n_g]
    out_ref, ltok, wtok, acc = refs[2 * n_g:]
    rows = out_ref.shape[0]

    def residue_rows(r, d):
        return pl.ds(r, rows // d, stride=d) if d > 1 else slice(None)

    for g, d in enumerate(dilations):
        for r in range(d):
            ltok[g, residue_rows(r, d), :] = l_refs[g][0, r]
    lse = [ltok[g] for g in range(n_g)]
    m = functools.reduce(jnp.maximum, lse)
    e = [jnp.exp(l - m) for l in lse]
    tot = functools.reduce(jnp.add, e)
    for g in range(n_g):
        wtok[g] = e[g] / tot
    for g, d in enumerate(dilations):
        for r in range(d):
            rr = residue_rows(r, d)
            w = wtok[g, rr, :]
            for h in range(DIL_HEADS):
                cols = slice(h * HEAD_DIM, (h + 1) * HEAD_DIM)
                term = o_refs[g][0, r, :, cols].astype(jnp.float32) * w[:, h:h + 1]
                if g == 0:
                    acc[h, rr, :] = term
                else:
                    acc[h, rr, :] = acc[h, rr, :] + term
    for h in range(DIL_HEADS):
        out_ref[:, h * HEAD_DIM:(h + 1) * HEAD_DIM] = acc[h].astype(out_ref.dtype)


def _merge_groups(outs, lses, rows=512):
    batch = outs[0].shape[0]
    dilations = tuple(o.shape[1] for o in outs)
    seq = outs[0].shape[1] * outs[0].shape[2]
    tiles = seq // rows

    def spec(d, width):
        return pl.BlockSpec((1, d, rows // d, width), lambda i: (i // tiles, 0, i % tiles, 0))

    n_g = len(outs)
    return pl.pallas_call(
        functools.partial(_merge_body, dilations=dilations),
        grid=(batch * tiles,),
        in_specs=[spec(d, DIL_INNER) for d in dilations] + [spec(d, LANES) for d in dilations],
        out_specs=pl.BlockSpec((rows, DIL_INNER), lambda i: (i, 0)),
        out_shape=jax.ShapeDtypeStruct((batch * seq, DIL_INNER), jnp.bfloat16),
        scratch_shapes=[pltpu.VMEM((n_g, rows, LANES), jnp.float32),
                        pltpu.VMEM((n_g, rows, LANES), jnp.float32),
                        pltpu.VMEM((DIL_HEADS, rows, HEAD_DIM), jnp.float32)],
        compiler_params=_params(("parallel",)),
        name="merge_groups",
    )(*outs, *lses)


def _pack_halves(v):
    c = v.shape[1] // 2
    r = v.astype(jnp.bfloat16).astype(jnp.float32)
    lo = lax.bitcast_convert_type(r[:, :c], jnp.uint32)
    hi = lax.bitcast_convert_type(r[:, c:], jnp.uint32)
    return (lo >> 16) | (hi & jnp.uint32(0xFFFF0000))


def _unpack_halves(w):
    lo = lax.bitcast_convert_type(w << 16, jnp.float32)
    hi = lax.bitcast_convert_type(w & jnp.uint32(0xFFFF0000), jnp.float32)
    return jnp.concatenate([lo, hi], axis=1)


def _router_body(x_ref, g_ref, w_ref, b_ref, o_ref, hp_ref, cnt_ref, carry):
    i = pl.program_id(0)

    @pl.when(i == 0)
    def _():
        carry[...] = jnp.zeros_like(carry)

    x = x_ref[...]
    ms = jnp.mean(x * x, axis=-1, keepdims=True)
    hf = x * lax.rsqrt(ms + EPS) * g_ref[...]
    hp_ref[...] = _pack_halves(hf)
    logits = jnp.dot(hf.astype(jnp.bfloat16), w_ref[...], preferred_element_type=jnp.float32) + b_ref[...]
    rows = logits.shape[0]
    lane = lax.broadcasted_iota(jnp.int32, logits.shape, 1)
    neg = -jnp.inf
    gl = jnp.where(lane < N_EXPERT_GROUPS, logits, neg)
    gmax = jnp.max(gl, axis=-1, keepdims=True)
    g_idx = jnp.min(jnp.where(gl == gmax, lane, LANES), axis=-1, keepdims=True)
    g_w = 1.0 / jnp.sum(jnp.exp(gl - gmax), axis=-1, keepdims=True)
    lo = N_EXPERT_GROUPS + EXPERTS_PER_GROUP * g_idx
    el = jnp.where(jnp.logical_and(lane >= lo, lane < lo + EXPERTS_PER_GROUP), logits, neg)
    v1 = jnp.max(el, axis=-1, keepdims=True)
    i1 = jnp.min(jnp.where(el == v1, lane, LANES), axis=-1, keepdims=True)
    el2 = jnp.where(lane == i1, neg, el)
    v2 = jnp.max(el2, axis=-1, keepdims=True)
    i2 = jnp.min(jnp.where(el2 == v2, lane, LANES), axis=-1, keepdims=True)
    e2 = jnp.exp(v2 - v1)
    p1 = 1.0 / (1.0 + e2)
    p2 = e2 / (1.0 + e2)
    oh1 = lane == i1 - N_EXPERT_GROUPS
    oh2 = lane == i2 - N_EXPERT_GROUPS
    oh1f, oh2f = oh1.astype(jnp.float32), oh2.astype(jnp.float32)
    before = (lax.broadcasted_iota(jnp.int32, (rows, rows), 1)
              < lax.broadcasted_iota(jnp.int32, (rows, rows), 0)).astype(jnp.bfloat16)
    pre1 = jnp.dot(before, oh1f.astype(jnp.bfloat16), preferred_element_type=jnp.float32)
    pre2 = jnp.dot(before, oh2f.astype(jnp.bfloat16), preferred_element_type=jnp.float32)
    cnt1 = jnp.sum(oh1f, axis=0, keepdims=True)
    cnt2 = jnp.sum(oh2f, axis=0, keepdims=True)
    base = carry[...]
    rank1 = jnp.sum(jnp.where(oh1, pre1 + base, 0.0), axis=-1, keepdims=True)
    rank2 = jnp.sum(jnp.where(oh2, pre2 + (base + cnt1), 0.0), axis=-1, keepdims=True)
    total = base + cnt1 + cnt2
    carry[...] = total
    cnt_ref[...] = total
    out = jnp.where(lane == 0, (i1 - N_EXPERT_GROUPS).astype(jnp.float32),
          jnp.where(lane == 1, (i2 - N_EXPERT_GROUPS).astype(jnp.float32),
          jnp.where(lane == 2, g_w * p1,
          jnp.where(lane == 3, g_w * p2,
          jnp.where(lane == 4, rank1,
          jnp.where(lane == 5, rank2, 0.0))))))
    o_ref[...] = out


def _router(x, g, w_rg, b_rg, w_re, b_re, rows=512):
    n, d = x.shape
    pad = LANES - N_EXPERT_GROUPS - N_EXPERTS
    w = jnp.concatenate([w_rg, w_re, jnp.zeros((d, pad), w_rg.dtype)], axis=1).astype(jnp.bfloat16)
    b = jnp.concatenate([b_rg, b_re, jnp.zeros((pad,), b_rg.dtype)]).reshape(1, LANES)
    return pl.pallas_call(
        _router_body,
        grid=(n // rows,),
        in_specs=[pl.BlockSpec((rows, d), lambda i: (i, 0)),
                  pl.BlockSpec((1, d), lambda i: (0, 0)),
                  pl.BlockSpec((d, LANES), lambda i: (0, 0)),
                  pl.BlockSpec((1, LANES), lambda i: (0, 0))],
        out_specs=[pl.BlockSpec((rows, LANES), lambda i: (i, 0)),
                   pl.BlockSpec((rows, d // 2), lambda i: (i, 0)),
                   pl.BlockSpec((1, LANES), lambda i: (0, 0))],
        out_shape=[jax.ShapeDtypeStruct((n, LANES), jnp.float32),
                   jax.ShapeDtypeStruct((n, d // 2), jnp.uint32),
                   jax.ShapeDtypeStruct((1, LANES), jnp.float32)],
        scratch_shapes=[pltpu.VMEM((1, LANES), jnp.float32)],
        compiler_params=_params(("arbitrary",)),
        name="router",
    )(x, g.reshape(1, d), w, b)


def _dispatch_body(s1_ref, s2_ref, hp_ref, xb_in_hbm, xb_hbm, stage, sem):
    del xb_in_hbm
    i = pl.program_id(0)
    n_steps = pl.num_programs(0)
    rows = s1_ref.shape[2]
    slot = i % 2

    def wait_slot(of_slot):
        for _ in range(TOP_K):
            pltpu.make_async_copy(stage.at[of_slot], stage.at[of_slot], sem.at[of_slot]).wait()

    @pl.when(i >= 2)
    def _():
        wait_slot(slot)

    stage[slot] = hp_ref[...]

    def issue(r, carry):
        src = stage.at[slot, pl.ds(r, 1)]
        pltpu.make_async_copy(src, xb_hbm.at[pl.ds(s1_ref[0, 0, r], 1)], sem.at[slot]).start()
        pltpu.make_async_copy(src, xb_hbm.at[pl.ds(s2_ref[0, 0, r], 1)], sem.at[slot]).start()
        return carry
    lax.fori_loop(0, rows, issue, 0, unroll=8)

    @pl.when(i == n_steps - 1)
    def _():
        @pl.when(i >= 1)
        def _():
            wait_slot(1 - slot)
        wait_slot(slot)


def _dispatch(hp, slot1, slot2, xb_init, rows=512):
    n, c = hp.shape
    p_total = xb_init.shape[0]
    steps = n // rows
    tab = pl.BlockSpec((1, 1, rows), lambda i: (i, 0, 0), memory_space=pltpu.SMEM)
    any_spec = pl.BlockSpec(memory_space=pl.ANY)
    return pl.pallas_call(
        _dispatch_body,
        grid=(steps,),
        in_specs=[tab, tab, pl.BlockSpec((rows, c), lambda i: (i, 0)), any_spec],
        out_specs=any_spec,
        out_shape=jax.ShapeDtypeStruct((p_total, c), hp.dtype),
        scratch_shapes=[pltpu.VMEM((2, rows, c), hp.dtype),
                        pltpu.SemaphoreType.DMA((2,))],
        input_output_aliases={3: 0},
        compiler_params=_params(("arbitrary",)),
        name="moe_dispatch",
    )(slot1.reshape(steps, 1, rows), slot2.reshape(steps, 1, rows), hp, xb_init)


def _side_cast_specs(side, n_steps, index_of_step, dtype):
    in_specs, args, out_specs, out_shape = [], [], [], []
    for stack, layer in side:
        cols = stack.shape[-1]
        flat = stack.reshape(-1, cols)
        layer_rows = flat.shape[0] // stack.shape[0]
        n_slabs = 1
        while (n_slabs * 2 <= n_steps and layer_rows % (n_slabs * 2) == 0
               and (layer_rows // (n_slabs * 2)) % 16 == 0):
            n_slabs *= 2
        slab_rows = layer_rows // n_slabs

        def slab(*idx, last=n_slabs - 1):
            return jnp.minimum(index_of_step(*idx), last)
        in_specs.append(pl.BlockSpec((slab_rows, cols),
                                     lambda *idx, slab=slab, first=layer * n_slabs: (first + slab(*idx), 0)))
        args.append(flat)
        out_specs.append(pl.BlockSpec((slab_rows, cols), lambda *idx, slab=slab: (slab(*idx), 0)))
        out_shape.append(jax.ShapeDtypeStruct((layer_rows, cols), dtype))
    return in_specs, args, out_specs, out_shape


def _cast_body(*refs):
    n_side = len(refs) // 2
    for s_in, s_out in zip(refs[:n_side], refs[n_side:]):
        s_out[...] = s_in[...].astype(s_out.dtype)


def _cast_layers(side, dtype, n_steps=32):
    s_in, s_args, s_out, s_shape = _side_cast_specs(side, n_steps, lambda i: i, dtype)
    outs = pl.pallas_call(
        _cast_body,
        grid=(n_steps,),
        in_specs=s_in,
        out_specs=s_out,
        out_shape=s_shape,
        compiler_params=_params(("arbitrary",)),
        name="cast",
    )(*s_args)
    return [o.reshape(stack.shape[1:]) for o, (stack, _) in zip(outs, side)]


def _expert_body(blk_e_ref, nvalid_ref, xb_ref, wg_ref, wu_ref, wd_ref, *refs):
    del blk_e_ref
    n_side = (len(refs) - 1) // 2
    side_in, yb_ref, side_out = refs[:n_side], refs[n_side], refs[n_side + 1:]
    i = pl.program_id(0)
    for s_in, s_out in zip(side_in, side_out):
        s_out[...] = s_in[...].astype(s_out.dtype)

    @pl.when(nvalid_ref[i] > 0)
    def _():
        h = _unpack_halves(xb_ref[...]).astype(jnp.bfloat16)
        a = jax.nn.silu(jnp.dot(h, wg_ref[0], preferred_element_type=jnp.float32))
        a = a * jnp.dot(h, wu_ref[0], preferred_element_type=jnp.float32)
        y = jnp.dot(a.astype(jnp.bfloat16), wd_ref[0], preferred_element_type=jnp.float32)
        yb_ref[...] = _pack_halves(y)

    @pl.when(nvalid_ref[i] == 0)
    def _():
        yb_ref[...] = jnp.zeros_like(yb_ref)


def _moe_experts(xb, blk_e, nvalid, w_gate, w_up, w_down, side=()):
    p_total, c = xb.shape
    d = 2 * c
    d_e = w_gate.shape[-1]
    n_blk = p_total // MOE_ROWS
    s_in, s_args, s_out, s_shape = _side_cast_specs(side, n_blk, lambda i, *_: i, jnp.bfloat16)
    grid_spec = pltpu.PrefetchScalarGridSpec(
        num_scalar_prefetch=2,
        grid=(n_blk,),
        in_specs=[pl.BlockSpec((MOE_ROWS, c), lambda i, e, nv: (i, 0)),
                  pl.BlockSpec((1, d, d_e), lambda i, e, nv: (e[i], 0, 0)),
                  pl.BlockSpec((1, d, d_e), lambda i, e, nv: (e[i], 0, 0)),
                  pl.BlockSpec((1, d_e, d), lambda i, e, nv: (e[i], 0, 0))] + s_in,
        out_specs=[pl.BlockSpec((MOE_ROWS, c), lambda i, e, nv: (i, 0))] + s_out,
    )
    outs = pl.pallas_call(
        _expert_body,
        grid_spec=grid_spec,
        out_shape=[jax.ShapeDtypeStruct((p_total, c), xb.dtype)] + s_shape,
        compiler_params=_params(("arbitrary",)),
        name="moe_experts",
    )(blk_e, nvalid, xb, w_gate, w_up, w_down, *s_args)
    return outs[0], [o.reshape(stack.shape[1:]) for o, (stack, _) in zip(outs[1:], side)]


def _combine_body(s1_ref, s2_ref, s1n_ref, s2n_ref, x_ref, routed_ref, g_ref, yb_hbm, *refs, emit_sum):
    out_refs, ybuf, sem = refs[:-2], refs[-2], refs[-1]
    i = pl.program_id(0)
    n_steps = pl.num_programs(0)
    slot = i % 2
    rows = x_ref.shape[0]

    def gather(t1_ref, t2_ref, to_slot):
        def issue(r, carry):
            pltpu.make_async_copy(yb_hbm.at[pl.ds(t1_ref[0, 0, r], 1)], ybuf.at[to_slot, pl.ds(r, 1)],
                                  sem.at[to_slot]).start()
            pltpu.make_async_copy(yb_hbm.at[pl.ds(t2_ref[0, 0, r], 1)], ybuf.at[to_slot, pl.ds(rows + r, 1)],
                                  sem.at[to_slot]).start()
            return carry
        lax.fori_loop(0, rows, issue, 0, unroll=8)

    @pl.when(i == 0)
    def _():
        gather(s1_ref, s2_ref, 0)

    @pl.when(i + 1 < n_steps)
    def _():
        gather(s1n_ref, s2n_ref, 1 - slot)

    pltpu.make_async_copy(ybuf.at[slot], ybuf.at[slot], sem.at[slot]).wait()
    y1 = _unpack_halves(ybuf[slot, :rows])
    y2 = _unpack_halves(ybuf[slot, rows:])
    s = x_ref[...] + (y1 * routed_ref[:, 2:3] + y2 * routed_ref[:, 3:4])
    if emit_sum:
        out_refs[0][...] = s
    ms = jnp.mean(s * s, axis=-1, keepdims=True)
    out_refs[-1][...] = (s * lax.rsqrt(ms + EPS) * g_ref[...]).astype(out_refs[-1].dtype)


def _combine_norm(x, yb, routed, slot1, slot2, g, norm_dtype, emit_sum, rows=256):
    n, d = x.shape
    steps = n // rows
    row_spec = pl.BlockSpec((rows, d), lambda i: (i, 0))
    tab = pl.BlockSpec((1, 1, rows), lambda i: (i, 0, 0), memory_space=pltpu.SMEM)
    tab_next = pl.BlockSpec((1, 1, rows), lambda i: (jnp.minimum(i + 1, steps - 1), 0, 0),
                            memory_space=pltpu.SMEM)
    out_specs = [row_spec]
    out_shape = [jax.ShapeDtypeStruct((n, d), norm_dtype)]
    if emit_sum:
        out_specs = [row_spec] + out_specs
        out_shape = [jax.ShapeDtypeStruct((n, d), x.dtype)] + out_shape
    s1 = slot1.reshape(steps, 1, rows)
    s2 = slot2.reshape(steps, 1, rows)
    return pl.pallas_call(
        functools.partial(_combine_body, emit_sum=emit_sum),
        grid=(steps,),
        in_specs=[tab, tab, tab_next, tab_next, row_spec,
                  pl.BlockSpec((rows, LANES), lambda i: (i, 0)),
                  pl.BlockSpec((1, d), lambda i: (0, 0)),
                  pl.BlockSpec(memory_space=pl.ANY)],
        out_specs=out_specs,
        out_shape=out_shape,
        scratch_shapes=[pltpu.VMEM((2, TOP_K * rows, d // 2), yb.dtype),
                        pltpu.SemaphoreType.DMA((2,))],
        compiler_params=_params(("arbitrary",)),
        name="combine_norm",
    )(s1, s2, s1, s2, x, routed, g.reshape(1, d), yb)


def _moe_slot_rows(n):
    return n * TOP_K + N_EXPERTS * MOE_ROWS


def _moe_layer(x, ffn_g, next_g, norm_dtype, emit_sum, w_rg, b_rg, w_re, b_re, w_gate, w_up, w_down, xb_init,
               side=()):
    n, d = x.shape
    routed, hp, counts = _router(x, ffn_g, w_rg, b_rg, w_re, b_re)

    n_blk = _moe_slot_rows(n) // MOE_ROWS
    experts = jnp.arange(N_EXPERTS, dtype=jnp.int32)
    counts = counts[0, :N_EXPERTS].astype(jnp.int32)
    padded = ((counts + MOE_ROWS - 1) // MOE_ROWS) * MOE_ROWS
    pends = jnp.cumsum(padded)
    pstarts = pends - padded
    expert_id = routed[:, 0:TOP_K].astype(jnp.int32)
    rank = routed[:, 2 * TOP_K:3 * TOP_K].astype(jnp.int32)
    slot = jnp.sum(jnp.where(expert_id[:, :, None] == experts, pstarts, 0), axis=-1) + rank
    blk_start = jnp.arange(n_blk, dtype=jnp.int32) * MOE_ROWS
    blk_e = jnp.minimum(jnp.sum(blk_start[:, None] >= pends[None, :], axis=1), N_EXPERTS - 1).astype(jnp.int32)
    seg_end = jnp.sum(jnp.where(blk_e[:, None] == experts, pstarts + counts, 0), axis=1)
    nvalid = jnp.clip(seg_end - blk_start, 0, MOE_ROWS).astype(jnp.int32)

    xb = _dispatch(hp, slot[:, 0], slot[:, 1], xb_init)
    yb, cast = _moe_experts(xb, blk_e, nvalid, w_gate, w_up, w_down, side)
    return _combine_norm(x, yb, routed, slot[:, 0], slot[:, 1], next_g, norm_dtype, emit_sum), cast, xb


def kernel(x, mix_norm, ffn_norm, sgu_w_in, sgu_b_in, sgu_v_gain, sgu_v_bias, sgu_w_spatial, sgu_b_spatial,
           sgu_w_out, dil_w_qkv, dil_w_out, rel_bias, router_w_group, router_b_group, router_w_expert,
           router_b_expert, moe_w_gate, moe_w_up, moe_w_down, final_norm):
    batch, seq, d = x.shape
    n = batch * seq
    bf16 = jnp.bfloat16
    xf = x.reshape(n, d)
    depth = mix_norm.shape[0]
    n_dil = len(DIL_CONFIGS)
    h = _rmsnorm(xf, mix_norm[0], bf16)
    out = None
    dense = {}
    xb_buf = None
    for i in range(depth):
        j = i // 2
        xb_shape = (_moe_slot_rows(n), d // 2, jnp.uint32) if xb_buf is None else None
        if i % 2 == 0:
            riders = ((moe_w_gate, i), (moe_w_up, i))
            if i in dense:
                w_in, w_out = dense[i]
            else:
                (w_in,), w_out = _cast_layers(((sgu_w_in, j),), bf16), None
                riders += ((sgu_w_out, j),)
            res = _matmul(h, w_in, bias=sgu_b_in[j], use_gelu=True, out_dtype=bf16, side=riders, zero_fill=xb_shape)
            z, (w_gate, w_up, *rest) = res[:2]
            w_out = rest[0] if rest else w_out
            if xb_shape is not None:
                xb_buf = res[2]
            y = _sgu_gate(z, sgu_v_gain[j], sgu_v_bias[j], sgu_w_spatial[j], sgu_b_spatial[j])
            xf, (w_down,) = _matmul(y, w_out, residual=xf, out_dtype=jnp.float32, side=((moe_w_down, i),))
        else:
            if i not in dense:
                dense[i] = tuple(_cast_layers(((dil_w_qkv, j), (dil_w_out, j)), bf16))
            w_qkv, w_out = dense[i]
            outs, lses, cast = [], [], []
            tn = 512
            per_part = DIL_INNER // tn
            for g, (window, dilation) in enumerate(DIL_CONFIGS):
                res = _matmul(
                    h, w_qkv, out_dtype=bf16, n_out=3 * DIL_INNER, tn=tn,
                    col_block_map=lambda c, g=g: ((c // per_part) * n_dil + g) * per_part + c % per_part,
                    col_scale=(per_part, HEAD_DIM ** -0.5),
                    regroup=(batch, seq, dilation), side=(((moe_w_gate, moe_w_up, moe_w_down)[g], i),),
                    zero_fill=xb_shape if g == 0 else None)
                qkv, (w_cast,) = res[:2]
                if g == 0 and xb_shape is not None:
                    xb_buf = res[2]
                cast.append(w_cast)
                bias = _band_bias(rel_bias[:, g * DIL_HEADS:(g + 1) * DIL_HEADS], window, dilation)
                o, lse = _dilated_group(qkv, bias)
                outs.append(o)
                lses.append(lse)
            w_gate, w_up, w_down = cast
            merged = _merge_groups(outs, lses)
            xf = _matmul(merged, w_out, residual=xf, out_dtype=jnp.float32)
        last = i + 1 == depth
        side = ()
        if not last:
            j_next = (i + 1) // 2
            side = (((sgu_w_in, j_next), (sgu_w_out, j_next)) if (i + 1) % 2 == 0
                    else ((dil_w_qkv, j_next), (dil_w_out, j_next)))
        res, cast, xb_buf = _moe_layer(xf, ffn_norm[i], final_norm if last else mix_norm[i + 1],
                                       x.dtype if last else bf16, not last,
                                       router_w_group[i], router_b_group[i], router_w_expert[i], router_b_expert[i],
                                       w_gate, w_up, w_down, xb_buf, side)
        if cast:
            dense[i + 1] = tuple(cast)
        if last:
            (out,) = res
        else:
            xf, h = res
    return out.reshape(batch, seq, d)
```

```python
import functools
import math

import jax
import jax.numpy as jnp
from jax import lax
from jax.experimental import pallas as pl
from jax.experimental.pallas import tpu as pltpu

EPS = 1e-6
SGU_CHUNK = 128
SGU_GROUP_DIM = 128
DIL_CONFIGS = ((128, 1), (512, 4), (2048, 16))
DIL_HEADS = 16
HEAD_DIM = 128
DIL_BLOCK = 128
DIL_INNER = DIL_HEADS * HEAD_DIM
NUM_BUCKETS = 32
MAX_DISTANCE = 2048
N_EXPERT_GROUPS = 4
EXPERTS_PER_GROUP = 8
N_EXPERTS = N_EXPERT_GROUPS * EXPERTS_PER_GROUP
TOP_K = 2
MASK_VALUE = -1e30

LANES = 128
MOE_ROWS = 256
VMEM_LIMIT = 52 * 1024 * 1024
W_RING_SLOTS = 3
W_RING_BYTES = 12 * 1024 * 1024


def _params(sem, vmem=VMEM_LIMIT):
    return pltpu.CompilerParams(dimension_semantics=sem, vmem_limit_bytes=vmem)


def _rmsnorm_body(x_ref, g_ref, o_ref):
    x = x_ref[...]
    ms = jnp.mean(x * x, axis=-1, keepdims=True)
    o_ref[...] = (x * lax.rsqrt(ms + EPS) * g_ref[...]).astype(o_ref.dtype)


def _rmsnorm(x, g, out_dtype, rows=512):
    n, d = x.shape
    return pl.pallas_call(
        _rmsnorm_body,
        grid=(n // rows,),
        in_specs=[pl.BlockSpec((rows, d), lambda i: (i, 0)),
                  pl.BlockSpec((1, d), lambda i: (0, 0))],
        out_specs=pl.BlockSpec((rows, d), lambda i: (i, 0)),
        out_shape=jax.ShapeDtypeStruct((n, d), out_dtype),
        compiler_params=_params(("parallel",)),
        name="rmsnorm",
    )(x, g.reshape(1, d))


def _regroup_pitch(dilation):
    return dilation + 4 if dilation % 8 == 0 else dilation


def _matmul_body(*refs, has_bias, use_gelu, has_residual, dilation, n_side, has_zero, col_scale, w_col_of):
    a_ref, w_ref = refs[0], refs[1]
    pos = 2
    if w_col_of is None:
        w = w_ref[...]
    else:
        ring, sem = refs[-2], refs[-1]
        nj = pl.num_programs(1)
        step = pl.program_id(0) * nj + pl.program_id(1)
        n_steps = pl.num_programs(0) * nj
        tn = ring.shape[2]

        def w_copy(of_step):
            col = pl.multiple_of(w_col_of(lax.rem(of_step, nj)) * tn, tn)
            slot = lax.rem(of_step, W_RING_SLOTS)
            return pltpu.make_async_copy(w_ref.at[:, pl.ds(col, tn)], ring.at[slot], sem.at[slot])

        @pl.when(step == 0)
        def _():
            for ahead in range(W_RING_SLOTS - 1):
                @pl.when(ahead < n_steps)
                def _():
                    w_copy(ahead).start()

        @pl.when(step + W_RING_SLOTS - 1 < n_steps)
        def _():
            w_copy(step + W_RING_SLOTS - 1).start()

        w_copy(step).wait()
        w = ring[lax.rem(step, W_RING_SLOTS)]
    first_side = 2 + int(has_bias) + int(has_residual)
    for s_in, s_out in zip(refs[first_side:first_side + n_side],
                           refs[first_side + n_side + 1:first_side + 2 * n_side + 1]):
        s_out[...] = s_in[...].astype(s_out.dtype)
    if has_zero:
        zero_out = refs[first_side + 2 * n_side + 1]
        zero_out[...] = jnp.zeros_like(zero_out)
    acc = jnp.dot(a_ref[...], w.astype(a_ref.dtype), preferred_element_type=jnp.float32)
    if col_scale is not None:
        blocks, scale = col_scale
        acc = acc * jnp.where(pl.program_id(1) < blocks, jnp.float32(scale), jnp.float32(1.0))
    if has_bias:
        acc = acc + refs[pos][...]
        pos += 1
    if use_gelu:
        acc = jax.nn.gelu(acc)
    if has_residual:
        acc = refs[pos][...] + acc
        pos += 1
    o_ref = refs[pos + n_side]
    pos += 2 * n_side + 1 + int(has_zero)
    if dilation == 1:
        o_ref[...] = acc.astype(o_ref.dtype).reshape(o_ref.shape)
    else:
        acc_ref = refs[pos]
        sub = acc.shape[0] // dilation
        pitch = _regroup_pitch(dilation)
        for panel in range(acc_ref.shape[0]):
            cols = slice(panel * LANES, (panel + 1) * LANES)
            if pitch == dilation:
                acc_ref[panel] = acc[:, cols]
            else:
                for l in range(sub):
                    acc_ref[panel, l * pitch:l * pitch + dilation, :] = acc[l * dilation:(l + 1) * dilation, cols]
            for r in range(dilation):
                o_ref[0, r, :, cols] = acc_ref[panel, pl.ds(r, sub, stride=pitch), :].astype(o_ref.dtype)


def _matmul(a, w, *, bias=None, use_gelu=False, residual=None, out_dtype, col_block_map=None, n_out=None,
            regroup=None, side=(), zero_fill=None, col_scale=None, tm=1024, tn=512):
    m, k = a.shape
    n = w.shape[1] if n_out is None else n_out
    tm, tn = min(tm, m), min(tn, n)
    grid = (m // tm, n // tn)
    w_col = (lambda j: j) if col_block_map is None else col_block_map
    use_ring = W_RING_SLOTS * k * tn * w.dtype.itemsize <= W_RING_BYTES
    w_spec = pl.BlockSpec(memory_space=pl.ANY) if use_ring else pl.BlockSpec((k, tn), lambda i, j: (0, w_col(j)))
    in_specs = [pl.BlockSpec((tm, k), lambda i, j: (i, 0)), w_spec]
    args = [a, w]
    if bias is not None:
        in_specs.append(pl.BlockSpec((1, tn), lambda i, j: (0, j)))
        args.append(bias.reshape(1, n))
    if residual is not None:
        in_specs.append(pl.BlockSpec((tm, tn), lambda i, j: (i, j)))
        args.append(residual)
    scratch = []
    dilation = 1
    if regroup is None:
        out_specs = [pl.BlockSpec((tm, tn), lambda i, j: (i, j))]
        out_shape = [jax.ShapeDtypeStruct((m, n), out_dtype)]
    else:
        batch, seq, dilation = regroup
        tiles = seq // tm
        out_specs = [pl.BlockSpec((1, dilation, tm // dilation, tn), lambda i, j: (i // tiles, 0, i % tiles, j))]
        out_shape = [jax.ShapeDtypeStruct((batch, dilation, seq // dilation, n), out_dtype)]
        if dilation > 1:
            scratch = [pltpu.VMEM((tn // LANES, tm // dilation * _regroup_pitch(dilation), LANES), jnp.float32)]
    n_steps = grid[0] * grid[1]

    def step_of(i, j):
        return i * grid[1] + j
    s_in, s_args, s_out, s_shape = _side_cast_specs(side, n_steps, step_of, a.dtype)
    in_specs += s_in
    args += s_args
    out_specs += s_out
    out_shape += s_shape
    if zero_fill is not None:
        z_rows, z_cols, z_dtype = zero_fill
        zero_slabs = 1
        while zero_slabs * 2 <= n_steps and z_rows % (zero_slabs * 2) == 0 and (z_rows // (zero_slabs * 2)) % 8 == 0:
            zero_slabs *= 2
        out_specs.append(pl.BlockSpec((z_rows // zero_slabs, z_cols),
                                      lambda i, j: (jnp.minimum(step_of(i, j), zero_slabs - 1), 0)))
        out_shape.append(jax.ShapeDtypeStruct((z_rows, z_cols), z_dtype))
    if use_ring:
        scratch += [pltpu.VMEM((W_RING_SLOTS, k, tn), w.dtype), pltpu.SemaphoreType.DMA((W_RING_SLOTS,))]
    body = functools.partial(_matmul_body, has_bias=bias is not None, use_gelu=use_gelu,
                             has_residual=residual is not None, dilation=dilation, n_side=len(side),
                             has_zero=zero_fill is not None, col_scale=col_scale,
                             w_col_of=w_col if use_ring else None)
    riders = bool(side) or zero_fill is not None
    sequential = riders or use_ring
    outs = pl.pallas_call(
        body,
        grid=grid,
        in_specs=in_specs,
        out_specs=out_specs,
        out_shape=out_shape,
        scratch_shapes=scratch,
        compiler_params=_params(("arbitrary", "arbitrary") if sequential else ("parallel", "arbitrary")),
        name="matmul",
    )(*args)
    if not riders:
        return outs[0]
    cast = [o.reshape(stack.shape[1:]) for o, (stack, _) in zip(outs[1:], side)]
    return (outs[0], cast, outs[-1]) if zero_fill is not None else (outs[0], cast)


def _sgu_gate_body(u_ref, v_ref, gain_ref, vbias_ref, ws_ref, bs_ref, o_ref, *, chunks):
    v = v_ref[...].astype(jnp.float32)
    mu = jnp.mean(v, axis=-1, keepdims=True)
    vc = v - mu
    var = jnp.mean(vc * vc, axis=-1, keepdims=True)
    vn = (vc * lax.rsqrt(var + EPS) * gain_ref[...] + vbias_ref[...]).astype(jnp.bfloat16)
    n_groups = ws_ref.shape[0]
    t_idx = lax.broadcasted_iota(jnp.int32, (SGU_CHUNK, SGU_CHUNK), 0)
    s_idx = lax.broadcasted_iota(jnp.int32, (SGU_CHUNK, SGU_CHUNK), 1)
    causal = s_idx <= t_idx
    for g in range(n_groups):
        cols = slice(g * SGU_GROUP_DIM, (g + 1) * SGU_GROUP_DIM)
        w = jnp.where(causal, ws_ref[g], jnp.zeros((), ws_ref.dtype))
        for c in range(chunks):
            rows = slice(c * SGU_CHUNK, (c + 1) * SGU_CHUNK)
            vm = jnp.dot(w, vn[rows, cols], preferred_element_type=jnp.float32) + bs_ref[:, cols]
            o_ref[rows, cols] = (u_ref[rows, cols].astype(jnp.float32) * vm).astype(o_ref.dtype)


def _sgu_gate(z, v_gain, v_bias, w_s, b_s, chunks=4):
    n, two_w = z.shape
    width = two_w // 2
    rows = chunks * SGU_CHUNK
    n_groups = w_s.shape[0]
    b_full = jnp.repeat(b_s.T, SGU_GROUP_DIM, axis=1)
    body = functools.partial(_sgu_gate_body, chunks=chunks)
    return pl.pallas_call(
        body,
        grid=(n // rows,),
        in_specs=[pl.BlockSpec((rows, width), lambda i: (i, 0)),
                  pl.BlockSpec((rows, width), lambda i: (i, 1)),
                  pl.BlockSpec((1, width), lambda i: (0, 0)),
                  pl.BlockSpec((1, width), lambda i: (0, 0)),
                  pl.BlockSpec((n_groups, SGU_CHUNK, SGU_CHUNK), lambda i: (0, 0, 0)),
                  pl.BlockSpec((SGU_CHUNK, width), lambda i: (0, 0))],
        out_specs=pl.BlockSpec((rows, width), lambda i: (i, 0)),
        out_shape=jax.ShapeDtypeStruct((n, width), jnp.bfloat16),
        compiler_params=_params(("parallel",)),
        name="sgu_gate",
    )(z, z, v_gain.reshape(1, width), v_bias.reshape(1, width), w_s.astype(jnp.bfloat16), b_full)


def _t5_causal_bucket(dist):
    max_exact = NUM_BUCKETS // 2
    d = jnp.maximum(dist, 0)
    df = jnp.maximum(d, 1).astype(jnp.float32)
    large = max_exact + (jnp.log(df / max_exact) / math.log(MAX_DISTANCE / max_exact)
                         * (NUM_BUCKETS - max_exact)).astype(jnp.int32)
    large = jnp.minimum(large, NUM_BUCKETS - 1)
    return jnp.where(d < max_exact, d, large)


def _band_bias(bias_table, window, dilation):
    span = window // dilation
    qi = jnp.arange(DIL_BLOCK)[:, None]
    kj = jnp.arange(2 * DIL_BLOCK)[None, :]
    sub_dist = qi + DIL_BLOCK - kj
    band = (sub_dist >= 0) & (sub_dist <= span)
    bucket = _t5_causal_bucket(sub_dist * dilation)
    onehot = (bucket[:, :, None] == jnp.arange(NUM_BUCKETS)).astype(jnp.float32)
    bias = jnp.einsum('qkb,bh->hqk', onehot, bias_table.astype(jnp.float32), precision=lax.Precision.HIGHEST)
    bias = jnp.where(band[None], bias, MASK_VALUE)
    return jnp.stack([bias, jnp.where(kj[None] < DIL_BLOCK, MASK_VALUE, bias)])


def _attn_body(q_ref, kp_ref, kc_ref, vp_ref, vc_ref, bias_ref, o_ref, lse_ref):
    for qb in range(q_ref.shape[2] // DIL_BLOCK):
        rows = slice(qb * DIL_BLOCK, (qb + 1) * DIL_BLOCK)
        before = slice((qb - 1) * DIL_BLOCK, qb * DIL_BLOCK)
        table = jnp.where(pl.program_id(2) == 0, 1, 0) if qb == 0 else 0
        lse_ref[0, 0, rows, :] = jnp.zeros((DIL_BLOCK, LANES), jnp.float32)
        for h in range(DIL_HEADS):
            cols = slice(h * HEAD_DIM, (h + 1) * HEAD_DIM)
            q = q_ref[0, 0, rows, cols]
            k_prev = kp_ref[0, 0, :, cols] if qb == 0 else kc_ref[0, 0, before, cols]
            v_prev = vp_ref[0, 0, :, cols] if qb == 0 else vc_ref[0, 0, before, cols]
            k = jnp.concatenate([k_prev, kc_ref[0, 0, rows, cols]], axis=0)
            v = jnp.concatenate([v_prev, vc_ref[0, 0, rows, cols]], axis=0)
            s = lax.dot_general(q, k, (((1,), (1,)), ((), ())), preferred_element_type=jnp.float32)
            s = s + bias_ref[table, h]
            m = jnp.max(s, axis=-1, keepdims=True)
            p = jnp.exp(s - m)
            den = jnp.sum(p, axis=-1, keepdims=True)
            o = jnp.dot(p.astype(jnp.bfloat16), v, preferred_element_type=jnp.float32) / den
            o_ref[0, 0, rows, cols] = o.astype(o_ref.dtype)
            lse_ref[0, 0, rows, h:h + 1] = m + jnp.log(den)


def _dilated_group(qkv, band_bias):
    batch, dilation, sub_len, _ = qkv.shape
    nb = sub_len // DIL_BLOCK
    per_step = next(c for c in (4, 2, 1) if nb % c == 0)
    rows = per_step * DIL_BLOCK

    def spec(part, prev):
        if prev:
            return pl.BlockSpec((1, 1, DIL_BLOCK, DIL_INNER),
                                lambda b, r, n: (b, r, jnp.maximum(per_step * n - 1, 0), part))
        return pl.BlockSpec((1, 1, rows, DIL_INNER), lambda b, r, n: (b, r, n, part))

    return pl.pallas_call(
        _attn_body,
        grid=(batch, dilation, nb // per_step),
        in_specs=[spec(0, False), spec(1, True), spec(1, False), spec(2, True), spec(2, False),
                  pl.BlockSpec((2, DIL_HEADS, DIL_BLOCK, 2 * DIL_BLOCK), lambda b, r, n: (0, 0, 0, 0))],
        out_specs=[pl.BlockSpec((1, 1, rows, DIL_INNER), lambda b, r, n: (b, r, n, 0)),
                   pl.BlockSpec((1, 1, rows, LANES), lambda b, r, n: (b, r, n, 0))],
        out_shape=[jax.ShapeDtypeStruct((batch, dilation, sub_len, DIL_INNER), jnp.bfloat16),
                   jax.ShapeDtypeStruct((batch, dilation, sub_len, LANES), jnp.float32)],
        compiler_params=_params(("parallel", "parallel", "arbitrary")),
        name="dilated_attn",
    )(qkv, qkv, qkv, qkv, qkv, band_bias)


def _merge_body(*refs, dilations):
    n_g = len(dilations)
    o_refs, l_refs = refs[:n_g], refs[n_g:2 * n_g]
    out_ref, ltok, wtok, acc = refs[2 * n_g:]
    rows = out_ref.shape[0]

    def residue_rows(r, d):
        return pl.ds(r, rows // d, stride=d) if d > 1 else slice(None)

    for g, d in enumerate(dilations):
        for r in range(d):
            ltok[g, residue_rows(r, d), :] = l_refs[g][0, r]
    lse = [ltok[g] for g in range(n_g)]
    m = functools.reduce(jnp.maximum, lse)
    e = [jnp.exp(l - m) for l in lse]
    tot = functools.reduce(jnp.add, e)
    for g in range(n_g):
        wtok[g] = e[g] / tot
    for g, d in enumerate(dilations):
        for r in range(d):
            rr = residue_rows(r, d)
            w = wtok[g, rr, :]
            for h in range(DIL_HEADS):
                cols = slice(h * HEAD_DIM, (h + 1) * HEAD_DIM)
                term = o_refs[g][0, r, :, cols].astype(jnp.float32) * w[:, h:h + 1]
                if g == 0:
                    acc[h, rr, :] = term
                else:
                    acc[h, rr, :] = acc[h, rr, :] + term
    for h in range(DIL_HEADS):
        out_ref[:, h * HEAD_DIM:(h + 1) * HEAD_DIM] = acc[h].astype(out_ref.dtype)


def _merge_groups(outs, lses, rows=512):
    batch = outs[0].shape[0]
    dilations = tuple(o.shape[1] for o in outs)
    seq = outs[0].shape[1] * outs[0].shape[2]
    tiles = seq // rows

    def spec(d, width):
        return pl.BlockSpec((1, d, rows // d, width), lambda i: (i // tiles, 0, i % tiles, 0))

    n_g = len(outs)
    return pl.pallas_call(
        functools.partial(_merge_body, dilations=dilations),
        grid=(batch * tiles,),
        in_specs=[spec(d, DIL_INNER) for d in dilations] + [spec(d, LANES) for d in dilations],
        out_specs=pl.BlockSpec((rows, DIL_INNER), lambda i: (i, 0)),
        out_shape=jax.ShapeDtypeStruct((batch * seq, DIL_INNER), jnp.bfloat16),
        scratch_shapes=[pltpu.VMEM((n_g, rows, LANES), jnp.float32),
                        pltpu.VMEM((n_g, rows, LANES), jnp.float32),
                        pltpu.VMEM((DIL_HEADS, rows, HEAD_DIM), jnp.float32)],
        compiler_params=_params(("parallel",)),
        name="merge_groups",
    )(*outs, *lses)


def _pack_halves(v):
    c = v.shape[1] // 2
    r = v.astype(jnp.bfloat16).astype(jnp.float32)
    lo = lax.bitcast_convert_type(r[:, :c], jnp.uint32)
    hi = lax.bitcast_convert_type(r[:, c:], jnp.uint32)
    return (lo >> 16) | (hi & jnp.uint32(0xFFFF0000))


def _unpack_halves(w):
    lo = lax.bitcast_convert_type(w << 16, jnp.float32)
    hi = lax.bitcast_convert_type(w & jnp.uint32(0xFFFF0000), jnp.float32)
    return jnp.concatenate([lo, hi], axis=1)


def _router_body(x_ref, g_ref, w_ref, b_ref, o_ref, hp_ref, cnt_ref, carry):
    i = pl.program_id(0)

    @pl.when(i == 0)
    def _():
        carry[...] = jnp.zeros_like(carry)

    x = x_ref[...]
    ms = jnp.mean(x * x, axis=-1, keepdims=True)
    hf = x * lax.rsqrt(ms + EPS) * g_ref[...]
    hp_ref[...] = _pack_halves(hf)
    logits = jnp.dot(hf.astype(jnp.bfloat16), w_ref[...], preferred_element_type=jnp.float32) + b_ref[...]
    rows = logits.shape[0]
    lane = lax.broadcasted_iota(jnp.int32, logits.shape, 1)
    neg = -jnp.inf
    gl = jnp.where(lane < N_EXPERT_GROUPS, logits, neg)
    gmax = jnp.max(gl, axis=-1, keepdims=True)
    g_idx = jnp.min(jnp.where(gl == gmax, lane, LANES), axis=-1, keepdims=True)
    g_w = 1.0 / jnp.sum(jnp.exp(gl - gmax), axis=-1, keepdims=True)
    lo = N_EXPERT_GROUPS + EXPERTS_PER_GROUP * g_idx
    el = jnp.where(jnp.logical_and(lane >= lo, lane < lo + EXPERTS_PER_GROUP), logits, neg)
    v1 = jnp.max(el, axis=-1, keepdims=True)
    i1 = jnp.min(jnp.where(el == v1, lane, LANES), axis=-1, keepdims=True)
    el2 = jnp.where(lane == i1, neg, el)
    v2 = jnp.max(el2, axis=-1, keepdims=True)
    i2 = jnp.min(jnp.where(el2 == v2, lane, LANES), axis=-1, keepdims=True)
    e2 = jnp.exp(v2 - v1)
    p1 = 1.0 / (1.0 + e2)
    p2 = e2 / (1.0 + e2)
    oh1 = lane == i1 - N_EXPERT_GROUPS
    oh2 = lane == i2 - N_EXPERT_GROUPS
    oh1f, oh2f = oh1.astype(jnp.float32), oh2.astype(jnp.float32)
    before = (lax.broadcasted_iota(jnp.int32, (rows, rows), 1)
              < lax.broadcasted_iota(jnp.int32, (rows, rows), 0)).astype(jnp.bfloat16)
    pre1 = jnp.dot(before, oh1f.astype(jnp.bfloat16), preferred_element_type=jnp.float32)
    pre2 = jnp.dot(before, oh2f.astype(jnp.bfloat16), preferred_element_type=jnp.float32)
    cnt1 = jnp.sum(oh1f, axis=0, keepdims=True)
    cnt2 = jnp.sum(oh2f, axis=0, keepdims=True)
    base = carry[...]
    rank1 = jnp.sum(jnp.where(oh1, pre1 + base, 0.0), axis=-1, keepdims=True)
    rank2 = jnp.sum(jnp.where(oh2, pre2 + (base + cnt1), 0.0), axis=-1, keepdims=True)
    total = base + cnt1 + cnt2
    carry[...] = total
    cnt_ref[...] = total
    out = jnp.where(lane == 0, (i1 - N_EXPERT_GROUPS).astype(jnp.float32),
          jnp.where(lane == 1, (i2 - N_EXPERT_GROUPS).astype(jnp.float32),
          jnp.where(lane == 2, g_w * p1,
          jnp.where(lane == 3, g_w * p2,
          jnp.where(lane == 4, rank1,
          jnp.where(lane == 5, rank2, 0.0))))))
    o_ref[...] = out


def _router(x, g, w_rg, b_rg, w_re, b_re, rows=512):
    n, d = x.shape
    pad = LANES - N_EXPERT_GROUPS - N_EXPERTS
    w = jnp.concatenate([w_rg, w_re, jnp.zeros((d, pad), w_rg.dtype)], axis=1).astype(jnp.bfloat16)
    b = jnp.concatenate([b_rg, b_re, jnp.zeros((pad,), b_rg.dtype)]).reshape(1, LANES)
    return pl.pallas_call(
        _router_body,
        grid=(n // rows,),
        in_specs=[pl.BlockSpec((rows, d), lambda i: (i, 0)),
                  pl.BlockSpec((1, d), lambda i: (0, 0)),
                  pl.BlockSpec((d, LANES), lambda i: (0, 0)),
                  pl.BlockSpec((1, LANES), lambda i: (0, 0))],
        out_specs=[pl.BlockSpec((rows, LANES), lambda i: (i, 0)),
                   pl.BlockSpec((rows, d // 2), lambda i: (i, 0)),
                   pl.BlockSpec((1, LANES), lambda i: (0, 0))],
        out_shape=[jax.ShapeDtypeStruct((n, LANES), jnp.float32),
                   jax.ShapeDtypeStruct((n, d // 2), jnp.uint32),
                   jax.ShapeDtypeStruct((1, LANES), jnp.float32)],
        scratch_shapes=[pltpu.VMEM((1, LANES), jnp.float32)],
        compiler_params=_params(("arbitrary",)),
        name="router",
    )(x, g.reshape(1, d), w, b)


def _dispatch_body(s1_ref, s2_ref, hp_ref, xb_in_hbm, xb_hbm, stage, sem):
    del xb_in_hbm
    i = pl.program_id(0)
    n_steps = pl.num_programs(0)
    rows = s1_ref.shape[2]
    slot = i % 2

    def wait_slot(of_slot):
        for _ in range(TOP_K):
            pltpu.make_async_copy(stage.at[of_slot], stage.at[of_slot], sem.at[of_slot]).wait()

    @pl.when(i >= 2)
    def _():
        wait_slot(slot)

    stage[slot] = hp_ref[...]

    def issue(r, carry):
        src = stage.at[slot, pl.ds(r, 1)]
        pltpu.make_async_copy(src, xb_hbm.at[pl.ds(s1_ref[0, 0, r], 1)], sem.at[slot]).start()
        pltpu.make_async_copy(src, xb_hbm.at[pl.ds(s2_ref[0, 0, r], 1)], sem.at[slot]).start()
        return carry
    lax.fori_loop(0, rows, issue, 0, unroll=8)

    @pl.when(i == n_steps - 1)
    def _():
        @pl.when(i >= 1)
        def _():
            wait_slot(1 - slot)
        wait_slot(slot)


def _dispatch(hp, slot1, slot2, xb_init, rows=512):
    n, c = hp.shape
    p_total = xb_init.shape[0]
    steps = n // rows
    tab = pl.BlockSpec((1, 1, rows), lambda i: (i, 0, 0), memory_space=pltpu.SMEM)
    any_spec = pl.BlockSpec(memory_space=pl.ANY)
    return pl.pallas_call(
        _dispatch_body,
        grid=(steps,),
        in_specs=[tab, tab, pl.BlockSpec((rows, c), lambda i: (i, 0)), any_spec],
        out_specs=any_spec,
        out_shape=jax.ShapeDtypeStruct((p_total, c), hp.dtype),
        scratch_shapes=[pltpu.VMEM((2, rows, c), hp.dtype),
                        pltpu.SemaphoreType.DMA((2,))],
        input_output_aliases={3: 0},
        compiler_params=_params(("arbitrary",)),
        name="moe_dispatch",
    )(slot1.reshape(steps, 1, rows), slot2.reshape(steps, 1, rows), hp, xb_init)


def _side_cast_specs(side, n_steps, index_of_step, dtype):
    in_specs, args, out_specs, out_shape = [], [], [], []
    for stack, layer in side:
        cols = stack.shape[-1]
        flat = stack.reshape(-1, cols)
        layer_rows = flat.shape[0] // stack.shape[0]
        n_slabs = 1
        while (n_slabs * 2 <= n_steps and layer_rows % (n_slabs * 2) == 0
               and (layer_rows // (n_slabs * 2)) % 16 == 0):
            n_slabs *= 2
        slab_rows = layer_rows // n_slabs

        def slab(*idx, last=n_slabs - 1):
            return jnp.minimum(index_of_step(*idx), last)
        in_specs.append(pl.BlockSpec((slab_rows, cols),
                                     lambda *idx, slab=slab, first=layer * n_slabs: (first + slab(*idx), 0)))
        args.append(flat)
        out_specs.append(pl.BlockSpec((slab_rows, cols), lambda *idx, slab=slab: (slab(*idx), 0)))
        out_shape.append(jax.ShapeDtypeStruct((layer_rows, cols), dtype))
    return in_specs, args, out_specs, out_shape


def _cast_body(*refs):
    n_side = len(refs) // 2
    for s_in, s_out in zip(refs[:n_side], refs[n_side:]):
        s_out[...] = s_in[...].astype(s_out.dtype)


def _cast_layers(side, dtype, n_steps=32):
    s_in, s_args, s_out, s_shape = _side_cast_specs(side, n_steps, lambda i: i, dtype)
    outs = pl.pallas_call(
        _cast_body,
        grid=(n_steps,),
        in_specs=s_in,
        out_specs=s_out,
        out_shape=s_shape,
        compiler_params=_params(("arbitrary",)),
        name="cast",
    )(*s_args)
    return [o.reshape(stack.shape[1:]) for o, (stack, _) in zip(outs, side)]


def _expert_body(blk_e_ref, nvalid_ref, xb_ref, wg_ref, wu_ref, wd_ref, *refs):
    del blk_e_ref
    n_side = (len(refs) - 1) // 2
    side_in, yb_ref, side_out = refs[:n_side], refs[n_side], refs[n_side + 1:]
    i = pl.program_id(0)
    for s_in, s_out in zip(side_in, side_out):
        s_out[...] = s_in[...].astype(s_out.dtype)

    @pl.when(nvalid_ref[i] > 0)
    def _():
        h = _unpack_halves(xb_ref[...]).astype(jnp.bfloat16)
        a = jax.nn.silu(jnp.dot(h, wg_ref[0], preferred_element_type=jnp.float32))
        a = a * jnp.dot(h, wu_ref[0], preferred_element_type=jnp.float32)
        y = jnp.dot(a.astype(jnp.bfloat16), wd_ref[0], preferred_element_type=jnp.float32)
        yb_ref[...] = _pack_halves(y)

    @pl.when(nvalid_ref[i] == 0)
    def _():
        yb_ref[...] = jnp.zeros_like(yb_ref)


def _moe_experts(xb, blk_e, nvalid, w_gate, w_up, w_down, side=()):
    p_total, c = xb.shape
    d = 2 * c
    d_e = w_gate.shape[-1]
    n_blk = p_total // MOE_ROWS
    s_in, s_args, s_out, s_shape = _side_cast_specs(side, n_blk, lambda i, *_: i, jnp.bfloat16)
    grid_spec = pltpu.PrefetchScalarGridSpec(
        num_scalar_prefetch=2,
        grid=(n_blk,),
        in_specs=[pl.BlockSpec((MOE_ROWS, c), lambda i, e, nv: (i, 0)),
                  pl.BlockSpec((1, d, d_e), lambda i, e, nv: (e[i], 0, 0)),
                  pl.BlockSpec((1, d, d_e), lambda i, e, nv: (e[i], 0, 0)),
                  pl.BlockSpec((1, d_e, d), lambda i, e, nv: (e[i], 0, 0))] + s_in,
        out_specs=[pl.BlockSpec((MOE_ROWS, c), lambda i, e, nv: (i, 0))] + s_out,
    )
    outs = pl.pallas_call(
        _expert_body,
        grid_spec=grid_spec,
        out_shape=[jax.ShapeDtypeStruct((p_total, c), xb.dtype)] + s_shape,
        compiler_params=_params(("arbitrary",)),
        name="moe_experts",
    )(blk_e, nvalid, xb, w_gate, w_up, w_down, *s_args)
    return outs[0], [o.reshape(stack.shape[1:]) for o, (stack, _) in zip(outs[1:], side)]


def _combine_body(s1_ref, s2_ref, s1n_ref, s2n_ref, x_ref, routed_ref, g_ref, yb_hbm, *refs, emit_sum):
    out_refs, ybuf, sem = refs[:-2], refs[-2], refs[-1]
    i = pl.program_id(0)
    n_steps = pl.num_programs(0)
    slot = i % 2
    rows = x_ref.shape[0]

    def gather(t1_ref, t2_ref, to_slot):
        def issue(r, carry):
            pltpu.make_async_copy(yb_hbm.at[pl.ds(t1_ref[0, 0, r], 1)], ybuf.at[to_slot, pl.ds(r, 1)],
                                  sem.at[to_slot]).start()
            pltpu.make_async_copy(yb_hbm.at[pl.ds(t2_ref[0, 0, r], 1)], ybuf.at[to_slot, pl.ds(rows + r, 1)],
                                  sem.at[to_slot]).start()
            return carry
        lax.fori_loop(0, rows, issue, 0, unroll=8)

    @pl.when(i == 0)
    def _():
        gather(s1_ref, s2_ref, 0)

    @pl.when(i + 1 < n_steps)
    def _():
        gather(s1n_ref, s2n_ref, 1 - slot)

    pltpu.make_async_copy(ybuf.at[slot], ybuf.at[slot], sem.at[slot]).wait()
    y1 = _unpack_halves(ybuf[slot, :rows])
    y2 = _unpack_halves(ybuf[slot, rows:])
    s = x_ref[...] + (y1 * routed_ref[:, 2:3] + y2 * routed_ref[:, 3:4])
    if emit_sum:
        out_refs[0][...] = s
    ms = jnp.mean(s * s, axis=-1, keepdims=True)
    out_refs[-1][...] = (s * lax.rsqrt(ms + EPS) * g_ref[...]).astype(out_refs[-1].dtype)


def _combine_norm(x, yb, routed, slot1, slot2, g, norm_dtype, emit_sum, rows=256):
    n, d = x.shape
    steps = n // rows
    row_spec = pl.BlockSpec((rows, d), lambda i: (i, 0))
    tab = pl.BlockSpec((1, 1, rows), lambda i: (i, 0, 0), memory_space=pltpu.SMEM)
    tab_next = pl.BlockSpec((1, 1, rows), lambda i: (jnp.minimum(i + 1, steps - 1), 0, 0),
                            memory_space=pltpu.SMEM)
    out_specs = [row_spec]
    out_shape = [jax.ShapeDtypeStruct((n, d), norm_dtype)]
    if emit_sum:
        out_specs = [row_spec] + out_specs
        out_shape = [jax.ShapeDtypeStruct((n, d), x.dtype)] + out_shape
    s1 = slot1.reshape(steps, 1, rows)
    s2 = slot2.reshape(steps, 1, rows)
    return pl.pallas_call(
        functools.partial(_combine_body, emit_sum=emit_sum),
        grid=(steps,),
        in_specs=[tab, tab, tab_next, tab_next, row_spec,
                  pl.BlockSpec((rows, LANES), lambda i: (i, 0)),
                  pl.BlockSpec((1, d), lambda i: (0, 0)),
                  pl.BlockSpec(memory_space=pl.ANY)],
        out_specs=out_specs,
        out_shape=out_shape,
        scratch_shapes=[pltpu.VMEM((2, TOP_K * rows, d // 2), yb.dtype),
                        pltpu.SemaphoreType.DMA((2,))],
        compiler_params=_params(("arbitrary",)),
        name="combine_norm",
    )(s1, s2, s1, s2, x, routed, g.reshape(1, d), yb)


def _moe_slot_rows(n):
    return n * TOP_K + N_EXPERTS * MOE_ROWS


def _moe_layer(x, ffn_g, next_g, norm_dtype, emit_sum, w_rg, b_rg, w_re, b_re, w_gate, w_up, w_down, xb_init,
               side=()):
    n, d = x.shape
    routed, hp, counts = _router(x, ffn_g, w_rg, b_rg, w_re, b_re)

    n_blk = _moe_slot_rows(n) // MOE_ROWS
    experts = jnp.arange(N_EXPERTS, dtype=jnp.int32)
    counts = counts[0, :N_EXPERTS].astype(jnp.int32)
    padded = ((counts + MOE_ROWS - 1) // MOE_ROWS) * MOE_ROWS
    pends = jnp.cumsum(padded)
    pstarts = pends - padded
    expert_id = routed[:, 0:TOP_K].astype(jnp.int32)
    rank = routed[:, 2 * TOP_K:3 * TOP_K].astype(jnp.int32)
    slot = jnp.sum(jnp.where(expert_id[:, :, None] == experts, pstarts, 0), axis=-1) + rank
    blk_start = jnp.arange(n_blk, dtype=jnp.int32) * MOE_ROWS
    blk_e = jnp.minimum(jnp.sum(blk_start[:, None] >= pends[None, :], axis=1), N_EXPERTS - 1).astype(jnp.int32)
    seg_end = jnp.sum(jnp.where(blk_e[:, None] == experts, pstarts + counts, 0), axis=1)
    nvalid = jnp.clip(seg_end - blk_start, 0, MOE_ROWS).astype(jnp.int32)

    xb = _dispatch(hp, slot[:, 0], slot[:, 1], xb_init)
    yb, cast = _moe_experts(xb, blk_e, nvalid, w_gate, w_up, w_down, side)
    return _combine_norm(x, yb, routed, slot[:, 0], slot[:, 1], next_g, norm_dtype, emit_sum), cast, xb


def kernel(x, mix_norm, ffn_norm, sgu_w_in, sgu_b_in, sgu_v_gain, sgu_v_bias, sgu_w_spatial, sgu_b_spatial,
           sgu_w_out, dil_w_qkv, dil_w_out, rel_bias, router_w_group, router_b_group, router_w_expert,
           router_b_expert, moe_w_gate, moe_w_up, moe_w_down, final_norm):
    batch, seq, d = x.shape
    n = batch * seq
    bf16 = jnp.bfloat16
    xf = x.reshape(n, d)
    depth = mix_norm.shape[0]
    n_dil = len(DIL_CONFIGS)
    h = _rmsnorm(xf, mix_norm[0], bf16)
    out = None
    dense = {}
    xb_buf = None
    for i in range(depth):
        j = i // 2
        xb_shape = (_moe_slot_rows(n), d // 2, jnp.uint32) if xb_buf is None else None
        if i % 2 == 0:
            riders = ((moe_w_gate, i), (moe_w_up, i))
            if i in dense:
                w_in, w_out = dense[i]
            else:
                (w_in,), w_out = _cast_layers(((sgu_w_in, j),), bf16), None
                riders += ((sgu_w_out, j),)
            res = _matmul(h, w_in, bias=sgu_b_in[j], use_gelu=True, out_dtype=bf16, side=riders, zero_fill=xb_shape)
            z, (w_gate, w_up, *rest) = res[:2]
            w_out = rest[0] if rest else w_out
            if xb_shape is not None:
                xb_buf = res[2]
            y = _sgu_gate(z, sgu_v_gain[j], sgu_v_bias[j], sgu_w_spatial[j], sgu_b_spatial[j])
            xf, (w_down,) = _matmul(y, w_out, residual=xf, out_dtype=jnp.float32, side=((moe_w_down, i),))
        else:
            if i not in dense:
                dense[i] = tuple(_cast_layers(((dil_w_qkv, j), (dil_w_out, j)), bf16))
            w_qkv, w_out = dense[i]
            outs, lses, cast = [], [], []
            tn = 512
            per_part = DIL_INNER // tn
            for g, (window, dilation) in enumerate(DIL_CONFIGS):
                res = _matmul(
                    h, w_qkv, out_dtype=bf16, n_out=3 * DIL_INNER, tn=tn,
                    col_block_map=lambda c, g=g: ((c // per_part) * n_dil + g) * per_part + c % per_part,
                    col_scale=(per_part, HEAD_DIM ** -0.5),
                    regroup=(batch, seq, dilation), side=(((moe_w_gate, moe_w_up, moe_w_down)[g], i),),
                    zero_fill=xb_shape if g == 0 else None)
                qkv, (w_cast,) = res[:2]
                if g == 0 and xb_shape is not None:
                    xb_buf = res[2]
                cast.append(w_cast)
                bias = _band_bias(rel_bias[:, g * DIL_HEADS:(g + 1) * DIL_HEADS], window, dilation)
                o, lse = _dilated_group(qkv, bias)
                outs.append(o)
                lses.append(lse)
            w_gate, w_up, w_down = cast
            merged = _merge_groups(outs, lses)
            xf = _matmul(merged, w_out, residual=xf, out_dtype=jnp.float32)
        last = i + 1 == depth
        side = ()
        if not last:
            j_next = (i + 1) // 2
            side = (((sgu_w_in, j_next), (sgu_w_out, j_next)) if (i + 1) % 2 == 0
                    else ((dil_w_qkv, j_next), (dil_w_out, j_next)))
        res, cast, xb_buf = _moe_layer(xf, ffn_norm[i], final_norm if last else mix_norm[i + 1],
                                       x.dtype if last else bf16, not last,
                                       router_w_group[i], router_b_group[i], router_w_expert[i], router_b_expert[i],
                                       w_gate, w_up, w_down, xb_buf, side)
        if cast:
            dense[i + 1] = tuple(cast)
        if last:
            (out,) = res
        else:
            xf, h = res
    return out.reshape(batch, seq, d)
```
